```python
import math
import jax, jax.numpy as jnp
from jax import lax
import numpy as np

D_MODEL = 1024
BATCH = 32
SEQ = 2048
DEPTH = 2

GRID_W = 64
CTX_LEN = 256
EPS = 1e-6

MLA_HEADS = 4
MLA_Q_LORA = 256
MLA_KV_LORA = 128
MLA_NOPE = 128
MLA_ROPE = 64
MLA_V = 128
MLA_WIDTH = MLA_HEADS * MLA_V
ROPE_BASE = 10000.0
ATTN_BLOCK = 128

HG_HEADS = 4
HG_K = 128
HG_V = 64
HG_WIDTH = HG_HEADS * HG_V
HG_CHUNK = 64

FN_GROUPS = 4
FN_GROUP_DIM = 64
FN_WIDTH = FN_GROUPS * FN_GROUP_DIM

MIX_WIDTH = MLA_WIDTH + HG_WIDTH + FN_WIDTH
FFN_HIDDEN = -(-8 * D_MODEL // (3 * 256)) * 256

O_CQ = 0
O_CKV = O_CQ + MLA_Q_LORA
O_KR = O_CKV + MLA_KV_LORA
O_HQ = O_KR + MLA_ROPE
O_HFF = O_HQ + HG_HEADS * HG_K
O_HFB = O_HFF + HG_HEADS * HG_K
O_HI = O_HFB + HG_HEADS * HG_K
O_HG = O_HI + HG_WIDTH
O_FN = O_HG + HG_WIDTH
IN_WIDTH = O_FN + FN_WIDTH

kernel_name = 'hybrid_mla_hgrn2_fnet_dit'


def rms_norm(x, g):
    xf = x.astype(jnp.float32)
    y = xf * lax.rsqrt(jnp.mean(xf * xf, axis=-1, keepdims=True) + EPS)
    return (y * g.astype(jnp.float32)).astype(x.dtype)


def modulate(x, g, shift, scale):
    return rms_norm(x, g) * (1 + scale) + shift


def axial_rope(n_tokens, dtype):
    rows = n_tokens // GRID_W
    row_pos = jnp.repeat(jnp.arange(rows, dtype=jnp.float32), GRID_W)
    col_pos = jnp.tile(jnp.arange(GRID_W, dtype=jnp.float32), rows)
    axis_dim = MLA_ROPE // 2
    inv_freq = ROPE_BASE ** (-jnp.arange(0, axis_dim, 2, dtype=jnp.float32) / axis_dim)
    ang_r = row_pos[:, None] * inv_freq
    ang_c = col_pos[:, None] * inv_freq
    ang = jnp.concatenate([ang_r, ang_r, ang_c, ang_c], axis=-1)
    return (jnp.cos(ang).astype(dtype)[None, :, None, :], jnp.sin(ang).astype(dtype)[None, :, None, :])


def apply_rope(x, cos, sin):
    r1, r2, c1, c2 = jnp.split(x, 4, axis=-1)
    rot = jnp.concatenate([-r2, r1, -c2, c1], axis=-1)
    return x * cos + rot * sin


def mla_queries(p, q_norm_g, w_uq, rope):
    b, t, _ = p.shape
    cq = rms_norm(p[..., O_CQ:O_CKV], q_norm_g)
    q = (cq @ w_uq).reshape(b, t, MLA_HEADS, MLA_NOPE + MLA_ROPE)
    q_nope, q_pe = q[..., :MLA_NOPE], q[..., MLA_NOPE:]
    if rope is not None:
        q_pe = apply_rope(q_pe, rope[0], rope[1])
    return jnp.concatenate([q_nope, q_pe], axis=-1)


def mla_keys_values(p, kv_norm_g, w_ukv, rope):
    b, t, _ = p.shape
    ckv = rms_norm(p[..., O_CKV:O_KR], kv_norm_g)
    kv = (ckv @ w_ukv).reshape(b, t, MLA_HEADS, MLA_NOPE + MLA_V)
    k_nope, v = kv[..., :MLA_NOPE], kv[..., MLA_NOPE:]
    k_pe = p[..., O_KR:O_HQ][:, :, None, :]
    if rope is not None:
        k_pe = apply_rope(k_pe, rope[0], rope[1])
    k = jnp.concatenate([k_nope, jnp.broadcast_to(k_pe, (b, t, MLA_HEADS, MLA_ROPE))], axis=-1)
    return k, v


def block_attention(q, k, v):
    b, t, h, dk = q.shape
    nb = t // ATTN_BLOCK
    qb = q.reshape(b, nb, ATTN_BLOCK, h, dk).swapaxes(0, 1)
    scale = 1.0 / math.sqrt(dk)

    def one_block(qi):
        s = jnp.einsum('bqhd,bkhd->bhqk', qi, k).astype(jnp.float32) * scale
        pr = jax.nn.softmax(s, axis=-1).astype(v.dtype)
        return jnp.einsum('bhqk,bkhd->bqhd', pr, v)

    o = lax.map(one_block, qb)
    return o.swapaxes(0, 1).reshape(b, t, h * v.shape[-1])


def chunk_gated_scan(q, k, v, log_f, s0, with_output):
    b, t, h, kd = q.shape
    n = t // HG_CHUNK

    def to_chunks(a):
        return a.reshape(b, n, HG_CHUNK, h, a.shape[-1]).swapaxes(0, 1)

    causal = jnp.tril(jnp.ones((HG_CHUNK, HG_CHUNK), dtype=bool))[None, :, :, None, None]

    def step(state, inp):
        qc, kc, vc, gc = inp
        bcum = jnp.cumsum(gc, axis=1)
        total = bcum[:, -1]
        new_state = jnp.exp(total)[..., None] * state + jnp.einsum(
            'bshk,bshv->bhkv', kc * jnp.exp(total[:, None] - bcum), vc)
        if not with_output:
            return new_state, None
        decay = jnp.exp(jnp.where(causal, bcum[:, :, None] - bcum[:, None], -jnp.inf))
        scores = jnp.einsum('bthk,btshk,bshk->bhts', qc, decay, kc)
        o = jnp.einsum('bhts,bshv->bthv', scores, vc) + jnp.einsum(
            'bthk,bhkv->bthv', qc * jnp.exp(bcum), state)
        return new_state, o

    s_fin, o = lax.scan(step, s0, (to_chunks(q), to_chunks(k), to_chunks(v), to_chunks(log_f)))
    if with_output:
        o = o.swapaxes(0, 1).reshape(b, t, h, v.shape[-1])
    return o, s_fin


def hgrn2_inputs(p, lb_fb):
    b, t, _ = p.shape
    q = jax.nn.silu(p[..., O_HQ:O_HFF].astype(jnp.float32)).reshape(b, t, HG_HEADS, HG_K)
    i = p[..., O_HI:O_HG].astype(jnp.float32).reshape(b, t, HG_HEADS, HG_V)

    def gate(z, lb):
        f = lb + (1.0 - lb) * jax.nn.sigmoid(z.astype(jnp.float32))
        return (1.0 - f).reshape(b, t, HG_HEADS, HG_K), jnp.log(f).reshape(b, t, HG_HEADS, HG_K)

    fwd = gate(p[..., O_HFF:O_HFB], lb_fb[0])
    bwd = gate(p[..., O_HFB:O_HI], lb_fb[1])
    return q, i, fwd, bwd


def hgrn2_readout(o, z_gate, gain):
    b, t = o.shape[0], o.shape[1]
    on = o * lax.rsqrt(jnp.mean(o * o, axis=-1, keepdims=True) + EPS)
    on = on * gain.astype(jnp.float32).reshape(HG_HEADS, HG_V)
    return (on.reshape(b, t, HG_WIDTH) * jax.nn.silu(z_gate.astype(jnp.float32))).astype(z_gate.dtype)


def fourier_mix(z, w):
    b, t, _ = z.shape
    zg = z.astype(jnp.float32).reshape(b, t, FN_GROUPS, FN_GROUP_DIM)
    mixed = jnp.fft.fft2(zg, axes=(1, 3), norm='ortho').real
    return mixed.reshape(b, t, FN_WIDTH).astype(z.dtype) @ w


def swiglu_ffn(h, w_gu, w_dn):
    gate, up = jnp.split(h @ w_gu, 2, axis=-1)
    return (jax.nn.silu(gate) * up) @ w_dn


def setup_inputs(seed: int = 0) -> dict:
    key = jax.random.key(seed)
    ks = jax.random.split(key, 20)
    f32 = jnp.float32

    def nrm(k, shape, scale):
        return jax.random.normal(k, shape, f32) * scale

    def gain(k, shape):
        return 1.0 + 0.02 * jax.random.normal(k, shape, f32)

    return {
        'x': nrm(ks[0], (BATCH, SEQ, D_MODEL), 1.0),
        'c': nrm(ks[1], (BATCH, D_MODEL), 1.0),
        'ctx': nrm(ks[2], (BATCH, CTX_LEN, D_MODEL), 1.0),
        'c_ctx': nrm(ks[3], (D_MODEL,), 1.0),
        'w_mod': nrm(ks[4], (DEPTH, D_MODEL, 6 * D_MODEL), D_MODEL ** -0.5),
        'b_mod': nrm(ks[5], (DEPTH, 6 * D_MODEL), 0.02),
        'norm1_g': gain(ks[6], (DEPTH, D_MODEL)),
        'norm2_g': gain(ks[7], (DEPTH, D_MODEL)),
        'w_in': nrm(ks[8], (DEPTH, D_MODEL, IN_WIDTH), D_MODEL ** -0.5),
        'q_norm_g': gain(ks[9], (DEPTH, MLA_Q_LORA)),
        'w_uq': nrm(ks[10], (DEPTH, MLA_Q_LORA, MLA_HEADS * (MLA_NOPE + MLA_ROPE)), MLA_Q_LORA ** -0.5),
        'kv_norm_g': gain(ks[11], (DEPTH, MLA_KV_LORA)),
        'w_ukv': nrm(ks[12], (DEPTH, MLA_KV_LORA, MLA_HEADS * (MLA_NOPE + MLA_V)), MLA_KV_LORA ** -0.5),
        'lb_param': nrm(ks[13], (DEPTH, 2, HG_HEADS * HG_K), 1.0),
        'hg_norm_g': gain(ks[14], (DEPTH, HG_WIDTH)),
        'w_fourier': nrm(ks[15], (DEPTH, FN_WIDTH, FN_WIDTH), FN_WIDTH ** -0.5),
        'w_out': nrm(ks[16], (DEPTH, MIX_WIDTH, D_MODEL), MIX_WIDTH ** -0.5),
        'w_gate_up': nrm(ks[17], (DEPTH, D_MODEL, 2 * FFN_HIDDEN), D_MODEL ** -0.5),
        'w_down': nrm(ks[18], (DEPTH, FFN_HIDDEN, D_MODEL), FFN_HIDDEN ** -0.5),
        'final_norm_g': gain(ks[19], (D_MODEL,)),
    }


def reference(x, c, ctx, c_ctx, w_mod, b_mod, norm1_g, norm2_g, w_in, q_norm_g, w_uq, kv_norm_g, w_ukv,
              lb_param, hg_norm_g, w_fourier, w_out, w_gate_up, w_down, final_norm_g):
    b, t, _ = x.shape
    rope = axial_rope(t, x.dtype)
    probs = jax.nn.softmax(lb_param.astype(jnp.float32), axis=0)
    lower_bounds = jnp.cumsum(probs, axis=0) - probs[0]
    s_zero = jnp.zeros((b, HG_HEADS, HG_K, HG_V), jnp.float32)
    c_act = jax.nn.silu(c)
    cc_act = jax.nn.silu(c_ctx)

    def flip(a):
        return jnp.flip(a, axis=1)

    for l in range(DEPTH):
        last = l == DEPTH - 1
        sh1, sc1, g1, sh2, sc2, g2 = jnp.split((c_act @ w_mod[l] + b_mod[l])[:, None, :], 6, axis=-1)
        csh1, csc1, cg1, csh2, csc2, cg2 = jnp.split((cc_act @ w_mod[l] + b_mod[l])[None, None, :], 6, axis=-1)

        p = modulate(x, norm1_g[l], sh1, sc1) @ w_in[l]
        pc = modulate(ctx, norm1_g[l], csh1, csc1) @ w_in[l]

        k_lat, v_lat = mla_keys_values(p, kv_norm_g[l], w_ukv[l], rope)
        k_ctx, v_ctx = mla_keys_values(pc, kv_norm_g[l], w_ukv[l], None)
        q_lat = mla_queries(p, q_norm_g[l], w_uq[l], rope)
        attn = block_attention(q_lat, jnp.concatenate([k_lat, k_ctx], axis=1),
                               jnp.concatenate([v_lat, v_ctx], axis=1))

        lb_fb = lower_bounds[l]
        qc, ic, (kcf, lcf), (kcb, lcb) = hgrn2_inputs(pc, lb_fb)
        o_cf, s_cf = chunk_gated_scan(qc, kcf, ic, lcf, s_zero, not last)
        o_cb, s_cb = chunk_gated_scan(flip(qc), flip(kcb), flip(ic), flip(lcb), s_zero, not last)
        ql, il, (klf, llf), (klb, llb) = hgrn2_inputs(p, lb_fb)
        o_lf, _ = chunk_gated_scan(ql, klf, il, llf, s_cf, True)
        o_lb, _ = chunk_gated_scan(flip(ql), flip(klb), flip(il), flip(llb), s_cb, True)
        hg = hgrn2_readout(o_lf + flip(o_lb), p[..., O_HG:O_FN], hg_norm_g[l])

        fn = fourier_mix(p[..., O_FN:IN_WIDTH], w_fourier[l])

        y = jnp.concatenate([attn, hg, fn], axis=-1) @ w_out[l]
        x_new = x + g1 * y
        x_new = x_new + g2 * swiglu_ffn(modulate(x_new, norm2_g[l], sh2, sc2), w_gate_up[l], w_down[l])

        if not last:
            attn_c = block_attention(mla_queries(pc, q_norm_g[l], w_uq[l], None), k_ctx, v_ctx)
            hg_c = hgrn2_readout(o_cf + flip(o_cb), pc[..., O_HG:O_FN], hg_norm_g[l])
            fn_c = fourier_mix(pc[..., O_FN:IN_WIDTH], w_fourier[l])
            yc = jnp.concatenate([attn_c, hg_c, fn_c], axis=-1) @ w_out[l]
            ctx = ctx + cg1 * yc
            ctx = ctx + cg2 * swiglu_ffn(modulate(ctx, norm2_g[l], csh2, csc2), w_gate_up[l], w_down[l])
        x = x_new

    return rms_norm(x, final_norm_g)
```

```python
import functools
import math

import numpy as np
import jax
import jax.numpy as jnp
from jax import lax
from jax.experimental import pallas as pl
from jax.experimental.pallas import tpu as pltpu

EPS = 1e-6
GRID_W = 64
ROPE_BASE = 10000.0

MLA_HEADS = 4
MLA_Q_LORA = 256
MLA_KV_LORA = 128
MLA_NOPE = 128
MLA_ROPE = 64
MLA_V = 128
MLA_WIDTH = MLA_HEADS * MLA_V
HEAD_PAD = 256

HG_HEADS = 4
HG_K = 128
HG_V = 64
HG_WIDTH = HG_HEADS * HG_V
HG_KW = HG_HEADS * HG_K
HG_CHUNK = 64
HG_LEVELS = 6

FN_GROUPS = 4
FN_GROUP_DIM = 64
FN_WIDTH = FN_GROUPS * FN_GROUP_DIM

O_CQ = 0
O_CKV = O_CQ + MLA_Q_LORA
O_KR = O_CKV + MLA_KV_LORA
O_HQ = O_KR + MLA_ROPE
IN_WIDTH = O_HQ + 3 * HG_KW + 2 * HG_WIDTH + FN_WIDTH

P_CQ = 0
P_CKV = P_CQ + MLA_Q_LORA
P_KPA = P_CKV + MLA_KV_LORA
P_KPB = P_KPA + 128
P_HG = P_KPB + 128
HG_IN_WIDTH = 3 * HG_KW + 2 * HG_WIDTH
P_FN = P_HG + HG_IN_WIDTH
P_WIDTH = P_FN + FN_WIDTH

ROW_TILE = 256
FFN_TILE = 256
VMEM_LIMIT = 56 * 1024 * 1024

BF16 = jnp.bfloat16
F32 = jnp.float32


def _dot(a, b):
    return jnp.dot(a, b, preferred_element_type=F32)


def _dot_nt(a, b):
    return lax.dot_general(a, b, (((1,), (1,)), ((), ())), preferred_element_type=F32)


def _dot_tn(a, b):
    return lax.dot_general(a, b, (((0,), (0,)), ((), ())), preferred_element_type=F32)


def _silu(x):
    return x * (1.0 / (1.0 + jnp.exp(-x)))


def _sigmoid(x):
    return 1.0 / (1.0 + jnp.exp(-x))


def _split3(x):
    hi = x.astype(BF16)
    r1 = x - hi.astype(F32)
    mid = r1.astype(BF16)
    lo = (r1 - mid.astype(F32)).astype(BF16)
    return hi, mid, lo


def _const_spec(shape):
    nd = len(shape)
    return pl.BlockSpec(shape, lambda *_: (0,) * nd)


def _mod_kernel(c_ref, w_ref, b_ref, o_ref):
    c = c_ref[...]
    act = _silu(c).astype(BF16)
    o_ref[0] = _dot(act, w_ref[0].astype(BF16)) + b_ref[0]


def _modulation(c_all, w_mod, b_mod):
    depth, d, n6 = w_mod.shape
    rows = c_all.shape[0]
    tn = n6 // 6
    return pl.pallas_call(
        _mod_kernel,
        grid=(depth, n6 // tn),
        in_specs=[
            pl.BlockSpec((rows, d), lambda l, j: (0, 0)),
            pl.BlockSpec((1, d, tn), lambda l, j: (l, 0, j)),
            pl.BlockSpec((1, 1, tn), lambda l, j: (l, 0, j)),
        ],
        out_specs=pl.BlockSpec((1, rows, tn), lambda l, j: (l, 0, j)),
        out_shape=jax.ShapeDtypeStruct((depth, rows, n6), F32),
        name="modulation",
    )(c_all, w_mod, b_mod.reshape(depth, 1, n6))


def _lower_bound_kernel(lb_ref, o_ref):
    depth = lb_ref.shape[0]
    lp = [lb_ref[l] for l in range(depth)]
    m = lp[0]
    for l in range(1, depth):
        m = jnp.maximum(m, lp[l])
    e = [jnp.exp(v - m) for v in lp]
    tot = e[0]
    for l in range(1, depth):
        tot = tot + e[l]
    probs = [v / tot for v in e]
    cum = probs[0]
    o_ref[0] = cum - probs[0]
    for l in range(1, depth):
        cum = cum + probs[l]
        o_ref[l] = cum - probs[0]


def _lower_bounds(lb_param):
    return pl.pallas_call(
        _lower_bound_kernel,
        out_shape=jax.ShapeDtypeStruct(lb_param.shape, F32),
        name="hgrn_lower_bounds",
    )(lb_param.astype(F32))


def _inproj_kernel(x_ref, mod_ref, g1_ref, w_in_ref, qg_ref, wq_ref, kvg_ref, wkv_ref, cos_ref, sin_ref,
                   q_ref, k_ref, v_ref, hg_ref, fn_ref, *, q_scale):
    x = x_ref[0]
    shift = mod_ref[0, 0:1, :]
    scale = mod_ref[0, 1:2, :]
    xn = x * lax.rsqrt(jnp.mean(x * x, axis=-1, keepdims=True) + EPS) * g1_ref[...]
    xm = (xn * (1.0 + scale) + shift).astype(BF16)
    p = _dot(xm, w_in_ref[...])

    cos = cos_ref[...]
    sin = sin_ref[...]

    cq = p[:, P_CQ:P_CKV]
    cqn = (cq * lax.rsqrt(jnp.mean(cq * cq, axis=-1, keepdims=True) + EPS) * qg_ref[...]).astype(BF16)
    qq = _dot(cqn, wq_ref[...])
    nw = MLA_HEADS * MLA_NOPE
    for h in range(MLA_HEADS):
        q_ref[0, :, h * HEAD_PAD:h * HEAD_PAD + 128] = (qq[:, h * 128:(h + 1) * 128] * q_scale).astype(BF16)
        pe = qq[:, nw + h * 128:nw + (h + 1) * 128] * cos + qq[:, 2 * nw + h * 128:2 * nw + (h + 1) * 128] * sin
        q_ref[0, :, h * HEAD_PAD + 128:(h + 1) * HEAD_PAD] = (pe * q_scale).astype(BF16)

    ckv = p[:, P_CKV:P_KPA]
    ckvn = (ckv * lax.rsqrt(jnp.mean(ckv * ckv, axis=-1, keepdims=True) + EPS) * kvg_ref[...]).astype(BF16)
    kv = _dot(ckvn, wkv_ref[...])
    kpe = (p[:, P_KPA:P_KPB] * cos + p[:, P_KPB:P_HG] * sin).astype(BF16)
    for h in range(MLA_HEADS):
        k_ref[0, :, h * HEAD_PAD:h * HEAD_PAD + 128] = kv[:, h * 128:(h + 1) * 128].astype(BF16)
        k_ref[0, :, h * HEAD_PAD + 128:(h + 1) * HEAD_PAD] = kpe
    v_ref[0] = kv[:, nw:].astype(BF16)

    hg_ref[0] = p[:, P_HG:P_FN].astype(BF16)
    fn_ref[0] = p[:, P_FN:P_WIDTH].astype(BF16)


def _inproj(xa, mod, g1, w_in_p, qg, wq_all, kvg, wkv_p, cos_t, sin_t, n_ctx_blocks, ctx_row):
    b, s, d = xa.shape
    tm = ROW_TILE
    nblk = s // tm

    def mod_map(i, j):
        return (jnp.where(j < n_ctx_blocks, ctx_row, i), 0, 0)

    row = lambda w: pl.BlockSpec((1, tm, w), lambda i, j: (i, j, 0))
    return pl.pallas_call(
        functools.partial(_inproj_kernel, q_scale=1.0 / math.sqrt(MLA_NOPE + MLA_ROPE)),
        grid=(b, nblk),
        in_specs=[
            row(d),
            pl.BlockSpec((1, 6, d), mod_map),
            _const_spec((1, d)),
            _const_spec(w_in_p.shape),
            _const_spec((1, MLA_Q_LORA)),
            _const_spec(wq_all.shape),
            _const_spec((1, MLA_KV_LORA)),
            _const_spec(wkv_p.shape),
            pl.BlockSpec((tm, 128), lambda i, j: (j, 0)),
            pl.BlockSpec((tm, 128), lambda i, j: (j, 0)),
        ],
        out_specs=[row(MLA_HEADS * HEAD_PAD), row(MLA_HEADS * HEAD_PAD), row(MLA_WIDTH), row(HG_IN_WIDTH),
                   row(FN_WIDTH)],
        out_shape=[
            jax.ShapeDtypeStruct((b, s, MLA_HEADS * HEAD_PAD), BF16),
            jax.ShapeDtypeStruct((b, s, MLA_HEADS * HEAD_PAD), BF16),
            jax.ShapeDtypeStruct((b, s, MLA_WIDTH), BF16),
            jax.ShapeDtypeStruct((b, s, HG_IN_WIDTH), BF16),
            jax.ShapeDtypeStruct((b, s, FN_WIDTH), BF16),
        ],
        compiler_params=pltpu.CompilerParams(dimension_semantics=("arbitrary", "arbitrary"),
                                             vmem_limit_bytes=VMEM_LIMIT),
        name="inproj",
    )(xa, mod, g1.reshape(1, d), w_in_p, qg.reshape(1, -1), wq_all, kvg.reshape(1, -1), wkv_p, cos_t, sin_t)


def _attend(q_ref, k_ref, v_ref, o_ref, n_keys):
    for h in range(MLA_HEADS):
        qh = q_ref[0, :, h * HEAD_PAD:(h + 1) * HEAD_PAD]
        kh = k_ref[0, 0:n_keys, h * HEAD_PAD:(h + 1) * HEAD_PAD]
        vh = v_ref[0, 0:n_keys, h * MLA_V:(h + 1) * MLA_V]
        s = _dot_nt(qh, kh)
        e = jnp.exp(s - jnp.max(s, axis=-1, keepdims=True))
        denom = jnp.sum(e, axis=-1, keepdims=True)
        o = _dot(e.astype(BF16), vh) * (1.0 / denom)
        o_ref[0, :, h * MLA_V:(h + 1) * MLA_V] = o.astype(BF16)


def _attn_kernel(q_ref, k_ref, v_ref, o_ref, *, n_ctx, q_block_offset, n_ctx_blocks):
    n_all = k_ref.shape[1]
    if q_block_offset >= n_ctx_blocks:
        _attend(q_ref, k_ref, v_ref, o_ref, n_all)
    else:
        j = pl.program_id(1) + q_block_offset

        @pl.when(j < n_ctx_blocks)
        def _():
            _attend(q_ref, k_ref, v_ref, o_ref, n_ctx)

        @pl.when(j >= n_ctx_blocks)
        def _():
            _attend(q_ref, k_ref, v_ref, o_ref, n_all)


def _attention(q, k, v, n_ctx, with_ctx_queries):
    b, s, _ = q.shape
    tq = ROW_TILE
    n_ctx_blocks = n_ctx // tq
    off = 0 if with_ctx_queries else n_ctx_blocks
    nq = s // tq - off
    return pl.pallas_call(
        functools.partial(_attn_kernel, n_ctx=n_ctx, q_block_offset=off, n_ctx_blocks=n_ctx_blocks),
        grid=(b, nq),
        in_specs=[
            pl.BlockSpec((1, tq, q.shape[2]), lambda i, j: (i, j + off, 0)),
            pl.BlockSpec((1, s, k.shape[2]), lambda i, j: (i, 0, 0)),
            pl.BlockSpec((1, s, v.shape[2]), lambda i, j: (i, 0, 0)),
        ],
        out_specs=pl.BlockSpec((1, tq, MLA_WIDTH), lambda i, j: (i, j, 0)),
        out_shape=jax.ShapeDtypeStruct((b, nq * tq, MLA_WIDTH), BF16),
        compiler_params=pltpu.CompilerParams(dimension_semantics=("arbitrary", "arbitrary"),
                                             vmem_limit_bytes=VMEM_LIMIT),
        name="attention",
    )(q, k, v)


def _hgrn_constants():
    c = HG_CHUNK
    t = np.arange(c)[:, None]
    u = np.arange(c)[None, :]
    blocks = [(u <= t), (u > t)]
    m = c
    while m >= 2:
        half = m // 2
        mid = (t // m) * m + half
        lower = (t % m) >= half
        blocks.append(np.where(lower, (u >= mid) & (u <= t), (u >= t + 1) & (u <= mid - 1)))
        m //= 2
    mf = np.stack([blk.astype(np.float32) for blk in blocks])
    mb = mf[:, ::-1, ::-1]
    x = t ^ u
    msb = np.floor(np.log2(np.maximum(x, 1))).astype(np.int32)
    lvl = np.where(t > u, HG_LEVELS - 1 - msb, np.where(t == u, HG_LEVELS, -1)).astype(np.int32)
    return (mf.reshape(-1, c), np.ascontiguousarray(mb).reshape(-1, c), lvl, np.ascontiguousarray(lvl.T))


def _hgrn_chunk(hq, z, vv, lb, m_ref, lvl, lower_is_set, state_ref, total_row):
    c = HG_CHUNK
    f = lb + (1.0 - lb) * _sigmoid(z)
    g = jnp.log(f)
    kk = 1.0 - f
    qq = _silu(hq)
    g_hi, g_mid, g_lo = _split3(g)
    m_all = m_ref[...]
    expo = _dot(m_all, g_hi) + _dot(m_all, g_mid) + _dot(m_all, g_lo)
    dec = jnp.exp(expo)
    q_in = (qq * dec[0:c]).astype(BF16)
    k_up = (kk * dec[c:2 * c]).astype(BF16)
    row = lax.broadcasted_iota(jnp.int32, (c, HG_KW), 0)
    xs = []
    for l in range(HG_LEVELS):
        half = c >> (l + 1)
        bit = (row & half) != 0
        lower = bit if lower_is_set else jnp.logical_not(bit)
        xs.append((jnp.where(lower, qq, kk) * dec[(2 + l) * c:(3 + l) * c]).astype(BF16))
    qb = qq.astype(BF16)
    kb = kk.astype(BF16)
    state = state_ref[...]
    state_b = state.astype(BF16)
    outs = []
    new_cols = []
    for h in range(HG_HEADS):
        ks = slice(h * HG_K, (h + 1) * HG_K)
        vs = slice(h * HG_V, (h + 1) * HG_V)
        sc = jnp.where(lvl == HG_LEVELS, _dot_nt(qb[:, ks], kb[:, ks]), 0.0)
        for l in range(HG_LEVELS):
            xl = xs[l][:, ks]
            sc = sc + jnp.where(lvl == l, _dot_nt(xl, xl), 0.0)
        vh = vv[:, vs]
        o = _dot(sc.astype(BF16), vh) + _dot_nt(q_in[:, ks], state_b[:, ks])
        outs.append(o)
        new_cols.append(_dot_tn(vh, k_up[:, ks]))
    total = dec[total_row:total_row + 1, :]
    state_ref[...] = total * state + jnp.concatenate(new_cols, axis=1)
    return jnp.concatenate(outs, axis=1)


def _hgrn_kernel(x_ref, lb_ref, gain_ref, mf_ref, mb_ref, lvlf_ref, lvlb_ref, ones_ref, o_ref,
                 of_ref, ob_ref, sf_ref, sb_ref, *, n_ctx_chunks, out_row_offset):
    c = HG_CHUNK
    s_len = x_ref.shape[1]
    n_chunks = s_len // c
    sf_ref[...] = jnp.zeros_like(sf_ref)
    sb_ref[...] = jnp.zeros_like(sb_ref)
    lvl_f = lvlf_ref[...]
    lvl_b = lvlb_ref[...]
    lb_f = lb_ref[0:1, :]
    lb_b = lb_ref[1:2, :]

    def load(ci):
        r0 = pl.multiple_of(ci * c, c)
        blk = x_ref[0, pl.ds(r0, c), :]
        hq = blk[:, 0:HG_KW].astype(F32)
        zf = blk[:, HG_KW:2 * HG_KW].astype(F32)
        zb = blk[:, 2 * HG_KW:3 * HG_KW].astype(F32)
        vv = blk[:, 3 * HG_KW:3 * HG_KW + HG_WIDTH]
        return r0, hq, zf, zb, vv

    def step(i, carry):
        r0, hq, zf, _, vv = load(i)
        of_ref[pl.ds(r0, c), :] = _hgrn_chunk(hq, zf, vv, lb_f, mf_ref, lvl_f, True, sf_ref, c - 1)
        cb = jnp.where(i < n_ctx_chunks, n_ctx_chunks - 1 - i, n_chunks - 1 - (i - n_ctx_chunks))
        r0, hq, _, zb, vv = load(cb)
        ob_ref[pl.ds(r0, c), :] = _hgrn_chunk(hq, zb, vv, lb_b, mb_ref, lvl_b, False, sb_ref, 0)
        return carry

    lax.fori_loop(0, n_chunks, step, 0)

    rt = 256
    n_out = o_ref.shape[1]
    gain = gain_ref[...]
    ones = ones_ref[...]

    def readout(i, carry):
        r0 = pl.multiple_of(i * rt + out_row_offset, rt)
        o = of_ref[pl.ds(r0, rt), :] + ob_ref[pl.ds(r0, rt), :]
        sq_hi, sq_mid, sq_lo = _split3(o * o)
        msq = _dot(sq_hi, ones) + _dot(sq_mid, ones) + _dot(sq_lo, ones)
        zg = x_ref[0, pl.ds(r0, rt), 3 * HG_KW + HG_WIDTH:HG_IN_WIDTH].astype(F32)
        y = o * lax.rsqrt(msq + EPS) * gain * _silu(zg)
        o_ref[0, pl.ds(pl.multiple_of(i * rt, rt), rt), :] = y.astype(BF16)
        return carry

    lax.fori_loop(0, n_out // rt, readout, 0)


def _hgrn(hg_in, lb, gain, n_ctx, with_ctx_outputs):
    b, s, w = hg_in.shape
    mf, mb, lvl_f, lvl_b = _hgrn_constants()
    ones = np.kron(np.eye(HG_HEADS, dtype=np.float32), np.full((HG_V, HG_V), 1.0 / HG_V, np.float32))
    off = 0 if with_ctx_outputs else n_ctx
    n_out = s - off
    return pl.pallas_call(
        functools.partial(_hgrn_kernel, n_ctx_chunks=n_ctx // HG_CHUNK, out_row_offset=off),
        grid=(b,),
        in_specs=[
            pl.BlockSpec((1, s, w), lambda i: (i, 0, 0)),
            _const_spec((2, HG_KW)),
            _const_spec((1, HG_WIDTH)),
            _const_spec(mf.shape),
            _const_spec(mb.shape),
            _const_spec(lvl_f.shape),
            _const_spec(lvl_b.shape),
            _const_spec(ones.shape),
        ],
        out_specs=pl.BlockSpec((1, n_out, HG_WIDTH), lambda i: (i, 0, 0)),
        out_shape=jax.ShapeDtypeStruct((b, n_out, HG_WIDTH), BF16),
        scratch_shapes=[
            pltpu.VMEM((s, HG_WIDTH), F32),
            pltpu.VMEM((s, HG_WIDTH), F32),
            pltpu.VMEM((HG_V, HG_KW), F32),
            pltpu.VMEM((HG_V, HG_KW), F32),
        ],
        compiler_params=pltpu.CompilerParams(dimension_semantics=("arbitrary",), vmem_limit_bytes=VMEM_LIMIT),
        name="hgrn2",
    )(hg_in, lb, gain.reshape(1, HG_WIDTH), jnp.asarray(mf, BF16), jnp.asarray(mb, BF16),
      jnp.asarray(lvl_f), jnp.asarray(lvl_b), jnp.asarray(ones, BF16))


def _dft_tables(n):
    idx = np.arange(n, dtype=np.int64)
    ang = 2.0 * np.pi * ((idx[:, None] * idx[None, :]) % n).astype(np.float64) / n
    return np.cos(ang).astype(np.float32), np.sin(ang).astype(np.float32)


def _fourier_tile(z, ct, st, cg, sg, w, norm):
    zc = _dot(ct, z).astype(BF16)
    zs = _dot(st, z).astype(BF16)
    mixed = (_dot(zc, cg) - _dot(zs, sg)) * norm
    return _dot(mixed.astype(BF16), w).astype(BF16)


def _fourier_kernel(z_ref, cl_ref, sl_ref, ct_ref, st_ref, cg_ref, sg_ref, w_ref, o_ref, *, n_ctx, block_offset,
                    n_ctx_blocks):
    s_len = z_ref.shape[1]
    n_lat = s_len - n_ctx
    cg = cg_ref[...]
    sg = sg_ref[...]
    w = w_ref[...]

    def latent():
        z = z_ref[0, n_ctx:s_len, :]
        o_ref[0] = _fourier_tile(z, ct_ref[...], st_ref[...], cg, sg, w, 1.0 / math.sqrt(n_lat * FN_GROUP_DIM))

    if block_offset >= n_ctx_blocks:
        latent()
    else:
        j = pl.program_id(1)

        @pl.when(j < n_ctx_blocks)
        def _():
            z = z_ref[0, 0:n_ctx, :]
            o_ref[0] = _fourier_tile(z, cl_ref[...], sl_ref[...], cg, sg, w, 1.0 / math.sqrt(n_ctx * FN_GROUP_DIM))

        pl.when(j >= n_ctx_blocks)(latent)


def _fourier(fn_in, w_fourier, n_ctx, with_ctx_outputs):
    b, s, w = fn_in.shape
    n_lat = s - n_ctx
    tn = ROW_TILE
    n_ctx_blocks = n_ctx // tn
    assert n_ctx_blocks == 1
    off = 0 if with_ctx_outputs else n_ctx_blocks
    nblk = s // tn - off
    cl, sl = _dft_tables(n_ctx)
    ct, st = _dft_tables(n_lat)
    cgrp, sgrp = _dft_tables(FN_GROUP_DIM)
    eye = np.eye(FN_GROUPS, dtype=np.float32)
    cg = np.kron(eye, cgrp)
    sg = np.kron(eye, sgrp)

    def lat_map(i, j):
        return (jnp.maximum(j + off - n_ctx_blocks, 0), 0)

    return pl.pallas_call(
        functools.partial(_fourier_kernel, n_ctx=n_ctx, block_offset=off, n_ctx_blocks=n_ctx_blocks),
        grid=(b, nblk),
        in_specs=[
            pl.BlockSpec((1, s, w), lambda i, j: (i, 0, 0)),
            _const_spec(cl.shape),
            _const_spec(sl.shape),
            pl.BlockSpec((tn, n_lat), lat_map),
            pl.BlockSpec((tn, n_lat), lat_map),
            _const_spec(cg.shape),
            _const_spec(sg.shape),
            _const_spec(w_fourier.shape),
        ],
        out_specs=pl.BlockSpec((1, tn, FN_WIDTH), lambda i, j: (i, j, 0)),
        out_shape=jax.ShapeDtypeStruct((b, nblk * tn, FN_WIDTH), BF16),
        compiler_params=pltpu.CompilerParams(dimension_semantics=("arbitrary", "arbitrary"),
                                             vmem_limit_bytes=VMEM_LIMIT),
        name="fourier",
    )(fn_in, jnp.asarray(cl, BF16), jnp.asarray(sl, BF16), jnp.asarray(ct, BF16), jnp.asarray(st, BF16),
      jnp.asarray(cg, BF16), jnp.asarray(sg, BF16), w_fourier)


def _ffn_kernel(x_ref, attn_ref, hg_ref, fn_ref, mod_ref, g2_ref, wo_ref, wg_ref, wu_ref, wd_ref, gf_ref, o_ref,
                *, final_norm):
    x = x_ref[0]
    gate1 = mod_ref[0, 2:3, :]
    shift2 = mod_ref[0, 3:4, :]
    scale2 = mod_ref[0, 4:5, :]
    gate2 = mod_ref[0, 5:6, :]
    y = (_dot(attn_ref[0], wo_ref[0:MLA_WIDTH, :])
         + _dot(hg_ref[0], wo_ref[MLA_WIDTH:MLA_WIDTH + HG_WIDTH, :])
         + _dot(fn_ref[0], wo_ref[MLA_WIDTH + HG_WIDTH:MLA_WIDTH + HG_WIDTH + FN_WIDTH, :]))
    x1 = x + gate1 * y
    xn = x1 * lax.rsqrt(jnp.mean(x1 * x1, axis=-1, keepdims=True) + EPS) * g2_ref[...]
    hm = (xn * (1.0 + scale2) + shift2).astype(BF16)

    def body(t, acc):
        gt = _dot(hm, wg_ref[t])
        up = _dot(hm, wu_ref[t])
        act = (_silu(gt) * up).astype(BF16)
        return acc + _dot(act, wd_ref[t])

    acc = lax.fori_loop(0, wg_ref.shape[0], body, jnp.zeros(x.shape, F32))
    out = x1 + gate2 * acc
    if final_norm:
        out = out * lax.rsqrt(jnp.mean(out * out, axis=-1, keepdims=True) + EPS) * gf_ref[...]
    o_ref[0] = out


def _outproj_ffn(xa, attn, hg, fn, mod, g2, w_out, w_g, w_u, w_d, g_final, n_ctx_blocks, ctx_row, x_block_offset,
                 final_norm):
    b, s, d = xa.shape
    tm = ROW_TILE
    nblk = s // tm - x_block_offset
    a_off, h_off, f_off = (x_block_offset if a.shape[1] == s else 0 for a in (attn, hg, fn))

    def mod_map(i, j):
        return (jnp.where(j + x_block_offset < n_ctx_blocks, ctx_row, i), 0, 0)

    def row(w, off):
        return pl.BlockSpec((1, tm, w), lambda i, j: (i, j + off, 0))

    return pl.pallas_call(
        functools.partial(_ffn_kernel, final_norm=final_norm),
        grid=(b, nblk),
        in_specs=[
            row(d, x_block_offset),
            row(MLA_WIDTH, a_off),
            row(HG_WIDTH, h_off),
            row(FN_WIDTH, f_off),
            pl.BlockSpec((1, 6, d), mod_map),
            _const_spec((1, d)),
            _const_spec(w_out.shape),
            _const_spec(w_g.shape),
            _const_spec(w_u.shape),
            _const_spec(w_d.shape),
            _const_spec((1, d)),
        ],
        out_specs=row(d, 0),
        out_shape=jax.ShapeDtypeStruct((b, nblk * tm, d), F32),
        compiler_params=pltpu.CompilerParams(dimension_semantics=("arbitrary", "arbitrary"),
                                             vmem_limit_bytes=VMEM_LIMIT),
        name="outproj_ffn",
    )(xa, attn, hg, fn, mod, g2.reshape(1, d), w_out, w_g, w_u, w_d, g_final.reshape(1, d))


def _rotate_half_cols(w):
    r1, r2, c1, c2 = jnp.split(w, 4, axis=-1)
    return jnp.concatenate([-r2, r1, -c2, c1], axis=-1)


def _pack_w_in(w):
    d = w.shape[0]
    zeros = jnp.zeros((d, 128 - MLA_ROPE), w.dtype)
    w_kr = w[:, O_KR:O_HQ]
    return jnp.concatenate([w[:, O_CQ:O_KR], w_kr, zeros, _rotate_half_cols(w_kr), zeros, w[:, O_HQ:]],
                           axis=1).astype(BF16)


def _pack_w_uq(w):
    r = w.shape[0]
    wh = w.reshape(r, MLA_HEADS, MLA_NOPE + MLA_ROPE)
    nope = wh[:, :, :MLA_NOPE].reshape(r, MLA_HEADS * MLA_NOPE)
    pe = wh[:, :, MLA_NOPE:]
    zeros = jnp.zeros((r, MLA_HEADS, 128 - MLA_ROPE), w.dtype)
    pe_a = jnp.concatenate([pe, zeros], axis=-1).reshape(r, MLA_HEADS * 128)
    pe_b = jnp.concatenate([_rotate_half_cols(pe), zeros], axis=-1).reshape(r, MLA_HEADS * 128)
    return jnp.concatenate([nope, pe_a, pe_b], axis=1).astype(BF16)


def _pack_w_ukv(w):
    r = w.shape[0]
    wh = w.reshape(r, MLA_HEADS, MLA_NOPE + MLA_V)
    k_nope = wh[:, :, :MLA_NOPE].reshape(r, MLA_HEADS * MLA_NOPE)
    v = wh[:, :, MLA_NOPE:].reshape(r, MLA_HEADS * MLA_V)
    return jnp.concatenate([k_nope, v], axis=1).astype(BF16)


def _rope_tables(n_ctx, n_lat):
    rows = n_lat // GRID_W
    row_pos = np.repeat(np.arange(rows, dtype=np.float32), GRID_W)
    col_pos = np.tile(np.arange(GRID_W, dtype=np.float32), rows)
    axis_dim = MLA_ROPE // 2
    inv_freq = (ROPE_BASE ** (-np.arange(0, axis_dim, 2, dtype=np.float32) / axis_dim)).astype(np.float32)
    ang_r = row_pos[:, None] * inv_freq
    ang_c = col_pos[:, None] * inv_freq
    ang = np.concatenate([ang_r, ang_r, ang_c, ang_c], axis=-1)
    cos = np.ones((n_ctx + n_lat, 128), np.float32)
    sin = np.zeros((n_ctx + n_lat, 128), np.float32)
    cos[n_ctx:, :MLA_ROPE] = np.cos(ang)
    sin[n_ctx:, :MLA_ROPE] = np.sin(ang)
    return jnp.asarray(cos), jnp.asarray(sin)


def kernel(x, c, ctx, c_ctx, w_mod, b_mod, norm1_g, norm2_g, w_in, q_norm_g, w_uq, kv_norm_g, w_ukv, lb_param,
           hg_norm_g, w_fourier, w_out, w_gate_up, w_down, final_norm_g):
    b, t, d = x.shape
    n_ctx = ctx.shape[1]
    depth = w_mod.shape[0]
    hidden = w_down.shape[1]
    assert n_ctx == ROW_TILE and t % ROW_TILE == 0 and hidden % FFN_TILE == 0 and t % GRID_W == 0
    n_ctx_blocks = n_ctx // ROW_TILE

    mod_rows = -(-(b + 1) // 8) * 8
    c_all = jnp.zeros((mod_rows, d), F32).at[:b].set(c).at[b].set(c_ctx)
    mod_all = _modulation(c_all, w_mod, b_mod).reshape(depth, mod_rows, 6, d)
    lower = _lower_bounds(lb_param)
    cos_t, sin_t = _rope_tables(n_ctx, t)

    xa = jnp.concatenate([ctx, x], axis=1)
    n_ft = hidden // FFN_TILE
    for l in range(depth):
        last = l == depth - 1
        w_gu = w_gate_up[l].astype(BF16)
        w_g = w_gu[:, :hidden].reshape(d, n_ft, FFN_TILE).transpose(1, 0, 2)
        w_u = w_gu[:, hidden:].reshape(d, n_ft, FFN_TILE).transpose(1, 0, 2)
        w_d = w_down[l].astype(BF16).reshape(n_ft, FFN_TILE, d)

        q, k, v, hg_in, fn_in = _inproj(xa, mod_all[l], norm1_g[l], _pack_w_in(w_in[l]), q_norm_g[l],
                                        _pack_w_uq(w_uq[l]), kv_norm_g[l], _pack_w_ukv(w_ukv[l]), cos_t, sin_t,
                                        n_ctx_blocks, b)
        attn = _attention(q, k, v, n_ctx, not last)
        hg = _hgrn(hg_in, lower[l], hg_norm_g[l], n_ctx, not last)
        fn = _fourier(fn_in, w_fourier[l].astype(BF16), n_ctx, not last)
        xa = _outproj_ffn(xa, attn, hg, fn, mod_all[l], norm2_g[l], w_out[l].astype(BF16), w_g, w_u, w_d,
                          final_norm_g, n_ctx_blocks, b, n_ctx_blocks if last else 0, last)
    return xa
```

```python
import functools
import math

import numpy as np
import jax
import jax.numpy as jnp
from jax import lax
from jax.experimental import pallas as pl
from jax.experimental.pallas import tpu as pltpu

EPS = 1e-6
GRID_W = 64
ROPE_BASE = 10000.0

MLA_HEADS = 4
MLA_Q_LORA = 256
MLA_KV_LORA = 128
MLA_NOPE = 128
MLA_ROPE = 64
MLA_V = 128
MLA_WIDTH = MLA_HEADS * MLA_V
HEAD_PAD = 256

HG_HEADS = 4
HG_K = 128
HG_V = 64
HG_WIDTH = HG_HEADS * HG_V
HG_KW = HG_HEADS * HG_K
HG_CHUNK = 64
HG_LEVELS = 6
HG_MATMUL_LEVELS = 3

FN_GROUPS = 4
FN_GROUP_DIM = 64
FN_WIDTH = FN_GROUPS * FN_GROUP_DIM

O_CQ = 0
O_CKV = O_CQ + MLA_Q_LORA
O_KR = O_CKV + MLA_KV_LORA
O_HQ = O_KR + MLA_ROPE

P_CQ = 0
P_CKV = P_CQ + MLA_Q_LORA
P_KPA = P_CKV + MLA_KV_LORA
P_KPB = P_KPA + 128
P_HG = P_KPB + 128
HG_IN_WIDTH = 3 * HG_KW + 2 * HG_WIDTH
P_FN = P_HG + HG_IN_WIDTH
P_WIDTH = P_FN + FN_WIDTH

STREAM_TILE = 768
LATENT_TILE = 512
ATTN_TILE = 256
FFN_TILE = 256
VMEM_LIMIT = 56 * 1024 * 1024

BF16 = jnp.bfloat16
F32 = jnp.float32


def _dot(a, b):
    return jnp.dot(a, b, preferred_element_type=F32)


def _dot_nt(a, b):
    return lax.dot_general(a, b, (((1,), (1,)), ((), ())), preferred_element_type=F32)


def _dot_tn(a, b):
    return lax.dot_general(a, b, (((0,), (0,)), ((), ())), preferred_element_type=F32)


def _silu(x):
    return x * (1.0 / (1.0 + jnp.exp(-x)))


def _sigmoid(x):
    return 1.0 / (1.0 + jnp.exp(-x))


def _split3(x):
    hi = x.astype(BF16)
    r1 = x - hi.astype(F32)
    mid = r1.astype(BF16)
    lo = (r1 - mid.astype(F32)).astype(BF16)
    return hi, mid, lo


def _const_spec(shape):
    nd = len(shape)
    return pl.BlockSpec(shape, lambda *_: (0,) * nd)


_RESIDENT = pl.BlockSpec(memory_space=pltpu.VMEM)


def _params(n_axes):
    return pltpu.CompilerParams(dimension_semantics=("arbitrary",) * n_axes, vmem_limit_bytes=VMEM_LIMIT)


def _mod_kernel(c_ref, w_ref, b_ref, o_ref):
    act = _silu(c_ref[...]).astype(BF16)
    o_ref[0] = _dot(act, w_ref[0].astype(BF16)) + b_ref[0]


def _modulation(c_all, w_mod, b_mod):
    depth, d, n6 = w_mod.shape
    rows = c_all.shape[0]
    tn = n6 // 6
    return pl.pallas_call(
        _mod_kernel,
        grid=(depth, n6 // tn),
        in_specs=[
            pl.BlockSpec((rows, d), lambda l, j: (0, 0)),
            pl.BlockSpec((1, d, tn), lambda l, j: (l, 0, j)),
            pl.BlockSpec((1, 1, tn), lambda l, j: (l, 0, j)),
        ],
        out_specs=pl.BlockSpec((1, rows, tn), lambda l, j: (l, 0, j)),
        out_shape=jax.ShapeDtypeStruct((depth, rows, n6), F32),
        name="modulation",
    )(c_all, w_mod, b_mod.reshape(depth, 1, n6))


def _lower_bound_kernel(lb_ref, o_ref):
    depth = lb_ref.shape[0]
    lp = [lb_ref[l] for l in range(depth)]
    m = lp[0]
    for l in range(1, depth):
        m = jnp.maximum(m, lp[l])
    e = [jnp.exp(v - m) for v in lp]
    tot = e[0]
    for l in range(1, depth):
        tot = tot + e[l]
    probs = [v / tot for v in e]
    cum = probs[0]
    o_ref[0] = cum - probs[0]
    for l in range(1, depth):
        cum = cum + probs[l]
        o_ref[l] = cum - probs[0]


def _lower_bounds(lb_param):
    return pl.pallas_call(
        _lower_bound_kernel,
        out_shape=jax.ShapeDtypeStruct(lb_param.shape, F32),
        name="hgrn_lower_bounds",
    )(lb_param.astype(F32))


def _row_select(tile_rows, n_lat, ctx_vec, lat_vec):
    row = pl.program_id(1) * tile_rows + lax.broadcasted_iota(jnp.int32, (tile_rows, 1), 0)
    return jnp.where(row >= n_lat, ctx_vec, lat_vec)


def _inproj_kernel(x_ref, modb_ref, modc_ref, g1_ref, w_in_ref, qg_ref, wq_ref, kvg_ref, wkv_ref, cos_ref, sin_ref,
                   q_ref, k_ref, v_ref, hg_ref, fn_ref, *, q_scale, n_lat):
    x = x_ref[0]
    tm = x.shape[0]
    shift = _row_select(tm, n_lat, modc_ref[0, 0:1, :], modb_ref[0, 0:1, :])
    scale = _row_select(tm, n_lat, modc_ref[0, 1:2, :], modb_ref[0, 1:2, :])
    xn = x * lax.rsqrt(jnp.mean(x * x, axis=-1, keepdims=True) + EPS) * g1_ref[...]
    xm = (xn * (1.0 + scale) + shift).astype(BF16)

    hg_ref[0] = _dot(xm, w_in_ref[:, P_HG:P_FN]).astype(BF16)
    fn_ref[0] = _dot(xm, w_in_ref[:, P_FN:P_WIDTH]).astype(BF16)
    p = _dot(xm, w_in_ref[:, 0:P_HG])

    cos = cos_ref[...]
    sin = sin_ref[...]

    cq = p[:, P_CQ:P_CKV]
    cqn = (cq * lax.rsqrt(jnp.mean(cq * cq, axis=-1, keepdims=True) + EPS) * qg_ref[...]).astype(BF16)
    qq = _dot(cqn, wq_ref[...])
    nw = MLA_HEADS * MLA_NOPE
    for h in range(MLA_HEADS):
        q_ref[0, :, h * HEAD_PAD:h * HEAD_PAD + 128] = (qq[:, h * 128:(h + 1) * 128] * q_scale).astype(BF16)
        pe = qq[:, nw + h * 128:nw + (h + 1) * 128] * cos + qq[:, 2 * nw + h * 128:2 * nw + (h + 1) * 128] * sin
        q_ref[0, :, h * HEAD_PAD + 128:(h + 1) * HEAD_PAD] = (pe * q_scale).astype(BF16)

    ckv = p[:, P_CKV:P_KPA]
    ckvn = (ckv * lax.rsqrt(jnp.mean(ckv * ckv, axis=-1, keepdims=True) + EPS) * kvg_ref[...]).astype(BF16)
    kv = _dot(ckvn, wkv_ref[...])
    kpe = (p[:, P_KPA:P_KPB] * cos + p[:, P_KPB:P_HG] * sin).astype(BF16)
    for h in range(MLA_HEADS):
        k_ref[0, :, h * HEAD_PAD:h * HEAD_PAD + 128] = kv[:, h * 128:(h + 1) * 128].astype(BF16)
        k_ref[0, :, h * HEAD_PAD + 128:(h + 1) * HEAD_PAD] = kpe
    v_ref[0] = kv[:, nw:].astype(BF16)


def _inproj(xa, mod, g1, w_in_p, qg, wq_all, kvg, wkv_p, cos_t, sin_t, n_lat, ctx_row):
    b, s, d = xa.shape
    tm = STREAM_TILE
    row = lambda w: pl.BlockSpec((1, tm, w), lambda i, j: (i, j, 0))
    return pl.pallas_call(
        functools.partial(_inproj_kernel, q_scale=1.0 / math.sqrt(MLA_NOPE + MLA_ROPE), n_lat=n_lat),
        grid=(b, s // tm),
        in_specs=[
            row(d),
            pl.BlockSpec((1, 6, d), lambda i, j: (i, 0, 0)),
            pl.BlockSpec((1, 6, d), lambda i, j: (ctx_row, 0, 0)),
            _const_spec((1, d)),
            _RESIDENT,
            _const_spec((1, MLA_Q_LORA)),
            _RESIDENT,
            _const_spec((1, MLA_KV_LORA)),
            _RESIDENT,
            pl.BlockSpec((tm, 128), lambda i, j: (j, 0)),
            pl.BlockSpec((tm, 128), lambda i, j: (j, 0)),
        ],
        out_specs=[row(MLA_HEADS * HEAD_PAD), row(MLA_HEADS * HEAD_PAD), row(MLA_WIDTH), row(HG_IN_WIDTH),
                   row(FN_WIDTH)],
        out_shape=[
            jax.ShapeDtypeStruct((b, s, MLA_HEADS * HEAD_PAD), BF16),
            jax.ShapeDtypeStruct((b, s, MLA_HEADS * HEAD_PAD), BF16),
            jax.ShapeDtypeStruct((b, s, MLA_WIDTH), BF16),
            jax.ShapeDtypeStruct((b, s, HG_IN_WIDTH), BF16),
            jax.ShapeDtypeStruct((b, s, FN_WIDTH), BF16),
        ],
        compiler_params=_params(2),
        name="inproj",
    )(xa, mod, mod, g1.reshape(1, d), w_in_p, qg.reshape(1, -1), wq_all, kvg.reshape(1, -1), wkv_p, cos_t, sin_t)


def _attend(q_ref, k_ref, v_ref, o_ref, key_lo, key_hi):
    for h in range(MLA_HEADS):
        qh = q_ref[0, :, h * HEAD_PAD:(h + 1) * HEAD_PAD]
        kh = k_ref[0, key_lo:key_hi, h * HEAD_PAD:(h + 1) * HEAD_PAD]
        vh = v_ref[0, key_lo:key_hi, h * MLA_V:(h + 1) * MLA_V]
        s = _dot_nt(qh, kh)
        e = jnp.exp(s - jnp.max(s, axis=-1, keepdims=True))
        denom = jnp.sum(e, axis=-1, keepdims=True)
        o = _dot(e.astype(BF16), vh) * (1.0 / denom)
        o_ref[0, :, h * MLA_V:(h + 1) * MLA_V] = o.astype(BF16)


def _attn_kernel(q_ref, k_ref, v_ref, o_ref, *, n_lat, with_ctx_queries):
    s_len = k_ref.shape[1]
    if not with_ctx_queries:
        _attend(q_ref, k_ref, v_ref, o_ref, 0, s_len)
    else:
        n_lat_blocks = n_lat // q_ref.shape[1]
        j = pl.program_id(1)

        @pl.when(j < n_lat_blocks)
        def _():
            _attend(q_ref, k_ref, v_ref, o_ref, 0, s_len)

        @pl.when(j >= n_lat_blocks)
        def _():
            _attend(q_ref, k_ref, v_ref, o_ref, n_lat, s_len)


def _attention(q, k, v, n_lat, with_ctx_queries):
    b, s, _ = q.shape
    tq = ATTN_TILE
    n_rows = s if with_ctx_queries else n_lat
    return pl.pallas_call(
        functools.partial(_attn_kernel, n_lat=n_lat, with_ctx_queries=with_ctx_queries),
        grid=(b, n_rows // tq),
        in_specs=[
            pl.BlockSpec((1, tq, q.shape[2]), lambda i, j: (i, j, 0)),
            pl.BlockSpec((1, s, k.shape[2]), lambda i, j: (i, 0, 0)),
            pl.BlockSpec((1, s, v.shape[2]), lambda i, j: (i, 0, 0)),
        ],
        out_specs=pl.BlockSpec((1, tq, MLA_WIDTH), lambda i, j: (i, j, 0)),
        out_shape=jax.ShapeDtypeStruct((b, n_rows, MLA_WIDTH), BF16),
        compiler_params=_params(2),
        name="attention",
    )(q, k, v)


def _hgrn_constants():
    c = HG_CHUNK
    t = np.arange(c)[:, None]
    u = np.arange(c)[None, :]
    blocks = [(u <= t)]
    for m in (8, 4, 2):
        half = m // 2
        mid = (t // m) * m + half
        lower = (t % m) >= half
        blocks.append(np.where(lower, (u >= mid) & (u <= t), (u >= t + 1) & (u <= mid - 1)))
    mf = np.stack([blk.astype(np.float32) for blk in blocks])
    mb = np.ascontiguousarray(mf[:, ::-1, ::-1])
    msb = np.floor(np.log2(np.maximum(t ^ u, 1))).astype(np.int32)
    lvl = np.where(t > u, HG_LEVELS - 1 - msb, np.where(t == u, HG_LEVELS, -1)).astype(np.int32)
    lvl_f = np.tile(lvl, (1, HG_HEADS))
    lvl_b = np.tile(np.ascontiguousarray(lvl.T), (1, HG_HEADS))
    return mf.reshape(-1, c), mb.reshape(-1, c), lvl_f, lvl_b


def _block_diag_rows(x, head_masks):
    return jnp.concatenate([x * mk for mk in head_masks], axis=0)


def _hgrn_chunk(hq, z, vv, lb, m_ref, lvl, k_masks, v_masks, state_ref, fwd):
    c = HG_CHUNK
    f = lb + (1.0 - lb) * _sigmoid(z)
    g = jnp.log(f)
    kk = 1.0 - f
    qq = _silu(hq)
    g_hi, g_mid, g_lo = _split3(g)
    m_all = m_ref[...]
    eb = _dot(m_all, g_hi) + _dot(m_all, g_mid) + _dot(m_all, g_lo)
    bc = eb[0:c]
    total = bc[c - 1:c] if fwd else bc[0:1]
    row = lax.broadcasted_iota(jnp.int32, (c, HG_KW), 0)
    xs = []
    for l in range(HG_LEVELS):
        m = c >> l
        half = m // 2
        bit = (row & half) != 0
        is_query = bit if fwd else jnp.logical_not(bit)
        if l < HG_LEVELS - HG_MATMUL_LEVELS:
            refs = [bc[b0 + half - 1:b0 + half] if fwd else bc[b0 + half:b0 + half + 1] for b0 in range(0, c, m)]
            ref = jnp.concatenate([jnp.broadcast_to(r, (m, HG_KW)) for r in refs], axis=0)
            diff = bc - ref
            expo = jnp.where(is_query, diff, -diff)
        else:
            idx = 1 + l - (HG_LEVELS - HG_MATMUL_LEVELS)
            expo = eb[idx * c:(idx + 1) * c]
        xs.append((jnp.where(is_query, qq, kk) * jnp.exp(expo)).astype(BF16))
    q_in = (qq * jnp.exp(bc)).astype(BF16)
    k_up = (kk * jnp.exp(total - bc)).astype(BF16)
    sc = jnp.where(lvl == HG_LEVELS, _dot_nt(qq.astype(BF16), _block_diag_rows(kk.astype(BF16), k_masks)), 0.0)
    for l in range(HG_LEVELS):
        sc = sc + jnp.where(lvl == l, _dot_nt(xs[l], _block_diag_rows(xs[l], k_masks)), 0.0)
    state = state_ref[...]
    o = _dot(sc.astype(BF16), _block_diag_rows(vv, v_masks)) + _dot_nt(q_in, state.astype(BF16))
    chunk_decay = jnp.exp(total)
    for h in range(HG_HEADS):
        ks = slice(h * HG_K, (h + 1) * HG_K)
        vs = slice(h * HG_V, (h + 1) * HG_V)
        state_ref[vs, ks] = chunk_decay[:, ks] * state[vs, ks] + _dot_tn(vv[:, vs], k_up[:, ks])
    return o


def _hgrn_kernel(x_ref, lb_ref, gain_ref, mf_ref, mb_ref, lvlf_ref, lvlb_ref, ones_ref, o_ref,
                 of_ref, ob_ref, sf_ref, sb_ref, *, n_lat_chunks):
    c = HG_CHUNK
    s_len = x_ref.shape[1]
    n_chunks = s_len // c
    n_ctx_chunks = n_chunks - n_lat_chunks
    sf_ref[...] = jnp.zeros_like(sf_ref)
    sb_ref[...] = jnp.zeros_like(sb_ref)
    lvl_f = lvlf_ref[...]
    lvl_b = lvlb_ref[...]
    lb_f = lb_ref[0:1, :]
    lb_b = lb_ref[1:2, :]
    k_lane = lax.broadcasted_iota(jnp.int32, (c, HG_KW), 1)
    v_lane = lax.broadcasted_iota(jnp.int32, (c, HG_WIDTH), 1)
    k_masks = [((k_lane >= h * HG_K) & (k_lane < (h + 1) * HG_K)).astype(F32).astype(BF16) for h in range(HG_HEADS)]
    v_masks = [((v_lane >= h * HG_V) & (v_lane < (h + 1) * HG_V)).astype(F32).astype(BF16) for h in range(HG_HEADS)]

    def load(ci):
        r0 = pl.multiple_of(ci * c, c)
        hq = x_ref[0, pl.ds(r0, c), 0:HG_KW].astype(F32)
        vv = x_ref[0, pl.ds(r0, c), 3 * HG_KW:3 * HG_KW + HG_WIDTH]
        return r0, hq, vv

    def step(i, carry):
        cf = jnp.where(i < n_ctx_chunks, n_lat_chunks + i, i - n_ctx_chunks)
        r0, hq, vv = load(cf)
        zf = x_ref[0, pl.ds(r0, c), HG_KW:2 * HG_KW].astype(F32)
        of_ref[pl.ds(r0, c), :] = _hgrn_chunk(hq, zf, vv, lb_f, mf_ref, lvl_f, k_masks, v_masks, sf_ref, True)
        cb = jnp.where(i < n_ctx_chunks, n_chunks - 1 - i, n_lat_chunks - 1 - (i - n_ctx_chunks))
        r0, hq, vv = load(cb)
        zb = x_ref[0, pl.ds(r0, c), 2 * HG_KW:3 * HG_KW].astype(F32)
        ob_ref[pl.ds(r0, c), :] = _hgrn_chunk(hq, zb, vv, lb_b, mb_ref, lvl_b, k_masks, v_masks, sb_ref, False)
        return carry

    lax.fori_loop(0, n_chunks, step, 0, unroll=2)

    rt = 256
    gain = gain_ref[...]
    ones = ones_ref[...]

    def readout(i, carry):
        r0 = pl.multiple_of(i * rt, rt)
        o = of_ref[pl.ds(r0, rt), :] + ob_ref[pl.ds(r0, rt), :]
        sq_hi, sq_mid, sq_lo = _split3(o * o)
        msq = _dot(sq_hi, ones) + _dot(sq_mid, ones) + _dot(sq_lo, ones)
        zg = x_ref[0, pl.ds(r0, rt), 3 * HG_KW + HG_WIDTH:HG_IN_WIDTH].astype(F32)
        y = o * lax.rsqrt(msq + EPS) * gain * _silu(zg)
        o_ref[0, pl.ds(r0, rt), :] = y.astype(BF16)
        return carry

    lax.fori_loop(0, o_ref.shape[1] // rt, readout, 0)


def _hgrn(hg_in, lb, gain, n_lat, with_ctx_outputs):
    b, s, w = hg_in.shape
    mf, mb, lvl_f, lvl_b = _hgrn_constants()
    ones = np.kron(np.eye(HG_HEADS, dtype=np.float32), np.full((HG_V, HG_V), 1.0 / HG_V, np.float32))
    n_out = s if with_ctx_outputs else n_lat
    return pl.pallas_call(
        functools.partial(_hgrn_kernel, n_lat_chunks=n_lat // HG_CHUNK),
        grid=(b,),
        in_specs=[
            pl.BlockSpec((1, s, w), lambda i: (i, 0, 0)),
            _const_spec((2, HG_KW)),
            _const_spec((1, HG_WIDTH)),
            _const_spec(mf.shape),
            _const_spec(mb.shape),
            _const_spec(lvl_f.shape),
            _const_spec(lvl_b.shape),
            _const_spec(ones.shape),
        ],
        out_specs=pl.BlockSpec((1, n_out, HG_WIDTH), lambda i: (i, 0, 0)),
        out_shape=jax.ShapeDtypeStruct((b, n_out, HG_WIDTH), BF16),
        scratch_shapes=[
            pltpu.VMEM((s, HG_WIDTH), F32),
            pltpu.VMEM((s, HG_WIDTH), F32),
            pltpu.VMEM((HG_WIDTH, HG_KW), F32),
            pltpu.VMEM((HG_WIDTH, HG_KW), F32),
        ],
        compiler_params=_params(1),
        name="hgrn2",
    )(hg_in, lb, gain.reshape(1, HG_WIDTH), jnp.asarray(mf, BF16), jnp.asarray(mb, BF16),
      jnp.asarray(lvl_f), jnp.asarray(lvl_b), jnp.asarray(ones, BF16))


def _dft_tables(n):
    idx = np.arange(n, dtype=np.int64)
    ang = 2.0 * np.pi * ((idx[:, None] * idx[None, :]) % n).astype(np.float64) / n
    return np.cos(ang).astype(np.float32), np.sin(ang).astype(np.float32)


def _fourier_tile(z, ct, st, cg, sg, w, norm):
    zc = _dot(ct, z).astype(BF16)
    zs = _dot(st, z).astype(BF16)
    mixed = (_dot(zc, cg) - _dot(zs, sg)) * norm
    return _dot(mixed.astype(BF16), w).astype(BF16)


def _fourier_kernel(z_ref, cl_ref, sl_ref, ct_ref, st_ref, cg_ref, sg_ref, w_ref, o_ref, *, n_lat, with_ctx_outputs):
    s_len = z_ref.shape[1]
    n_ctx = s_len - n_lat
    cg = cg_ref[...]
    sg = sg_ref[...]
    w = w_ref[...]

    def latent():
        o_ref[0] = _fourier_tile(z_ref[0, 0:n_lat, :], ct_ref[...], st_ref[...], cg, sg, w,
                                 1.0 / math.sqrt(n_lat * FN_GROUP_DIM))

    def context():
        o_ref[0] = _fourier_tile(z_ref[0, n_lat:s_len, :], cl_ref[...], sl_ref[...], cg, sg, w,
                                 1.0 / math.sqrt(n_ctx * FN_GROUP_DIM))

    if not with_ctx_outputs:
        latent()
    else:
        j = pl.program_id(1)
        n_lat_blocks = n_lat // o_ref.shape[1]
        pl.when(j < n_lat_blocks)(latent)
        pl.when(j >= n_lat_blocks)(context)


def _fourier(fn_in, w_fourier, n_lat, with_ctx_outputs):
    b, s, w = fn_in.shape
    n_ctx = s - n_lat
    tn = ATTN_TILE
    assert n_ctx == tn
    n_lat_blocks = n_lat // tn
    n_rows = s if with_ctx_outputs else n_lat
    cl, sl = _dft_tables(n_ctx)
    ct, st = _dft_tables(n_lat)
    cgrp, sgrp = _dft_tables(FN_GROUP_DIM)
    eye = np.eye(FN_GROUPS, dtype=np.float32)
    cg = np.kron(eye, cgrp)
    sg = np.kron(eye, sgrp)

    def lat_map(i, j):
        return (jnp.minimum(j, n_lat_blocks - 1), 0)

    return pl.pallas_call(
        functools.partial(_fourier_kernel, n_lat=n_lat, with_ctx_outputs=with_ctx_outputs),
        grid=(b, n_rows // tn),
        in_specs=[
            pl.BlockSpec((1, s, w), lambda i, j: (i, 0, 0)),
            _const_spec(cl.shape),
            _const_spec(sl.shape),
            pl.BlockSpec((tn, n_lat), lat_map),
            pl.BlockSpec((tn, n_lat), lat_map),
            _const_spec(cg.shape),
            _const_spec(sg.shape),
            _const_spec(w_fourier.shape),
        ],
        out_specs=pl.BlockSpec((1, tn, FN_WIDTH), lambda i, j: (i, j, 0)),
        out_shape=jax.ShapeDtypeStruct((b, n_rows, FN_WIDTH), BF16),
        compiler_params=_params(2),
        name="fourier",
    )(fn_in, jnp.asarray(cl, BF16), jnp.asarray(sl, BF16), jnp.asarray(ct, BF16), jnp.asarray(st, BF16),
      jnp.asarray(cg, BF16), jnp.asarray(sg, BF16), w_fourier)


def _ffn_kernel(x_ref, attn_ref, hg_ref, fn_ref, modb_ref, modc_ref, g2_ref, wo_ref, wgu_ref, wd_ref, gf_ref, o_ref,
                act_ref, *, final_norm, n_lat, select_ctx):
    x = x_ref[0]
    tm = x.shape[0]

    def mod_vec(r):
        if select_ctx:
            return _row_select(tm, n_lat, modc_ref[0, r:r + 1, :], modb_ref[0, r:r + 1, :])
        return modb_ref[0, r:r + 1, :]

    y = (_dot(attn_ref[0], wo_ref[0:MLA_WIDTH, :])
         + _dot(hg_ref[0], wo_ref[MLA_WIDTH:MLA_WIDTH + HG_WIDTH, :])
         + _dot(fn_ref[0], wo_ref[MLA_WIDTH + HG_WIDTH:MLA_WIDTH + HG_WIDTH + FN_WIDTH, :]))
    x1 = x + mod_vec(2) * y
    o_ref[0] = x1
    xn = x1 * lax.rsqrt(jnp.mean(x1 * x1, axis=-1, keepdims=True) + EPS) * g2_ref[...]
    hm = (xn * (1.0 + mod_vec(4)) + mod_vec(3)).astype(BF16)

    ft = wgu_ref.shape[2] // 2
    for t in range(wgu_ref.shape[0]):
        gu = _dot(hm, wgu_ref[t])
        act_ref[:, t * ft:(t + 1) * ft] = (_silu(gu[:, 0:ft]) * gu[:, ft:2 * ft]).astype(BF16)
    out = o_ref[0] + mod_vec(5) * _dot(act_ref[...], wd_ref[...])
    if final_norm:
        out = out * lax.rsqrt(jnp.mean(out * out, axis=-1, keepdims=True) + EPS) * gf_ref[...]
    o_ref[0] = out


def _outproj_ffn(xa, attn, hg, fn, mod, g2, w_out, w_gu, w_d, g_final, n_lat, ctx_row, with_ctx_rows, final_norm):
    b, s, d = xa.shape
    tm = STREAM_TILE if with_ctx_rows else LATENT_TILE
    n_rows = s if with_ctx_rows else n_lat
    hidden = w_d.shape[0]
    row = lambda w: pl.BlockSpec((1, tm, w), lambda i, j: (i, j, 0))
    return pl.pallas_call(
        functools.partial(_ffn_kernel, final_norm=final_norm, n_lat=n_lat, select_ctx=with_ctx_rows),
        grid=(b, n_rows // tm),
        in_specs=[
            row(d),
            row(MLA_WIDTH),
            row(HG_WIDTH),
            row(FN_WIDTH),
            pl.BlockSpec((1, 6, d), lambda i, j: (i, 0, 0)),
            pl.BlockSpec((1, 6, d), lambda i, j: (ctx_row, 0, 0)),
            _const_spec((1, d)),
            _RESIDENT,
            _RESIDENT,
            _RESIDENT,
            _const_spec((1, d)),
        ],
        out_specs=row(d),
        out_shape=jax.ShapeDtypeStruct((b, n_rows, d), F32),
        scratch_shapes=[pltpu.VMEM((tm, hidden), BF16)],
        compiler_params=_params(2),
        name="outproj_ffn",
    )(xa, attn, hg, fn, mod, mod, g2.reshape(1, d), w_out, w_gu, w_d, g_final.reshape(1, d))


def _rotate_half_cols(w):
    r1, r2, c1, c2 = jnp.split(w, 4, axis=-1)
    return jnp.concatenate([-r2, r1, -c2, c1], axis=-1)


def _pack_w_in(w):
    d = w.shape[0]
    zeros = jnp.zeros((d, 128 - MLA_ROPE), w.dtype)
    w_kr = w[:, O_KR:O_HQ]
    return jnp.concatenate([w[:, O_CQ:O_KR], w_kr, zeros, _rotate_half_cols(w_kr), zeros, w[:, O_HQ:]],
                           axis=1).astype(BF16)


def _pack_w_uq(w):
    r = w.shape[0]
    wh = w.reshape(r, MLA_HEADS, MLA_NOPE + MLA_ROPE)
    nope = wh[:, :, :MLA_NOPE].reshape(r, MLA_HEADS * MLA_NOPE)
    pe = wh[:, :, MLA_NOPE:]
    zeros = jnp.zeros((r, MLA_HEADS, 128 - MLA_ROPE), w.dtype)
    pe_a = jnp.concatenate([pe, zeros], axis=-1).reshape(r, MLA_HEADS * 128)
    pe_b = jnp.concatenate([_rotate_half_cols(pe), zeros], axis=-1).reshape(r, MLA_HEADS * 128)
    return jnp.concatenate([nope, pe_a, pe_b], axis=1).astype(BF16)


def _pack_w_ukv(w):
    r = w.shape[0]
    wh = w.reshape(r, MLA_HEADS, MLA_NOPE + MLA_V)
    k_nope = wh[:, :, :MLA_NOPE].reshape(r, MLA_HEADS * MLA_NOPE)
    v = wh[:, :, MLA_NOPE:].reshape(r, MLA_HEADS * MLA_V)
    return jnp.concatenate([k_nope, v], axis=1).astype(BF16)


def _pack_w_gate_up(w, hidden):
    d = w.shape[0]
    n_ft = hidden // FFN_TILE
    gate = w[:, :hidden].reshape(d, n_ft, FFN_TILE)
    up = w[:, hidden:].reshape(d, n_ft, FFN_TILE)
    return jnp.concatenate([gate, up], axis=2).transpose(1, 0, 2).astype(BF16)


def _rope_tables(n_lat, n_ctx):
    rows = n_lat // GRID_W
    row_pos = np.repeat(np.arange(rows, dtype=np.float32), GRID_W)
    col_pos = np.tile(np.arange(GRID_W, dtype=np.float32), rows)
    axis_dim = MLA_ROPE // 2
    inv_freq = (ROPE_BASE ** (-np.arange(0, axis_dim, 2, dtype=np.float32) / axis_dim)).astype(np.float32)
    ang_r = row_pos[:, None] * inv_freq
    ang_c = col_pos[:, None] * inv_freq
    ang = np.concatenate([ang_r, ang_r, ang_c, ang_c], axis=-1)
    cos = np.ones((n_lat + n_ctx, 128), np.float32)
    sin = np.zeros((n_lat + n_ctx, 128), np.float32)
    cos[:n_lat, :MLA_ROPE] = np.cos(ang)
    sin[:n_lat, :MLA_ROPE] = np.sin(ang)
    return jnp.asarray(cos), jnp.asarray(sin)


def kernel(x, c, ctx, c_ctx, w_mod, b_mod, norm1_g, norm2_g, w_in, q_norm_g, w_uq, kv_norm_g, w_ukv, lb_param,
           hg_norm_g, w_fourier, w_out, w_gate_up, w_down, final_norm_g):
    b, t, d = x.shape
    n_ctx = ctx.shape[1]
    s = t + n_ctx
    depth = w_mod.shape[0]
    hidden = w_down.shape[1]
    assert s % STREAM_TILE == 0 and t % LATENT_TILE == 0 and t % ATTN_TILE == 0 and n_ctx == ATTN_TILE
    assert hidden % FFN_TILE == 0 and t % GRID_W == 0

    mod_rows = -(-(b + 1) // 8) * 8
    c_all = jnp.zeros((mod_rows, d), F32).at[:b].set(c).at[b].set(c_ctx)
    mod_all = _modulation(c_all, w_mod, b_mod).reshape(depth, mod_rows, 6, d)
    lower = _lower_bounds(lb_param)
    cos_t, sin_t = _rope_tables(t, n_ctx)

    xa = jnp.concatenate([x, ctx], axis=1)
    for l in range(depth):
        last = l == depth - 1
        q, k, v, hg_in, fn_in = _inproj(xa, mod_all[l], norm1_g[l], _pack_w_in(w_in[l]), q_norm_g[l],
                                        _pack_w_uq(w_uq[l]), kv_norm_g[l], _pack_w_ukv(w_ukv[l]), cos_t, sin_t, t, b)
        attn = _attention(q, k, v, t, not last)
        hg = _hgrn(hg_in, lower[l], hg_norm_g[l], t, not last)
        fn = _fourier(fn_in, w_fourier[l].astype(BF16), t, not last)
        xa = _outproj_ffn(xa, attn, hg, fn, mod_all[l], norm2_g[l], w_out[l].astype(BF16),
                          _pack_w_gate_up(w_gate_up[l], hidden), w_down[l].astype(BF16), final_norm_g, t, b,
                          not last, last)
    return xa
```

```python
import functools
import math

import numpy as np
import jax
import jax.numpy as jnp
from jax import lax
from jax.experimental import pallas as pl
from jax.experimental.pallas import tpu as pltpu

EPS = 1e-6
GRID_W = 64
ROPE_BASE = 10000.0

MLA_HEADS = 4
MLA_Q_LORA = 256
MLA_KV_LORA = 128
MLA_NOPE = 128
MLA_ROPE = 64
MLA_V = 128
MLA_WIDTH = MLA_HEADS * MLA_V
HEAD_PAD = 256

HG_HEADS = 4
HG_K = 128
HG_V = 64
HG_WIDTH = HG_HEADS * HG_V
HG_KW = HG_HEADS * HG_K
HG_CHUNK = 64
HG_LEVELS = 6
HG_MATMUL_LEVELS = 3

FN_GROUPS = 4
FN_GROUP_DIM = 64
FN_WIDTH = FN_GROUPS * FN_GROUP_DIM

O_CQ = 0
O_CKV = O_CQ + MLA_Q_LORA
O_KR = O_CKV + MLA_KV_LORA
O_HQ = O_KR + MLA_ROPE

P_CQ = 0
P_CKV = P_CQ + MLA_Q_LORA
P_KPA = P_CKV + MLA_KV_LORA
P_KPB = P_KPA + 128
P_HG = P_KPB + 128
HG_IN_WIDTH = 3 * HG_KW + 2 * HG_WIDTH
P_FN = P_HG + HG_IN_WIDTH
P_WIDTH = P_FN + FN_WIDTH

STREAM_TILE = 768
LATENT_TILE = 512
ATTN_TILE = 512
FOURIER_TILE = 256
FFN_TILE = 256
VMEM_LIMIT = 56 * 1024 * 1024

BF16 = jnp.bfloat16
F32 = jnp.float32


def _dot(a, b):
    return jnp.dot(a, b, preferred_element_type=F32)


def _dot_nt(a, b):
    return lax.dot_general(a, b, (((1,), (1,)), ((), ())), preferred_element_type=F32)


def _dot_tn(a, b):
    return lax.dot_general(a, b, (((0,), (0,)), ((), ())), preferred_element_type=F32)


def _silu(x):
    return x * (1.0 / (1.0 + jnp.exp(-x)))


def _sigmoid(x):
    return 1.0 / (1.0 + jnp.exp(-x))


def _split3(x):
    hi = x.astype(BF16)
    r1 = x - hi.astype(F32)
    mid = r1.astype(BF16)
    lo = (r1 - mid.astype(F32)).astype(BF16)
    return hi, mid, lo


def _const_spec(shape):
    nd = len(shape)
    return pl.BlockSpec(shape, lambda *_: (0,) * nd)


_RESIDENT = pl.BlockSpec(memory_space=pltpu.VMEM)


def _params(n_axes):
    return pltpu.CompilerParams(dimension_semantics=("arbitrary",) * n_axes, vmem_limit_bytes=VMEM_LIMIT)


def _mod_kernel(c_ref, w_ref, b_ref, o_ref):
    act = _silu(c_ref[...]).astype(BF16)
    o_ref[0] = _dot(act, w_ref[0].astype(BF16)) + b_ref[0]


def _modulation(c_all, w_mod, b_mod):
    depth, d, n6 = w_mod.shape
    rows = c_all.shape[0]
    tn = n6 // 6
    return pl.pallas_call(
        _mod_kernel,
        grid=(depth, n6 // tn),
        in_specs=[
            pl.BlockSpec((rows, d), lambda l, j: (0, 0)),
            pl.BlockSpec((1, d, tn), lambda l, j: (l, 0, j)),
            pl.BlockSpec((1, 1, tn), lambda l, j: (l, 0, j)),
        ],
        out_specs=pl.BlockSpec((1, rows, tn), lambda l, j: (l, 0, j)),
        out_shape=jax.ShapeDtypeStruct((depth, rows, n6), F32),
        name="modulation",
    )(c_all, w_mod, b_mod.reshape(depth, 1, n6))


def _lower_bound_kernel(lb_ref, o_ref):
    depth = lb_ref.shape[0]
    lp = [lb_ref[l] for l in range(depth)]
    m = lp[0]
    for l in range(1, depth):
        m = jnp.maximum(m, lp[l])
    e = [jnp.exp(v - m) for v in lp]
    tot = e[0]
    for l in range(1, depth):
        tot = tot + e[l]
    probs = [v / tot for v in e]
    cum = probs[0]
    o_ref[0] = cum - probs[0]
    for l in range(1, depth):
        cum = cum + probs[l]
        o_ref[l] = cum - probs[0]


def _lower_bounds(lb_param):
    return pl.pallas_call(
        _lower_bound_kernel,
        out_shape=jax.ShapeDtypeStruct(lb_param.shape, F32),
        name="hgrn_lower_bounds",
    )(lb_param.astype(F32))


def _row_select(tile_rows, n_lat, ctx_vec, lat_vec):
    row = pl.program_id(1) * tile_rows + lax.broadcasted_iota(jnp.int32, (tile_rows, 1), 0)
    return jnp.where(row >= n_lat, ctx_vec, lat_vec)


def _inproj_kernel(x_ref, modb_ref, modc_ref, g1_ref, w_in_ref, qg_ref, wq_ref, kvg_ref, wkv_ref, cos_ref, sin_ref,
                   q_ref, k_ref, v_ref, hg_ref, fn_ref, *, q_scale, n_lat):
    x = x_ref[0]
    tm = x.shape[0]
    shift = _row_select(tm, n_lat, modc_ref[0, 0:1, :], modb_ref[0, 0:1, :])
    scale = _row_select(tm, n_lat, modc_ref[0, 1:2, :], modb_ref[0, 1:2, :])
    xn = x * lax.rsqrt(jnp.mean(x * x, axis=-1, keepdims=True) + EPS) * g1_ref[...]
    xm = (xn * (1.0 + scale) + shift).astype(BF16)

    hg_ref[0] = _dot(xm, w_in_ref[:, P_HG:P_FN]).astype(BF16)
    fn_ref[0] = _dot(xm, w_in_ref[:, P_FN:P_WIDTH]).astype(BF16)
    p = _dot(xm, w_in_ref[:, 0:P_HG])

    cos = cos_ref[...]
    sin = sin_ref[...]

    cq = p[:, P_CQ:P_CKV]
    cqn = (cq * lax.rsqrt(jnp.mean(cq * cq, axis=-1, keepdims=True) + EPS) * qg_ref[...]).astype(BF16)
    qq = _dot(cqn, wq_ref[...])
    nw = MLA_HEADS * MLA_NOPE
    for h in range(MLA_HEADS):
        q_ref[0, :, h * HEAD_PAD:h * HEAD_PAD + 128] = (qq[:, h * 128:(h + 1) * 128] * q_scale).astype(BF16)
        pe = qq[:, nw + h * 128:nw + (h + 1) * 128] * cos + qq[:, 2 * nw + h * 128:2 * nw + (h + 1) * 128] * sin
        q_ref[0, :, h * HEAD_PAD + 128:(h + 1) * HEAD_PAD] = (pe * q_scale).astype(BF16)

    ckv = p[:, P_CKV:P_KPA]
    ckvn = (ckv * lax.rsqrt(jnp.mean(ckv * ckv, axis=-1, keepdims=True) + EPS) * kvg_ref[...]).astype(BF16)
    kv = _dot(ckvn, wkv_ref[...])
    kpe = (p[:, P_KPA:P_KPB] * cos + p[:, P_KPB:P_HG] * sin).astype(BF16)
    for h in range(MLA_HEADS):
        k_ref[0, :, h * HEAD_PAD:h * HEAD_PAD + 128] = kv[:, h * 128:(h + 1) * 128].astype(BF16)
        k_ref[0, :, h * HEAD_PAD + 128:(h + 1) * HEAD_PAD] = kpe
    v_ref[0] = kv[:, nw:].astype(BF16)


def _inproj(xa, mod, g1, w_in_p, qg, wq_all, kvg, wkv_p, cos_t, sin_t, n_lat, ctx_row):
    b, s, d = xa.shape
    tm = STREAM_TILE
    row = lambda w: pl.BlockSpec((1, tm, w), lambda i, j: (i, j, 0))
    return pl.pallas_call(
        functools.partial(_inproj_kernel, q_scale=math.log2(math.e) / math.sqrt(MLA_NOPE + MLA_ROPE), n_lat=n_lat),
        grid=(b, s // tm),
        in_specs=[
            row(d),
            pl.BlockSpec((1, 6, d), lambda i, j: (i, 0, 0)),
            pl.BlockSpec((1, 6, d), lambda i, j: (ctx_row, 0, 0)),
            _const_spec((1, d)),
            _RESIDENT,
            _const_spec((1, MLA_Q_LORA)),
            _RESIDENT,
            _const_spec((1, MLA_KV_LORA)),
            _RESIDENT,
            pl.BlockSpec((tm, 128), lambda i, j: (j, 0)),
            pl.BlockSpec((tm, 128), lambda i, j: (j, 0)),
        ],
        out_specs=[row(MLA_HEADS * HEAD_PAD), row(MLA_HEADS * HEAD_PAD), row(MLA_WIDTH), row(HG_IN_WIDTH),
                   row(FN_WIDTH)],
        out_shape=[
            jax.ShapeDtypeStruct((b, s, MLA_HEADS * HEAD_PAD), BF16),
            jax.ShapeDtypeStruct((b, s, MLA_HEADS * HEAD_PAD), BF16),
            jax.ShapeDtypeStruct((b, s, MLA_WIDTH), BF16),
            jax.ShapeDtypeStruct((b, s, HG_IN_WIDTH), BF16),
            jax.ShapeDtypeStruct((b, s, FN_WIDTH), BF16),
        ],
        compiler_params=_params(2),
        name="inproj",
    )(xa, mod, mod, g1.reshape(1, d), w_in_p, qg.reshape(1, -1), wq_all, kvg.reshape(1, -1), wkv_p, cos_t, sin_t)


def _attend(q_ref, k_ref, v_ref, o_ref, key_lo, key_hi):
    for h in range(MLA_HEADS):
        qh = q_ref[0, :, h * HEAD_PAD:(h + 1) * HEAD_PAD]
        kh = k_ref[0, key_lo:key_hi, h * HEAD_PAD:(h + 1) * HEAD_PAD]
        vh = v_ref[0, key_lo:key_hi, h * MLA_V:(h + 1) * MLA_V]
        s = _dot_nt(qh, kh)
        e = jnp.exp2(s - jnp.max(s, axis=-1, keepdims=True))
        denom = jnp.sum(e, axis=-1, keepdims=True)
        o = _dot(e.astype(BF16), vh) * (1.0 / denom)
        o_ref[0, :, h * MLA_V:(h + 1) * MLA_V] = o.astype(BF16)


def _attn_kernel(q_ref, k_ref, v_ref, o_ref):
    _attend(q_ref, k_ref, v_ref, o_ref, 0, k_ref.shape[1])


def _attn_ctx_kernel(q_ref, qc_ref, k_ref, v_ref, o_ref, oc_ref, *, n_lat):
    s_len = k_ref.shape[1]
    _attend(q_ref, k_ref, v_ref, o_ref, 0, s_len)

    @pl.when(pl.program_id(1) == pl.num_programs(1) - 1)
    def _():
        _attend(qc_ref, k_ref, v_ref, oc_ref, n_lat, s_len)


def _attention(q, k, v, n_lat, with_ctx_queries):
    b, s, _ = q.shape
    tq = ATTN_TILE
    n_ctx = s - n_lat
    q_spec = pl.BlockSpec((1, tq, q.shape[2]), lambda i, j: (i, j, 0))
    kv_specs = [pl.BlockSpec((1, s, k.shape[2]), lambda i, j: (i, 0, 0)),
                pl.BlockSpec((1, s, v.shape[2]), lambda i, j: (i, 0, 0))]
    o_spec = pl.BlockSpec((1, tq, MLA_WIDTH), lambda i, j: (i, j, 0))
    o_shape = jax.ShapeDtypeStruct((b, n_lat, MLA_WIDTH), BF16)
    if not with_ctx_queries:
        out = pl.pallas_call(
            _attn_kernel,
            grid=(b, n_lat // tq),
            in_specs=[q_spec] + kv_specs,
            out_specs=o_spec,
            out_shape=o_shape,
            compiler_params=_params(2),
            name="attention",
        )(q, k, v)
        return out, None
    assert n_lat % n_ctx == 0
    return pl.pallas_call(
        functools.partial(_attn_ctx_kernel, n_lat=n_lat),
        grid=(b, n_lat // tq),
        in_specs=[q_spec, pl.BlockSpec((1, n_ctx, q.shape[2]), lambda i, j: (i, n_lat // n_ctx, 0))] + kv_specs,
        out_specs=[o_spec, pl.BlockSpec((1, n_ctx, MLA_WIDTH), lambda i, j: (i, 0, 0))],
        out_shape=[o_shape, jax.ShapeDtypeStruct((b, n_ctx, MLA_WIDTH), BF16)],
        compiler_params=_params(2),
        name="attention_ctx",
    )(q, q, k, v)


def _hgrn_constants():
    c = HG_CHUNK
    t = np.arange(c)[:, None]
    u = np.arange(c)[None, :]
    blocks = [(u <= t)]
    for m in (8, 4, 2):
        half = m // 2
        mid = (t // m) * m + half
        lower = (t % m) >= half
        blocks.append(np.where(lower, (u >= mid) & (u <= t), (u >= t + 1) & (u <= mid - 1)))
    mf = np.stack([blk.astype(np.float32) for blk in blocks])
    mb = np.ascontiguousarray(mf[:, ::-1, ::-1])
    msb = np.floor(np.log2(np.maximum(t ^ u, 1))).astype(np.int32)
    lvl = np.where(t > u, HG_LEVELS - 1 - msb, np.where(t == u, HG_LEVELS, -1)).astype(np.int32)
    lvl_f = np.tile(lvl, (1, 2))
    lvl_b = np.tile(np.ascontiguousarray(lvl.T), (1, 2))
    return mf.reshape(-1, c), mb.reshape(-1, c), lvl_f, lvl_b


def _pair_weights(x, p):
    zeros = jnp.zeros((x.shape[0], HG_K), x.dtype)
    xa = x[:, 2 * p * HG_K:(2 * p + 1) * HG_K]
    xb = x[:, (2 * p + 1) * HG_K:(2 * p + 2) * HG_K]
    blk = jnp.concatenate([jnp.concatenate([xa, zeros], axis=1), jnp.concatenate([zeros, xb], axis=1)], axis=0)
    return blk.T


def _hgrn_chunk(hq, z, vv, lb, m_ref, lvl_masks, v_lo, v_hi, state_ref, fwd):
    c = HG_CHUNK
    f = lb + (1.0 - lb) * _sigmoid(z)
    g = jnp.log2(f)
    kk = 1.0 - f
    qq = _silu(hq)
    g_hi = g.astype(BF16)
    g_lo = (g - g_hi.astype(F32)).astype(BF16)
    m_all = m_ref[...]
    eb = _dot(m_all, g_hi) + _dot(m_all, g_lo)
    bc = eb[0:c]
    total = bc[c - 1:c] if fwd else bc[0:1]
    row = lax.broadcasted_iota(jnp.int32, (c, HG_KW), 0)
    xs = []
    for l in range(HG_LEVELS):
        m = c >> l
        half = m // 2
        bit = (row & half) != 0
        is_query = bit if fwd else jnp.logical_not(bit)
        if l < HG_LEVELS - HG_MATMUL_LEVELS:
            refs = [bc[b0 + half - 1:b0 + half] if fwd else bc[b0 + half:b0 + half + 1] for b0 in range(0, c, m)]
            ref = jnp.concatenate([jnp.broadcast_to(r, (m, HG_KW)) for r in refs], axis=0)
            diff = bc - ref
            expo = jnp.where(is_query, diff, -diff)
        else:
            idx = 1 + l - (HG_LEVELS - HG_MATMUL_LEVELS)
            expo = eb[idx * c:(idx + 1) * c]
        xs.append((jnp.where(is_query, qq, kk) * jnp.exp2(expo)).astype(BF16))
    q_in = (qq * jnp.exp2(bc)).astype(BF16)
    k_up = (kk * jnp.exp2(total - bc)).astype(BF16)
    qb = qq.astype(BF16)
    kb = kk.astype(BF16)
    chunk_decay = jnp.exp2(total)
    outs = []
    for p in range(HG_HEADS // 2):
        kp = slice(2 * p * HG_K, (2 * p + 2) * HG_K)
        vp = slice(2 * p * HG_V, (2 * p + 2) * HG_V)
        sc = jnp.where(lvl_masks[HG_LEVELS], _dot(qb[:, kp], _pair_weights(kb, p)), 0.0)
        for l in range(HG_LEVELS):
            sc = jnp.where(lvl_masks[l], _dot(xs[l][:, kp], _pair_weights(xs[l], p)), sc)
        v_pair = vv[:, vp]
        v_diag = jnp.concatenate([v_pair * v_lo, v_pair * v_hi], axis=0)
        st_pair = state_ref[2 * p * HG_V:(2 * p + 2) * HG_V, kp]
        outs.append(_dot(sc.astype(BF16), v_diag) + _dot(q_in[:, kp], st_pair.astype(BF16).T))
        for h in (2 * p, 2 * p + 1):
            ks = slice(h * HG_K, (h + 1) * HG_K)
            vs = slice(h * HG_V, (h + 1) * HG_V)
            state_ref[vs, ks] = chunk_decay[:, ks] * state_ref[vs, ks] + _dot_tn(vv[:, vs], k_up[:, ks])
    return jnp.concatenate(outs, axis=1)


def _hgrn_kernel(x_ref, lb_ref, gain_ref, mf_ref, mb_ref, lvlf_ref, lvlb_ref, ones_ref, o_ref,
                 of_ref, ob_ref, sf_ref, sb_ref, *, n_lat_chunks):
    c = HG_CHUNK
    s_len = x_ref.shape[1]
    n_chunks = s_len // c
    n_ctx_chunks = n_chunks - n_lat_chunks
    sf_ref[...] = jnp.zeros_like(sf_ref)
    sb_ref[...] = jnp.zeros_like(sb_ref)
    lvl_f = lvlf_ref[...]
    lvl_b = lvlb_ref[...]
    masks_f = [lvl_f == l for l in range(HG_LEVELS + 1)]
    masks_b = [lvl_b == l for l in range(HG_LEVELS + 1)]
    lb_f = lb_ref[0:1, :]
    lb_b = lb_ref[1:2, :]
    v_lane = lax.broadcasted_iota(jnp.int32, (c, 2 * HG_V), 1)
    v_lo = (v_lane < HG_V).astype(F32).astype(BF16)
    v_hi = (v_lane >= HG_V).astype(F32).astype(BF16)

    def load(ci):
        r0 = pl.multiple_of(ci * c, c)
        hq = x_ref[0, pl.ds(r0, c), 0:HG_KW].astype(F32)
        vv = x_ref[0, pl.ds(r0, c), 3 * HG_KW:3 * HG_KW + HG_WIDTH]
        return r0, hq, vv

    def step(i, carry):
        cf = jnp.where(i < n_ctx_chunks, n_lat_chunks + i, i - n_ctx_chunks)
        r0, hq, vv = load(cf)
        zf = x_ref[0, pl.ds(r0, c), HG_KW:2 * HG_KW].astype(F32)
        of_ref[pl.ds(r0, c), :] = _hgrn_chunk(hq, zf, vv, lb_f, mf_ref, masks_f, v_lo, v_hi, sf_ref, True)
        cb = jnp.where(i < n_ctx_chunks, n_chunks - 1 - i, n_lat_chunks - 1 - (i - n_ctx_chunks))
        r0, hq, vv = load(cb)
        zb = x_ref[0, pl.ds(r0, c), 2 * HG_KW:3 * HG_KW].astype(F32)
        ob_ref[pl.ds(r0, c), :] = _hgrn_chunk(hq, zb, vv, lb_b, mb_ref, masks_b, v_lo, v_hi, sb_ref, False)
        return carry

    lax.fori_loop(0, n_chunks, step, 0, unroll=2)

    rt = 256
    gain = gain_ref[...]
    ones = ones_ref[...]

    def readout(i, carry):
        r0 = pl.multiple_of(i * rt, rt)
        o = of_ref[pl.ds(r0, rt), :] + ob_ref[pl.ds(r0, rt), :]
        sq_hi, sq_mid, sq_lo = _split3(o * o)
        msq = _dot(sq_hi, ones) + _dot(sq_mid, ones) + _dot(sq_lo, ones)
        zg = x_ref[0, pl.ds(r0, rt), 3 * HG_KW + HG_WIDTH:HG_IN_WIDTH].astype(F32)
        y = o * lax.rsqrt(msq + EPS) * gain * _silu(zg)
        o_ref[0, pl.ds(r0, rt), :] = y.astype(BF16)
        return carry

    lax.fori_loop(0, o_ref.shape[1] // rt, readout, 0)


def _hgrn(hg_in, lb, gain, n_lat, with_ctx_outputs):
    b, s, w = hg_in.shape
    mf, mb, lvl_f, lvl_b = _hgrn_constants()
    ones = np.kron(np.eye(HG_HEADS, dtype=np.float32), np.full((HG_V, HG_V), 1.0 / HG_V, np.float32))
    n_out = s if with_ctx_outputs else n_lat
    return pl.pallas_call(
        functools.partial(_hgrn_kernel, n_lat_chunks=n_lat // HG_CHUNK),
        grid=(b,),
        in_specs=[
            pl.BlockSpec((1, s, w), lambda i: (i, 0, 0)),
            _const_spec((2, HG_KW)),
            _const_spec((1, HG_WIDTH)),
            _const_spec(mf.shape),
            _const_spec(mb.shape),
            _const_spec(lvl_f.shape),
            _const_spec(lvl_b.shape),
            _const_spec(ones.shape),
        ],
        out_specs=pl.BlockSpec((1, n_out, HG_WIDTH), lambda i: (i, 0, 0)),
        out_shape=jax.ShapeDtypeStruct((b, n_out, HG_WIDTH), BF16),
        scratch_shapes=[
            pltpu.VMEM((s, HG_WIDTH), F32),
            pltpu.VMEM((s, HG_WIDTH), F32),
            pltpu.VMEM((HG_WIDTH, HG_KW), F32),
            pltpu.VMEM((HG_WIDTH, HG_KW), F32),
        ],
        compiler_params=_params(1),
        name="hgrn2",
    )(hg_in, lb, gain.reshape(1, HG_WIDTH), jnp.asarray(mf, BF16), jnp.asarray(mb, BF16),
      jnp.asarray(lvl_f), jnp.asarray(lvl_b), jnp.asarray(ones, BF16))


def _dft_tables(n):
    idx = np.arange(n, dtype=np.int64)
    ang = 2.0 * np.pi * ((idx[:, None] * idx[None, :]) % n).astype(np.float64) / n
    return np.cos(ang).astype(np.float32), np.sin(ang).astype(np.float32)


def _fourier_tile(z, ct, st, cg, sg, w, norm):
    zc = _dot(ct, z).astype(BF16)
    zs = _dot(st, z).astype(BF16)
    mixed = (_dot(zc, cg) - _dot(zs, sg)) * norm
    return _dot(mixed.astype(BF16), w).astype(BF16)


def _fourier_kernel(z_ref, cl_ref, sl_ref, ct_ref, st_ref, cg_ref, sg_ref, w_ref, o_ref, *, n_lat, with_ctx_outputs):
    s_len = z_ref.shape[1]
    n_ctx = s_len - n_lat
    cg = cg_ref[...]
    sg = sg_ref[...]
    w = w_ref[...]

    def latent():
        o_ref[0] = _fourier_tile(z_ref[0, 0:n_lat, :], ct_ref[...], st_ref[...], cg, sg, w,
                                 1.0 / math.sqrt(n_lat * FN_GROUP_DIM))

    def context():
        o_ref[0] = _fourier_tile(z_ref[0, n_lat:s_len, :], cl_ref[...], sl_ref[...], cg, sg, w,
                                 1.0 / math.sqrt(n_ctx * FN_GROUP_DIM))

    if not with_ctx_outputs:
        latent()
    else:
        j = pl.program_id(1)
        n_lat_blocks = n_lat // o_ref.shape[1]
        pl.when(j < n_lat_blocks)(latent)
        pl.when(j >= n_lat_blocks)(context)


def _fourier(fn_in, w_fourier, n_lat, with_ctx_outputs):
    b, s, w = fn_in.shape
    n_ctx = s - n_lat
    tn = FOURIER_TILE
    assert n_ctx == tn
    n_lat_blocks = n_lat // tn
    n_rows = s if with_ctx_outputs else n_lat
    cl, sl = _dft_tables(n_ctx)
    ct, st = _dft_tables(n_lat)
    cgrp, sgrp = _dft_tables(FN_GROUP_DIM)
    eye = np.eye(FN_GROUPS, dtype=np.float32)
    cg = np.kron(eye, cgrp)
    sg = np.kron(eye, sgrp)

    def lat_map(i, j):
        return (jnp.minimum(j, n_lat_blocks - 1), 0)

    return pl.pallas_call(
        functools.partial(_fourier_kernel, n_lat=n_lat, with_ctx_outputs=with_ctx_outputs),
        grid=(b, n_rows // tn),
        in_specs=[
            pl.BlockSpec((1, s, w), lambda i, j: (i, 0, 0)),
            _const_spec(cl.shape),
            _const_spec(sl.shape),
            pl.BlockSpec((tn, n_lat), lat_map),
            pl.BlockSpec((tn, n_lat), lat_map),
            _const_spec(cg.shape),
            _const_spec(sg.shape),
            _const_spec(w_fourier.shape),
        ],
        out_specs=pl.BlockSpec((1, tn, FN_WIDTH), lambda i, j: (i, j, 0)),
        out_shape=jax.ShapeDtypeStruct((b, n_rows, FN_WIDTH), BF16),
        compiler_params=_params(2),
        name="fourier",
    )(fn_in, jnp.asarray(cl, BF16), jnp.asarray(sl, BF16), jnp.asarray(ct, BF16), jnp.asarray(st, BF16),
      jnp.asarray(cg, BF16), jnp.asarray(sg, BF16), w_fourier)


def _ffn_kernel(x_ref, attn_ref, hg_ref, fn_ref, modb_ref, modc_ref, g2_ref, wo_ref, wgu_ref, wd_ref, gf_ref, o_ref,
                act_ref, *, final_norm, n_lat, select_ctx):
    x = x_ref[0]
    tm = x.shape[0]

    def mod_vec(r):
        if select_ctx:
            return _row_select(tm, n_lat, modc_ref[0, r:r + 1, :], modb_ref[0, r:r + 1, :])
        return modb_ref[0, r:r + 1, :]

    y = (_dot(attn_ref[0], wo_ref[0:MLA_WIDTH, :])
         + _dot(hg_ref[0], wo_ref[MLA_WIDTH:MLA_WIDTH + HG_WIDTH, :])
         + _dot(fn_ref[0], wo_ref[MLA_WIDTH + HG_WIDTH:MLA_WIDTH + HG_WIDTH + FN_WIDTH, :]))
    x1 = x + mod_vec(2) * y
    o_ref[0] = x1
    xn = x1 * lax.rsqrt(jnp.mean(x1 * x1, axis=-1, keepdims=True) + EPS) * g2_ref[...]
    hm = (xn * (1.0 + mod_vec(4)) + mod_vec(3)).astype(BF16)

    ft = wgu_ref.shape[2] // 2
    for t in range(wgu_ref.shape[0]):
        gu = _dot(hm, wgu_ref[t])
        act_ref[:, t * ft:(t + 1) * ft] = (_silu(gu[:, 0:ft]) * gu[:, ft:2 * ft]).astype(BF16)
    out = o_ref[0] + mod_vec(5) * _dot(act_ref[...], wd_ref[...])
    if final_norm:
        out = out * lax.rsqrt(jnp.mean(out * out, axis=-1, keepdims=True) + EPS) * gf_ref[...]
    o_ref[0] = out


def _outproj_ffn(xa, attn, hg, fn, mod, g2, w_out, w_gu, w_d, g_final, n_lat, ctx_row, with_ctx_rows, final_norm):
    b, s, d = xa.shape
    tm = STREAM_TILE if with_ctx_rows else LATENT_TILE
    n_rows = s if with_ctx_rows else n_lat
    hidden = w_d.shape[0]
    row = lambda w: pl.BlockSpec((1, tm, w), lambda i, j: (i, j, 0))
    return pl.pallas_call(
        functools.partial(_ffn_kernel, final_norm=final_norm, n_lat=n_lat, select_ctx=with_ctx_rows),
        grid=(b, n_rows // tm),
        in_specs=[
            row(d),
            row(MLA_WIDTH),
            row(HG_WIDTH),
            row(FN_WIDTH),
            pl.BlockSpec((1, 6, d), lambda i, j: (i, 0, 0)),
            pl.BlockSpec((1, 6, d), lambda i, j: (ctx_row, 0, 0)),
            _const_spec((1, d)),
            _RESIDENT,
            _RESIDENT,
            _RESIDENT,
            _const_spec((1, d)),
        ],
        out_specs=row(d),
        out_shape=jax.ShapeDtypeStruct((b, n_rows, d), F32),
        scratch_shapes=[pltpu.VMEM((tm, hidden), BF16)],
        compiler_params=_params(2),
        name="outproj_ffn",
    )(xa, attn, hg, fn, mod, mod, g2.reshape(1, d), w_out, w_gu, w_d, g_final.reshape(1, d))


def _rotate_half_cols(w):
    r1, r2, c1, c2 = jnp.split(w, 4, axis=-1)
    return jnp.concatenate([-r2, r1, -c2, c1], axis=-1)


def _pack_w_in(w):
    d = w.shape[0]
    zeros = jnp.zeros((d, 128 - MLA_ROPE), w.dtype)
    w_kr = w[:, O_KR:O_HQ]
    return jnp.concatenate([w[:, O_CQ:O_KR], w_kr, zeros, _rotate_half_cols(w_kr), zeros, w[:, O_HQ:]],
                           axis=1).astype(BF16)


def _pack_w_uq(w):
    r = w.shape[0]
    wh = w.reshape(r, MLA_HEADS, MLA_NOPE + MLA_ROPE)
    nope = wh[:, :, :MLA_NOPE].reshape(r, MLA_HEADS * MLA_NOPE)
    pe = wh[:, :, MLA_NOPE:]
    zeros = jnp.zeros((r, MLA_HEADS, 128 - MLA_ROPE), w.dtype)
    pe_a = jnp.concatenate([pe, zeros], axis=-1).reshape(r, MLA_HEADS * 128)
    pe_b = jnp.concatenate([_rotate_half_cols(pe), zeros], axis=-1).reshape(r, MLA_HEADS * 128)
    return jnp.concatenate([nope, pe_a, pe_b], axis=1).astype(BF16)


def _pack_w_ukv(w):
    r = w.shape[0]
    wh = w.reshape(r, MLA_HEADS, MLA_NOPE + MLA_V)
    k_nope = wh[:, :, :MLA_NOPE].reshape(r, MLA_HEADS * MLA_NOPE)
    v = wh[:, :, MLA_NOPE:].reshape(r, MLA_HEADS * MLA_V)
    return jnp.concatenate([k_nope, v], axis=1).astype(BF16)


def _pack_w_gate_up(w, hidden):
    d = w.shape[0]
    n_ft = hidden // FFN_TILE
    gate = w[:, :hidden].reshape(d, n_ft, FFN_TILE)
    up = w[:, hidden:].reshape(d, n_ft, FFN_TILE)
    return jnp.concatenate([gate, up], axis=2).transpose(1, 0, 2).astype(BF16)


def _rope_tables(n_lat, n_ctx):
    rows = n_lat // GRID_W
    row_pos = np.repeat(np.arange(rows, dtype=np.float32), GRID_W)
    col_pos = np.tile(np.arange(GRID_W, dtype=np.float32), rows)
    axis_dim = MLA_ROPE // 2
    inv_freq = (ROPE_BASE ** (-np.arange(0, axis_dim, 2, dtype=np.float32) / axis_dim)).astype(np.float32)
    ang_r = row_pos[:, None] * inv_freq
    ang_c = col_pos[:, None] * inv_freq
    ang = np.concatenate([ang_r, ang_r, ang_c, ang_c], axis=-1)
    cos = np.ones((n_lat + n_ctx, 128), np.float32)
    sin = np.zeros((n_lat + n_ctx, 128), np.float32)
    cos[:n_lat, :MLA_ROPE] = np.cos(ang)
    sin[:n_lat, :MLA_ROPE] = np.sin(ang)
    return jnp.asarray(cos), jnp.asarray(sin)


def kernel(x, c, ctx, c_ctx, w_mod, b_mod, norm1_g, norm2_g, w_in, q_norm_g, w_uq, kv_norm_g, w_ukv, lb_param,
           hg_norm_g, w_fourier, w_out, w_gate_up, w_down, final_norm_g):
    b, t, d = x.shape
    n_ctx = ctx.shape[1]
    s = t + n_ctx
    depth = w_mod.shape[0]
    hidden = w_down.shape[1]
    assert s % STREAM_TILE == 0 and t % LATENT_TILE == 0 and t % ATTN_TILE == 0 and n_ctx == FOURIER_TILE
    assert hidden % FFN_TILE == 0 and t % GRID_W == 0

    mod_rows = -(-(b + 1) // 8) * 8
    c_all = jnp.zeros((mod_rows, d), F32).at[:b].set(c).at[b].set(c_ctx)
    mod_all = _modulation(c_all, w_mod, b_mod).reshape(depth, mod_rows, 6, d)
    lower = _lower_bounds(lb_param)
    cos_t, sin_t = _rope_tables(t, n_ctx)

    xa = jnp.concatenate([x, ctx], axis=1)
    for l in range(depth):
        last = l == depth - 1
        q, k, v, hg_in, fn_in = _inproj(xa, mod_all[l], norm1_g[l], _pack_w_in(w_in[l]), q_norm_g[l],
                                        _pack_w_uq(w_uq[l]), kv_norm_g[l], _pack_w_ukv(w_ukv[l]), cos_t, sin_t, t, b)
        attn, attn_ctx = _attention(q, k, v, t, not last)
        if not last:
            attn = jnp.concatenate([attn, attn_ctx], axis=1)
        hg = _hgrn(hg_in, lower[l], hg_norm_g[l], t, not last)
        fn = _fourier(fn_in, w_fourier[l].astype(BF16), t, not last)
        xa = _outproj_ffn(xa, attn, hg, fn, mod_all[l], norm2_g[l], w_out[l].astype(BF16),
                          _pack_w_gate_up(w_gate_up[l], hidden), w_down[l].astype(BF16), final_norm_g, t, b,
                          not last, last)
    return xa
```

```python
import functools
import math

import numpy as np
import jax
import jax.numpy as jnp
from jax import lax
from jax.experimental import pallas as pl
from jax.experimental.pallas import tpu as pltpu

EPS = 1e-6
GRID_W = 64
ROPE_BASE = 10000.0

MLA_HEADS = 4
MLA_Q_LORA = 256
MLA_KV_LORA = 128
MLA_NOPE = 128
MLA_ROPE = 64
MLA_V = 128
MLA_WIDTH = MLA_HEADS * MLA_V
HEAD_PAD = 256

HG_HEADS = 4
HG_K = 128
HG_V = 64
HG_WIDTH = HG_HEADS * HG_V
HG_KW = HG_HEADS * HG_K
HG_CHUNK = 64
HG_LEVELS = 6
HG_MATMUL_LEVELS = 3

FN_GROUPS = 4
FN_GROUP_DIM = 64
FN_WIDTH = FN_GROUPS * FN_GROUP_DIM

O_CQ = 0
O_CKV = O_CQ + MLA_Q_LORA
O_KR = O_CKV + MLA_KV_LORA
O_HQ = O_KR + MLA_ROPE

P_CQ = 0
P_CKV = P_CQ + MLA_Q_LORA
P_KPA = P_CKV + MLA_KV_LORA
P_KPB = P_KPA + 128
P_HG = P_KPB + 128
HG_IN_WIDTH = 3 * HG_KW + 2 * HG_WIDTH
P_FN = P_HG + HG_IN_WIDTH
P_WIDTH = P_FN + FN_WIDTH

STREAM_TILE = 768
LATENT_TILE = 1024
ATTN_TILE = 512
FOURIER_TILE = 512
FFN_TILE = 256
VMEM_LIMIT = 56 * 1024 * 1024

BF16 = jnp.bfloat16
F32 = jnp.float32


def _dot(a, b):
    return jnp.dot(a, b, preferred_element_type=F32)


def _dot_nt(a, b):
    return lax.dot_general(a, b, (((1,), (1,)), ((), ())), preferred_element_type=F32)


def _dot_tn(a, b):
    return lax.dot_general(a, b, (((0,), (0,)), ((), ())), preferred_element_type=F32)


def _silu(x):
    return x * (1.0 / (1.0 + jnp.exp(-x)))


def _sigmoid(x):
    return 1.0 / (1.0 + jnp.exp(-x))


def _split3(x):
    hi = x.astype(BF16)
    r1 = x - hi.astype(F32)
    mid = r1.astype(BF16)
    lo = (r1 - mid.astype(F32)).astype(BF16)
    return hi, mid, lo


def _const_spec(shape):
    nd = len(shape)
    return pl.BlockSpec(shape, lambda *_: (0,) * nd)


_RESIDENT = pl.BlockSpec(memory_space=pltpu.VMEM)


def _params(n_axes):
    return pltpu.CompilerParams(dimension_semantics=("arbitrary",) * n_axes, vmem_limit_bytes=VMEM_LIMIT)


def _mod_kernel(c_ref, w_ref, b_ref, o_ref):
    act = _silu(c_ref[...]).astype(BF16)
    o_ref[0] = _dot(act, w_ref[0].astype(BF16)) + b_ref[0]


def _modulation(c_all, w_mod, b_mod):
    depth, d, n6 = w_mod.shape
    rows = c_all.shape[0]
    tn = n6 // 6
    return pl.pallas_call(
        _mod_kernel,
        grid=(depth, n6 // tn),
        in_specs=[
            pl.BlockSpec((rows, d), lambda l, j: (0, 0)),
            pl.BlockSpec((1, d, tn), lambda l, j: (l, 0, j)),
            pl.BlockSpec((1, 1, tn), lambda l, j: (l, 0, j)),
        ],
        out_specs=pl.BlockSpec((1, rows, tn), lambda l, j: (l, 0, j)),
        out_shape=jax.ShapeDtypeStruct((depth, rows, n6), F32),
        name="modulation",
    )(c_all, w_mod, b_mod.reshape(depth, 1, n6))


def _lower_bound_kernel(lb_ref, o_ref):
    depth = lb_ref.shape[0]
    lp = [lb_ref[l] for l in range(depth)]
    m = lp[0]
    for l in range(1, depth):
        m = jnp.maximum(m, lp[l])
    e = [jnp.exp(v - m) for v in lp]
    tot = e[0]
    for l in range(1, depth):
        tot = tot + e[l]
    probs = [v / tot for v in e]
    cum = probs[0]
    o_ref[0] = cum - probs[0]
    for l in range(1, depth):
        cum = cum + probs[l]
        o_ref[l] = cum - probs[0]


def _lower_bounds(lb_param):
    return pl.pallas_call(
        _lower_bound_kernel,
        out_shape=jax.ShapeDtypeStruct(lb_param.shape, F32),
        name="hgrn_lower_bounds",
    )(lb_param.astype(F32))


def _row_select(tile_rows, n_lat, ctx_vec, lat_vec):
    row = pl.program_id(1) * tile_rows + lax.broadcasted_iota(jnp.int32, (tile_rows, 1), 0)
    return jnp.where(row >= n_lat, ctx_vec, lat_vec)


def _stream_specs(src, tm):
    if not isinstance(src, tuple):
        return [pl.BlockSpec((1, tm, src.shape[2]), lambda i, j: (i, j, 0))], [src]
    lat, ctx = src
    t, d = lat.shape[1:]
    n_ctx = ctx.shape[1]
    n_full = t // tm
    tail = t - n_full * tm
    assert n_full >= 1 and tail > 0 and tail + n_ctx == tm and (n_full * tm) % tail == 0
    specs = [pl.BlockSpec((1, tm, d), lambda i, j: (i, jnp.minimum(j, n_full - 1), 0)),
             pl.BlockSpec((1, tail, d), lambda i, j: (i, (n_full * tm) // tail, 0)),
             pl.BlockSpec((1, n_ctx, d), lambda i, j: (i, 0, 0))]
    return specs, [lat, lat, ctx]


def _stream_tile(refs):
    if len(refs) == 1:
        return refs[0][0]
    main_ref, tail_ref, ctx_ref = refs
    mixed = jnp.concatenate([tail_ref[0], ctx_ref[0]], axis=0)
    return jnp.where(pl.program_id(1) < pl.num_programs(1) - 1, main_ref[0], mixed)


def _inproj_kernel(*refs, q_scale, n_lat, n_src):
    (modb_ref, modc_ref, g1_ref, w_in_ref, qg_ref, wq_ref, kvg_ref, wkv_ref, cos_ref, sin_ref,
     q_ref, k_ref, v_ref, hg_ref, fn_ref) = refs[n_src:]
    x = _stream_tile(refs[:n_src])
    tm = x.shape[0]
    shift = _row_select(tm, n_lat, modc_ref[0, 0:1, :], modb_ref[0, 0:1, :])
    scale = _row_select(tm, n_lat, modc_ref[0, 1:2, :], modb_ref[0, 1:2, :])
    xn = x * lax.rsqrt(jnp.mean(x * x, axis=-1, keepdims=True) + EPS) * g1_ref[...]
    xm = (xn * (1.0 + scale) + shift).astype(BF16)

    hg_ref[0] = _dot(xm, w_in_ref[:, P_HG:P_FN]).astype(BF16)
    fn_ref[0] = _dot(xm, w_in_ref[:, P_FN:P_WIDTH]).astype(BF16)
    p = _dot(xm, w_in_ref[:, 0:P_HG])

    cos = cos_ref[...]
    sin = sin_ref[...]

    cq = p[:, P_CQ:P_CKV]
    cqn = (cq * lax.rsqrt(jnp.mean(cq * cq, axis=-1, keepdims=True) + EPS) * qg_ref[...]).astype(BF16)
    qq = _dot(cqn, wq_ref[...])
    nw = MLA_HEADS * MLA_NOPE
    for h in range(MLA_HEADS):
        q_ref[0, :, h * HEAD_PAD:h * HEAD_PAD + 128] = (qq[:, h * 128:(h + 1) * 128] * q_scale).astype(BF16)
        pe = qq[:, nw + h * 128:nw + (h + 1) * 128] * cos + qq[:, 2 * nw + h * 128:2 * nw + (h + 1) * 128] * sin
        q_ref[0, :, h * HEAD_PAD + 128:(h + 1) * HEAD_PAD] = (pe * q_scale).astype(BF16)

    ckv = p[:, P_CKV:P_KPA]
    ckvn = (ckv * lax.rsqrt(jnp.mean(ckv * ckv, axis=-1, keepdims=True) + EPS) * kvg_ref[...]).astype(BF16)
    kv = _dot(ckvn, wkv_ref[...])
    kpe = (p[:, P_KPA:P_KPB] * cos + p[:, P_KPB:P_HG] * sin).astype(BF16)
    for h in range(MLA_HEADS):
        k_ref[0, :, h * HEAD_PAD:h * HEAD_PAD + 128] = kv[:, h * 128:(h + 1) * 128].astype(BF16)
        k_ref[0, :, h * HEAD_PAD + 128:(h + 1) * HEAD_PAD] = kpe
    v_ref[0] = kv[:, nw:].astype(BF16)


def _inproj(src, mod, g1, w_in_p, qg, wq_all, kvg, wkv_p, cos_t, sin_t, n_lat, ctx_row):
    s = cos_t.shape[0]
    d = g1.shape[0]
    b = (src[0] if isinstance(src, tuple) else src).shape[0]
    tm = STREAM_TILE
    row = lambda w: pl.BlockSpec((1, tm, w), lambda i, j: (i, j, 0))
    src_specs, src_args = _stream_specs(src, tm)
    return pl.pallas_call(
        functools.partial(_inproj_kernel, q_scale=math.log2(math.e) / math.sqrt(MLA_NOPE + MLA_ROPE), n_lat=n_lat,
                          n_src=len(src_args)),
        grid=(b, s // tm),
        in_specs=src_specs + [
            pl.BlockSpec((1, 6, d), lambda i, j: (i, 0, 0)),
            pl.BlockSpec((1, 6, d), lambda i, j: (ctx_row, 0, 0)),
            _const_spec((1, d)),
            _RESIDENT,
            _const_spec((1, MLA_Q_LORA)),
            _RESIDENT,
            _const_spec((1, MLA_KV_LORA)),
            _RESIDENT,
            pl.BlockSpec((tm, 128), lambda i, j: (j, 0)),
            pl.BlockSpec((tm, 128), lambda i, j: (j, 0)),
        ],
        out_specs=[row(MLA_HEADS * HEAD_PAD), row(MLA_HEADS * HEAD_PAD), row(MLA_WIDTH), row(HG_IN_WIDTH),
                   row(FN_WIDTH)],
        out_shape=[
            jax.ShapeDtypeStruct((b, s, MLA_HEADS * HEAD_PAD), BF16),
            jax.ShapeDtypeStruct((b, s, MLA_HEADS * HEAD_PAD), BF16),
            jax.ShapeDtypeStruct((b, s, MLA_WIDTH), BF16),
            jax.ShapeDtypeStruct((b, s, HG_IN_WIDTH), BF16),
            jax.ShapeDtypeStruct((b, s, FN_WIDTH), BF16),
        ],
        compiler_params=_params(2),
        name="inproj",
    )(*src_args, mod, mod, g1.reshape(1, d), w_in_p, qg.reshape(1, -1), wq_all, kvg.reshape(1, -1), wkv_p, cos_t,
      sin_t)


def _attend(q_ref, k_ref, v_ref, o_ref, key_lo, key_hi):
    for h in range(MLA_HEADS):
        qh = q_ref[0, :, h * HEAD_PAD:(h + 1) * HEAD_PAD]
        kh = k_ref[0, key_lo:key_hi, h * HEAD_PAD:(h + 1) * HEAD_PAD]
        vh = v_ref[0, key_lo:key_hi, h * MLA_V:(h + 1) * MLA_V]
        s = _dot_nt(qh, kh)
        e = jnp.exp2(s - jnp.max(s, axis=-1, keepdims=True))
        denom = jnp.sum(e, axis=-1, keepdims=True)
        o = _dot(e.astype(BF16), vh) * (1.0 / denom)
        o_ref[0, :, h * MLA_V:(h + 1) * MLA_V] = o.astype(BF16)


def _attn_kernel(q_ref, k_ref, v_ref, o_ref):
    _attend(q_ref, k_ref, v_ref, o_ref, 0, k_ref.shape[1])


def _attn_ctx_kernel(q_ref, qc_ref, k_ref, v_ref, o_ref, oc_ref, *, n_lat):
    s_len = k_ref.shape[1]
    _attend(q_ref, k_ref, v_ref, o_ref, 0, s_len)

    @pl.when(pl.program_id(1) == pl.num_programs(1) - 1)
    def _():
        _attend(qc_ref, k_ref, v_ref, oc_ref, n_lat, s_len)


def _attention(q, k, v, n_lat, with_ctx_queries):
    b, s, _ = q.shape
    tq = ATTN_TILE
    n_ctx = s - n_lat
    q_spec = pl.BlockSpec((1, tq, q.shape[2]), lambda i, j: (i, j, 0))
    kv_specs = [pl.BlockSpec((1, s, k.shape[2]), lambda i, j: (i, 0, 0)),
                pl.BlockSpec((1, s, v.shape[2]), lambda i, j: (i, 0, 0))]
    o_spec = pl.BlockSpec((1, tq, MLA_WIDTH), lambda i, j: (i, j, 0))
    o_shape = jax.ShapeDtypeStruct((b, n_lat, MLA_WIDTH), BF16)
    if not with_ctx_queries:
        out = pl.pallas_call(
            _attn_kernel,
            grid=(b, n_lat // tq),
            in_specs=[q_spec] + kv_specs,
            out_specs=o_spec,
            out_shape=o_shape,
            compiler_params=_params(2),
            name="attention",
        )(q, k, v)
        return out, None
    assert n_lat % n_ctx == 0
    return pl.pallas_call(
        functools.partial(_attn_ctx_kernel, n_lat=n_lat),
        grid=(b, n_lat // tq),
        in_specs=[q_spec, pl.BlockSpec((1, n_ctx, q.shape[2]), lambda i, j: (i, n_lat // n_ctx, 0))] + kv_specs,
        out_specs=[o_spec, pl.BlockSpec((1, n_ctx, MLA_WIDTH), lambda i, j: (i, 0, 0))],
        out_shape=[o_shape, jax.ShapeDtypeStruct((b, n_ctx, MLA_WIDTH), BF16)],
        compiler_params=_params(2),
        name="attention_ctx",
    )(q, q, k, v)


def _hgrn_constants():
    c = HG_CHUNK
    t = np.arange(c)[:, None]
    u = np.arange(c)[None, :]
    blocks = [(u <= t)]
    for m in (8, 4, 2):
        half = m // 2
        mid = (t // m) * m + half
        lower = (t % m) >= half
        blocks.append(np.where(lower, (u >= mid) & (u <= t), (u >= t + 1) & (u <= mid - 1)))
    mf = np.stack([blk.astype(np.float32) for blk in blocks])
    mb = np.ascontiguousarray(mf[:, ::-1, ::-1])
    msb = np.floor(np.log2(np.maximum(t ^ u, 1))).astype(np.int32)
    lvl = np.where(t > u, HG_LEVELS - 1 - msb, np.where(t == u, HG_LEVELS, -1)).astype(np.int32)
    lvl_f = np.tile(lvl, (1, 2))
    lvl_b = np.tile(np.ascontiguousarray(lvl.T), (1, 2))
    return mf.reshape(-1, c), mb.reshape(-1, c), lvl_f, lvl_b


def _pair_weights(x, p):
    zeros = jnp.zeros((x.shape[0], HG_K), x.dtype)
    xa = x[:, 2 * p * HG_K:(2 * p + 1) * HG_K]
    xb = x[:, (2 * p + 1) * HG_K:(2 * p + 2) * HG_K]
    blk = jnp.concatenate([jnp.concatenate([xa, zeros], axis=1), jnp.concatenate([zeros, xb], axis=1)], axis=0)
    return blk.T


def _hgrn_chunk(hq, z, vv, lb, m_ref, lvl_masks, v_lo, v_hi, state_ref, fwd):
    c = HG_CHUNK
    f = lb + (1.0 - lb) * _sigmoid(z)
    g = jnp.log2(f)
    kk = 1.0 - f
    qq = _silu(hq)
    g_hi = g.astype(BF16)
    g_lo = (g - g_hi.astype(F32)).astype(BF16)
    m_all = m_ref[...]
    eb = _dot(m_all, g_hi) + _dot(m_all, g_lo)
    bc = eb[0:c]
    total = bc[c - 1:c] if fwd else bc[0:1]
    row = lax.broadcasted_iota(jnp.int32, (c, HG_KW), 0)
    xs = []
    for l in range(HG_LEVELS):
        m = c >> l
        half = m // 2
        bit = (row & half) != 0
        is_query = bit if fwd else jnp.logical_not(bit)
        if l < HG_LEVELS - HG_MATMUL_LEVELS:
            refs = [bc[b0 + half - 1:b0 + half] if fwd else bc[b0 + half:b0 + half + 1] for b0 in range(0, c, m)]
            ref = jnp.concatenate([jnp.broadcast_to(r, (m, HG_KW)) for r in refs], axis=0)
            diff = bc - ref
            expo = jnp.where(is_query, diff, -diff)
        else:
            idx = 1 + l - (HG_LEVELS - HG_MATMUL_LEVELS)
            expo = eb[idx * c:(idx + 1) * c]
        xs.append((jnp.where(is_query, qq, kk) * jnp.exp2(expo)).astype(BF16))
    q_in = (qq * jnp.exp2(bc)).astype(BF16)
    k_up = (kk * jnp.exp2(total - bc)).astype(BF16)
    qb = qq.astype(BF16)
    kb = kk.astype(BF16)
    chunk_decay = jnp.exp2(total)
    outs = []
    for p in range(HG_HEADS // 2):
        kp = slice(2 * p * HG_K, (2 * p + 2) * HG_K)
        vp = slice(2 * p * HG_V, (2 * p + 2) * HG_V)
        sc = jnp.where(lvl_masks[HG_LEVELS], _dot(qb[:, kp], _pair_weights(kb, p)), 0.0)
        for l in range(HG_LEVELS):
            sc = jnp.where(lvl_masks[l], _dot(xs[l][:, kp], _pair_weights(xs[l], p)), sc)
        v_pair = vv[:, vp]
        v_diag = jnp.concatenate([v_pair * v_lo, v_pair * v_hi], axis=0)
        st_pair = state_ref[2 * p * HG_V:(2 * p + 2) * HG_V, kp]
        outs.append(_dot(sc.astype(BF16), v_diag) + _dot(q_in[:, kp], st_pair.astype(BF16).T))
        for h in (2 * p, 2 * p + 1):
            ks = slice(h * HG_K, (h + 1) * HG_K)
            vs = slice(h * HG_V, (h + 1) * HG_V)
            state_ref[vs, ks] = chunk_decay[:, ks] * state_ref[vs, ks] + _dot_tn(vv[:, vs], k_up[:, ks])
    return jnp.concatenate(outs, axis=1)


def _hgrn_kernel(x_ref, lb_ref, gain_ref, mf_ref, mb_ref, lvlf_ref, lvlb_ref, ones_ref, o_ref,
                 of_ref, ob_ref, sf_ref, sb_ref, *, n_lat_chunks):
    c = HG_CHUNK
    s_len = x_ref.shape[1]
    n_chunks = s_len // c
    n_ctx_chunks = n_chunks - n_lat_chunks
    sf_ref[...] = jnp.zeros_like(sf_ref)
    sb_ref[...] = jnp.zeros_like(sb_ref)
    lvl_f = lvlf_ref[...]
    lvl_b = lvlb_ref[...]
    masks_f = [lvl_f == l for l in range(HG_LEVELS + 1)]
    masks_b = [lvl_b == l for l in range(HG_LEVELS + 1)]
    lb_f = lb_ref[0:1, :]
    lb_b = lb_ref[1:2, :]
    v_lane = lax.broadcasted_iota(jnp.int32, (c, 2 * HG_V), 1)
    v_lo = (v_lane < HG_V).astype(F32).astype(BF16)
    v_hi = (v_lane >= HG_V).astype(F32).astype(BF16)

    def load(ci):
        r0 = pl.multiple_of(ci * c, c)
        hq = x_ref[0, pl.ds(r0, c), 0:HG_KW].astype(F32)
        vv = x_ref[0, pl.ds(r0, c), 3 * HG_KW:3 * HG_KW + HG_WIDTH]
        return r0, hq, vv

    def step(i, carry):
        cf = jnp.where(i < n_ctx_chunks, n_lat_chunks + i, i - n_ctx_chunks)
        r0, hq, vv = load(cf)
        zf = x_ref[0, pl.ds(r0, c), HG_KW:2 * HG_KW].astype(F32)
        of_ref[pl.ds(r0, c), :] = _hgrn_chunk(hq, zf, vv, lb_f, mf_ref, masks_f, v_lo, v_hi, sf_ref, True)
        cb = jnp.where(i < n_ctx_chunks, n_chunks - 1 - i, n_lat_chunks - 1 - (i - n_ctx_chunks))
        r0, hq, vv = load(cb)
        zb = x_ref[0, pl.ds(r0, c), 2 * HG_KW:3 * HG_KW].astype(F32)
        ob_ref[pl.ds(r0, c), :] = _hgrn_chunk(hq, zb, vv, lb_b, mb_ref, masks_b, v_lo, v_hi, sb_ref, False)
        return carry

    lax.fori_loop(0, n_chunks, step, 0, unroll=4)

    rt = 256
    gain = gain_ref[...]
    ones = ones_ref[...]

    def readout(i, carry):
        r0 = pl.multiple_of(i * rt, rt)
        o = of_ref[pl.ds(r0, rt), :] + ob_ref[pl.ds(r0, rt), :]
        sq_hi, sq_mid, sq_lo = _split3(o * o)
        msq = _dot(sq_hi, ones) + _dot(sq_mid, ones) + _dot(sq_lo, ones)
        zg = x_ref[0, pl.ds(r0, rt), 3 * HG_KW + HG_WIDTH:HG_IN_WIDTH].astype(F32)
        y = o * lax.rsqrt(msq + EPS) * gain * _silu(zg)
        o_ref[0, pl.ds(r0, rt), :] = y.astype(BF16)
        return carry

    lax.fori_loop(0, o_ref.shape[1] // rt, readout, 0)


def _hgrn(hg_in, lb, gain, n_lat, with_ctx_outputs):
    b, s, w = hg_in.shape
    mf, mb, lvl_f, lvl_b = _hgrn_constants()
    ones = np.kron(np.eye(HG_HEADS, dtype=np.float32), np.full((HG_V, HG_V), 1.0 / HG_V, np.float32))
    n_out = s if with_ctx_outputs else n_lat
    return pl.pallas_call(
        functools.partial(_hgrn_kernel, n_lat_chunks=n_lat // HG_CHUNK),
        grid=(b,),
        in_specs=[
            pl.BlockSpec((1, s, w), lambda i: (i, 0, 0)),
            _const_spec((2, HG_KW)),
            _const_spec((1, HG_WIDTH)),
            _const_spec(mf.shape),
            _const_spec(mb.shape),
            _const_spec(lvl_f.shape),
            _const_spec(lvl_b.shape),
            _const_spec(ones.shape),
        ],
        out_specs=pl.BlockSpec((1, n_out, HG_WIDTH), lambda i: (i, 0, 0)),
        out_shape=jax.ShapeDtypeStruct((b, n_out, HG_WIDTH), BF16),
        scratch_shapes=[
            pltpu.VMEM((s, HG_WIDTH), F32),
            pltpu.VMEM((s, HG_WIDTH), F32),
            pltpu.VMEM((HG_WIDTH, HG_KW), F32),
            pltpu.VMEM((HG_WIDTH, HG_KW), F32),
        ],
        compiler_params=_params(1),
        name="hgrn2",
    )(hg_in, lb, gain.reshape(1, HG_WIDTH), jnp.asarray(mf, BF16), jnp.asarray(mb, BF16),
      jnp.asarray(lvl_f), jnp.asarray(lvl_b), jnp.asarray(ones, BF16))


def _dft_tables(n):
    idx = np.arange(n, dtype=np.int64)
    ang = 2.0 * np.pi * ((idx[:, None] * idx[None, :]) % n).astype(np.float64) / n
    return np.cos(ang).astype(np.float32), np.sin(ang).astype(np.float32)


def _fourier_tile(z, table, mix, w, norm):
    tn = table.shape[0] // 2
    zz = _dot(table, z)
    zcat = jnp.concatenate([zz[0:tn], zz[tn:2 * tn]], axis=1).astype(BF16)
    mixed = _dot(zcat, mix) * norm
    return _dot(mixed.astype(BF16), w).astype(BF16)


def _fourier_kernel(z_ref, tl_ref, tc_ref, mix_ref, w_ref, o_ref, *maybe_oc_ref, n_lat):
    s_len = z_ref.shape[1]
    n_ctx = s_len - n_lat
    mix = mix_ref[...]
    w = w_ref[...]
    o_ref[0] = _fourier_tile(z_ref[0, 0:n_lat, :], tl_ref[0], mix, w, 1.0 / math.sqrt(n_lat * FN_GROUP_DIM))
    if maybe_oc_ref:
        @pl.when(pl.program_id(1) == pl.num_programs(1) - 1)
        def _():
            maybe_oc_ref[0][0] = _fourier_tile(z_ref[0, n_lat:s_len, :], tc_ref[...], mix, w,
                                               1.0 / math.sqrt(n_ctx * FN_GROUP_DIM))


def _stacked_dft_table(n, tile):
    cos, sin = _dft_tables(n)
    return np.concatenate([cos.reshape(n // tile, tile, n), sin.reshape(n // tile, tile, n)], axis=1)


def _fourier(fn_in, w_fourier, n_lat, with_ctx_outputs):
    b, s, w = fn_in.shape
    n_ctx = s - n_lat
    tn = FOURIER_TILE
    table_lat = _stacked_dft_table(n_lat, tn)
    table_ctx = _stacked_dft_table(n_ctx, n_ctx)[0]
    cgrp, sgrp = _dft_tables(FN_GROUP_DIM)
    eye = np.eye(FN_GROUPS, dtype=np.float32)
    mix = np.concatenate([np.kron(eye, cgrp), -np.kron(eye, sgrp)], axis=0)
    out_specs = [pl.BlockSpec((1, tn, FN_WIDTH), lambda i, j: (i, j, 0))]
    out_shape = [jax.ShapeDtypeStruct((b, n_lat, FN_WIDTH), BF16)]
    if with_ctx_outputs:
        out_specs.append(pl.BlockSpec((1, n_ctx, FN_WIDTH), lambda i, j: (i, 0, 0)))
        out_shape.append(jax.ShapeDtypeStruct((b, n_ctx, FN_WIDTH), BF16))
    outs = pl.pallas_call(
        functools.partial(_fourier_kernel, n_lat=n_lat),
        grid=(b, n_lat // tn),
        in_specs=[
            pl.BlockSpec((1, s, w), lambda i, j: (i, 0, 0)),
            pl.BlockSpec((1, 2 * tn, n_lat), lambda i, j: (j, 0, 0)),
            _const_spec(table_ctx.shape),
            _const_spec(mix.shape),
            _const_spec(w_fourier.shape),
        ],
        out_specs=out_specs,
        out_shape=out_shape,
        compiler_params=_params(2),
        name="fourier",
    )(fn_in, jnp.asarray(table_lat, BF16), jnp.asarray(table_ctx, BF16), jnp.asarray(mix, BF16), w_fourier)
    return (outs[0], outs[1]) if with_ctx_outputs else (outs[0], None)


def _ffn_kernel(*refs, final_norm, n_lat, select_ctx, n_srcs):
    streams = []
    pos = 0
    for n in n_srcs:
        streams.append(_stream_tile(refs[pos:pos + n]))
        pos += n
    x, attn, hg, fn = streams
    modb_ref, modc_ref, g2_ref, wo_ref, wgu_ref, wd_ref, gf_ref, o_ref, act_ref = refs[pos:]
    tm = x.shape[0]

    def mod_vec(r):
        if select_ctx:
            return _row_select(tm, n_lat, modc_ref[0, r:r + 1, :], modb_ref[0, r:r + 1, :])
        return modb_ref[0, r:r + 1, :]

    y = (_dot(attn, wo_ref[0:MLA_WIDTH, :])
         + _dot(hg, wo_ref[MLA_WIDTH:MLA_WIDTH + HG_WIDTH, :])
         + _dot(fn, wo_ref[MLA_WIDTH + HG_WIDTH:MLA_WIDTH + HG_WIDTH + FN_WIDTH, :]))
    x1 = x + mod_vec(2) * y
    o_ref[0] = x1
    xn = x1 * lax.rsqrt(jnp.mean(x1 * x1, axis=-1, keepdims=True) + EPS) * g2_ref[...]
    hm = (xn * (1.0 + mod_vec(4)) + mod_vec(3)).astype(BF16)

    ft = wgu_ref.shape[2] // 2
    for t in range(wgu_ref.shape[0]):
        gu = _dot(hm, wgu_ref[t])
        act_ref[:, t * ft:(t + 1) * ft] = (_silu(gu[:, 0:ft]) * gu[:, ft:2 * ft]).astype(BF16)
    out = o_ref[0] + mod_vec(5) * _dot(act_ref[...], wd_ref[...])
    if final_norm:
        out = out * lax.rsqrt(jnp.mean(out * out, axis=-1, keepdims=True) + EPS) * gf_ref[...]
    o_ref[0] = out


def _outproj_ffn(x_src, attn, hg, fn, mod, g2, w_out, w_gu, w_d, g_final, n_lat, n_ctx, ctx_row, with_ctx_rows,
                 final_norm):
    d = g2.shape[0]
    b = (x_src[0] if isinstance(x_src, tuple) else x_src).shape[0]
    tm = STREAM_TILE if with_ctx_rows else LATENT_TILE
    n_rows = n_lat + n_ctx if with_ctx_rows else n_lat
    hidden = w_d.shape[0]
    row = lambda w: pl.BlockSpec((1, tm, w), lambda i, j: (i, j, 0))
    src_specs, src_args, n_srcs = [], [], []
    for src in (x_src, attn, hg, fn):
        specs, args = _stream_specs(src, tm)
        src_specs += specs
        src_args += args
        n_srcs.append(len(args))
    return pl.pallas_call(
        functools.partial(_ffn_kernel, final_norm=final_norm, n_lat=n_lat, select_ctx=with_ctx_rows,
                          n_srcs=tuple(n_srcs)),
        grid=(b, n_rows // tm),
        in_specs=src_specs + [
            pl.BlockSpec((1, 6, d), lambda i, j: (i, 0, 0)),
            pl.BlockSpec((1, 6, d), lambda i, j: (ctx_row, 0, 0)),
            _const_spec((1, d)),
            _RESIDENT,
            _RESIDENT,
            _RESIDENT,
            _const_spec((1, d)),
        ],
        out_specs=row(d),
        out_shape=jax.ShapeDtypeStruct((b, n_rows, d), F32),
        scratch_shapes=[pltpu.VMEM((tm, hidden), BF16)],
        compiler_params=_params(2),
        name="outproj_ffn",
    )(*src_args, mod, mod, g2.reshape(1, d), w_out, w_gu, w_d, g_final.reshape(1, d))


def _rotate_half_cols(w):
    r1, r2, c1, c2 = jnp.split(w, 4, axis=-1)
    return jnp.concatenate([-r2, r1, -c2, c1], axis=-1)


def _pack_w_in(w):
    d = w.shape[0]
    zeros = jnp.zeros((d, 128 - MLA_ROPE), w.dtype)
    w_kr = w[:, O_KR:O_HQ]
    return jnp.concatenate([w[:, O_CQ:O_KR], w_kr, zeros, _rotate_half_cols(w_kr), zeros, w[:, O_HQ:]],
                           axis=1).astype(BF16)


def _pack_w_uq(w):
    r = w.shape[0]
    wh = w.reshape(r, MLA_HEADS, MLA_NOPE + MLA_ROPE)
    nope = wh[:, :, :MLA_NOPE].reshape(r, MLA_HEADS * MLA_NOPE)
    pe = wh[:, :, MLA_NOPE:]
    zeros = jnp.zeros((r, MLA_HEADS, 128 - MLA_ROPE), w.dtype)
    pe_a = jnp.concatenate([pe, zeros], axis=-1).reshape(r, MLA_HEADS * 128)
    pe_b = jnp.concatenate([_rotate_half_cols(pe), zeros], axis=-1).reshape(r, MLA_HEADS * 128)
    return jnp.concatenate([nope, pe_a, pe_b], axis=1).astype(BF16)


def _pack_w_ukv(w):
    r = w.shape[0]
    wh = w.reshape(r, MLA_HEADS, MLA_NOPE + MLA_V)
    k_nope = wh[:, :, :MLA_NOPE].reshape(r, MLA_HEADS * MLA_NOPE)
    v = wh[:, :, MLA_NOPE:].reshape(r, MLA_HEADS * MLA_V)
    return jnp.concatenate([k_nope, v], axis=1).astype(BF16)


def _pack_w_gate_up(w, hidden):
    d = w.shape[0]
    n_ft = hidden // FFN_TILE
    gate = w[:, :hidden].reshape(d, n_ft, FFN_TILE)
    up = w[:, hidden:].reshape(d, n_ft, FFN_TILE)
    return jnp.concatenate([gate, up], axis=2).transpose(1, 0, 2).astype(BF16)


def _rope_tables(n_lat, n_ctx):
    rows = n_lat // GRID_W
    row_pos = np.repeat(np.arange(rows, dtype=np.float32), GRID_W)
    col_pos = np.tile(np.arange(GRID_W, dtype=np.float32), rows)
    axis_dim = MLA_ROPE // 2
    inv_freq = (ROPE_BASE ** (-np.arange(0, axis_dim, 2, dtype=np.float32) / axis_dim)).astype(np.float32)
    ang_r = row_pos[:, None] * inv_freq
    ang_c = col_pos[:, None] * inv_freq
    ang = np.concatenate([ang_r, ang_r, ang_c, ang_c], axis=-1)
    cos = np.ones((n_lat + n_ctx, 128), np.float32)
    sin = np.zeros((n_lat + n_ctx, 128), np.float32)
    cos[:n_lat, :MLA_ROPE] = np.cos(ang)
    sin[:n_lat, :MLA_ROPE] = np.sin(ang)
    return jnp.asarray(cos), jnp.asarray(sin)


def kernel(x, c, ctx, c_ctx, w_mod, b_mod, norm1_g, norm2_g, w_in, q_norm_g, w_uq, kv_norm_g, w_ukv, lb_param,
           hg_norm_g, w_fourier, w_out, w_gate_up, w_down, final_norm_g):
    b, t, d = x.shape
    n_ctx = ctx.shape[1]
    s = t + n_ctx
    depth = w_mod.shape[0]
    hidden = w_down.shape[1]
    assert s % STREAM_TILE == 0 and t % LATENT_TILE == 0 and t % ATTN_TILE == 0 and t % FOURIER_TILE == 0
    assert hidden % FFN_TILE == 0 and t % GRID_W == 0 and n_ctx % HG_CHUNK == 0

    mod_rows = -(-(b + 1) // 8) * 8
    c_all = jnp.zeros((mod_rows, d), F32).at[:b].set(c).at[b].set(c_ctx)
    mod_all = _modulation(c_all, w_mod, b_mod).reshape(depth, mod_rows, 6, d)
    lower = _lower_bounds(lb_param)
    cos_t, sin_t = _rope_tables(t, n_ctx)

    xa = (x, ctx)
    for l in range(depth):
        last = l == depth - 1
        q, k, v, hg_in, fn_in = _inproj(xa, mod_all[l], norm1_g[l], _pack_w_in(w_in[l]), q_norm_g[l],
                                        _pack_w_uq(w_uq[l]), kv_norm_g[l], _pack_w_ukv(w_ukv[l]), cos_t, sin_t, t, b)
        attn, attn_ctx = _attention(q, k, v, t, not last)
        hg = _hgrn(hg_in, lower[l], hg_norm_g[l], t, not last)
        fn, fn_ctx = _fourier(fn_in, w_fourier[l].astype(BF16), t, not last)
        if not last:
            attn, fn = (attn, attn_ctx), (fn, fn_ctx)
        xa = _outproj_ffn(xa, attn, hg, fn, mod_all[l], norm2_g[l], w_out[l].astype(BF16),
                          _pack_w_gate_up(w_gate_up[l], hidden), w_down[l].astype(BF16), final_norm_g, t, n_ctx, b,
                          not last, last)
    return xa
```

```python
import functools
import math

import numpy as np
import jax
import jax.numpy as jnp
from jax import lax
from jax.experimental import pallas as pl
from jax.experimental.pallas import tpu as pltpu

EPS = 1e-6
GRID_W = 64
ROPE_BASE = 10000.0

MLA_HEADS = 4
MLA_Q_LORA = 256
MLA_KV_LORA = 128
MLA_NOPE = 128
MLA_ROPE = 64
MLA_V = 128
MLA_WIDTH = MLA_HEADS * MLA_V
HEAD_PAD = 256

HG_HEADS = 4
HG_K = 128
HG_V = 64
HG_WIDTH = HG_HEADS * HG_V
HG_KW = HG_HEADS * HG_K
HG_CHUNK = 64
HG_LEVELS = 6
HG_MATMUL_LEVELS = 3

FN_GROUPS = 4
FN_GROUP_DIM = 64
FN_WIDTH = FN_GROUPS * FN_GROUP_DIM

O_CQ = 0
O_CKV = O_CQ + MLA_Q_LORA
O_KR = O_CKV + MLA_KV_LORA
O_HQ = O_KR + MLA_ROPE

P_CQ = 0
P_CKV = P_CQ + MLA_Q_LORA
P_KPA = P_CKV + MLA_KV_LORA
P_KPB = P_KPA + 128
P_HG = P_KPB + 128
HG_IN_WIDTH = 3 * HG_KW + 2 * HG_WIDTH
P_FN = P_HG + HG_IN_WIDTH
P_WIDTH = P_FN + FN_WIDTH

STREAM_TILE = 768
LATENT_TILE = 1024
ATTN_TILE = 512
FOURIER_TILE = 512
FFN_TILE = 256
VMEM_LIMIT = 56 * 1024 * 1024

BF16 = jnp.bfloat16
F32 = jnp.float32


def _dot(a, b):
    return jnp.dot(a, b, preferred_element_type=F32)


def _dot_nt(a, b):
    return lax.dot_general(a, b, (((1,), (1,)), ((), ())), preferred_element_type=F32)


def _dot_tn(a, b):
    return lax.dot_general(a, b, (((0,), (0,)), ((), ())), preferred_element_type=F32)


def _silu(x):
    return x * (1.0 / (1.0 + jnp.exp(-x)))


def _sigmoid(x):
    return 1.0 / (1.0 + jnp.exp(-x))


def _split3(x):
    hi = x.astype(BF16)
    r1 = x - hi.astype(F32)
    mid = r1.astype(BF16)
    lo = (r1 - mid.astype(F32)).astype(BF16)
    return hi, mid, lo


def _const_spec(shape):
    nd = len(shape)
    return pl.BlockSpec(shape, lambda *_: (0,) * nd)


_RESIDENT = pl.BlockSpec(memory_space=pltpu.VMEM)


def _params(n_axes, flags=None):
    return pltpu.CompilerParams(dimension_semantics=("arbitrary",) * n_axes, vmem_limit_bytes=VMEM_LIMIT,
                                flags=flags)


def _mod_kernel(c_ref, w_ref, b_ref, o_ref):
    act = _silu(c_ref[...]).astype(BF16)
    o_ref[0] = _dot(act, w_ref[0].astype(BF16)) + b_ref[0]


def _modulation(c_all, w_mod, b_mod):
    depth, d, n6 = w_mod.shape
    rows = c_all.shape[0]
    tn = n6 // 6
    return pl.pallas_call(
        _mod_kernel,
        grid=(depth, n6 // tn),
        in_specs=[
            pl.BlockSpec((rows, d), lambda l, j: (0, 0)),
            pl.BlockSpec((1, d, tn), lambda l, j: (l, 0, j)),
            pl.BlockSpec((1, 1, tn), lambda l, j: (l, 0, j)),
        ],
        out_specs=pl.BlockSpec((1, rows, tn), lambda l, j: (l, 0, j)),
        out_shape=jax.ShapeDtypeStruct((depth, rows, n6), F32),
        name="modulation",
    )(c_all, w_mod, b_mod.reshape(depth, 1, n6))


def _lower_bound_kernel(lb_ref, o_ref):
    depth = lb_ref.shape[0]
    lp = [lb_ref[l] for l in range(depth)]
    m = lp[0]
    for l in range(1, depth):
        m = jnp.maximum(m, lp[l])
    e = [jnp.exp(v - m) for v in lp]
    tot = e[0]
    for l in range(1, depth):
        tot = tot + e[l]
    probs = [v / tot for v in e]
    cum = probs[0]
    o_ref[0] = cum - probs[0]
    for l in range(1, depth):
        cum = cum + probs[l]
        o_ref[l] = cum - probs[0]


def _lower_bounds(lb_param):
    return pl.pallas_call(
        _lower_bound_kernel,
        out_shape=jax.ShapeDtypeStruct(lb_param.shape, F32),
        name="hgrn_lower_bounds",
    )(lb_param.astype(F32))


def _row_select(tile_rows, n_lat, ctx_vec, lat_vec):
    row = pl.program_id(1) * tile_rows + lax.broadcasted_iota(jnp.int32, (tile_rows, 1), 0)
    return jnp.where(row >= n_lat, ctx_vec, lat_vec)


def _stream_specs(src, tm):
    if not isinstance(src, tuple):
        return [pl.BlockSpec((1, tm, src.shape[2]), lambda i, j: (i, j, 0))], [src]
    lat, ctx = src
    t, d = lat.shape[1:]
    n_ctx = ctx.shape[1]
    n_full = t // tm
    tail = t - n_full * tm
    assert n_full >= 1 and tail > 0 and tail + n_ctx == tm and (n_full * tm) % tail == 0
    specs = [pl.BlockSpec((1, tm, d), lambda i, j: (i, jnp.minimum(j, n_full - 1), 0)),
             pl.BlockSpec((1, tail, d), lambda i, j: (i, (n_full * tm) // tail, 0)),
             pl.BlockSpec((1, n_ctx, d), lambda i, j: (i, 0, 0))]
    return specs, [lat, lat, ctx]


def _stream_tile(refs):
    if len(refs) == 1:
        return refs[0][0]
    main_ref, tail_ref, ctx_ref = refs
    mixed = jnp.concatenate([tail_ref[0], ctx_ref[0]], axis=0)
    return jnp.where(pl.program_id(1) < pl.num_programs(1) - 1, main_ref[0], mixed)


def _inproj_kernel(*refs, q_scale, n_lat, n_src):
    (modb_ref, modc_ref, g1_ref, w_in_ref, qg_ref, wq_ref, kvg_ref, wk_ref, wvt_ref, cos_ref, sin_ref,
     q_ref, k_ref, vt_ref, hg_ref, fn_ref) = refs[n_src:]
    x = _stream_tile(refs[:n_src])
    tm = x.shape[0]
    shift = _row_select(tm, n_lat, modc_ref[0, 0:1, :], modb_ref[0, 0:1, :])
    scale = _row_select(tm, n_lat, modc_ref[0, 1:2, :], modb_ref[0, 1:2, :])
    xn = x * lax.rsqrt(jnp.mean(x * x, axis=-1, keepdims=True) + EPS) * g1_ref[...]
    xm = (xn * (1.0 + scale) + shift).astype(BF16)

    hg_ref[0] = _dot(xm, w_in_ref[:, P_HG:P_FN]).astype(BF16)
    fn_ref[0] = _dot(xm, w_in_ref[:, P_FN:P_WIDTH]).astype(BF16)
    p = _dot(xm, w_in_ref[:, 0:P_HG])

    cos = cos_ref[...]
    sin = sin_ref[...]

    cq = p[:, P_CQ:P_CKV]
    cqn = (cq * lax.rsqrt(jnp.mean(cq * cq, axis=-1, keepdims=True) + EPS) * qg_ref[...]).astype(BF16)
    qq = _dot(cqn, wq_ref[...])
    nw = MLA_HEADS * MLA_NOPE
    for h in range(MLA_HEADS):
        q_ref[0, :, h * HEAD_PAD:h * HEAD_PAD + 128] = (qq[:, h * 128:(h + 1) * 128] * q_scale).astype(BF16)
        pe = qq[:, nw + h * 128:nw + (h + 1) * 128] * cos + qq[:, 2 * nw + h * 128:2 * nw + (h + 1) * 128] * sin
        q_ref[0, :, h * HEAD_PAD + 128:(h + 1) * HEAD_PAD] = (pe * q_scale).astype(BF16)

    ckv = p[:, P_CKV:P_KPA]
    ckvn = (ckv * lax.rsqrt(jnp.mean(ckv * ckv, axis=-1, keepdims=True) + EPS) * kvg_ref[...]).astype(BF16)
    kv = _dot(ckvn, wk_ref[...])
    vt_ref[0] = _dot_nt(wvt_ref[...], ckvn).astype(BF16)
    kpe = (p[:, P_KPA:P_KPB] * cos + p[:, P_KPB:P_HG] * sin).astype(BF16)
    for h in range(MLA_HEADS):
        k_ref[0, :, h * HEAD_PAD:h * HEAD_PAD + 128] = kv[:, h * 128:(h + 1) * 128].astype(BF16)
        k_ref[0, :, h * HEAD_PAD + 128:(h + 1) * HEAD_PAD] = kpe


def _inproj(src, mod, g1, w_in_p, qg, wq_all, kvg, wk_p, wvt_p, cos_t, sin_t, n_lat, ctx_row):
    s = cos_t.shape[0]
    d = g1.shape[0]
    b = (src[0] if isinstance(src, tuple) else src).shape[0]
    tm = STREAM_TILE
    row = lambda w: pl.BlockSpec((1, tm, w), lambda i, j: (i, j, 0))
    src_specs, src_args = _stream_specs(src, tm)
    return pl.pallas_call(
        functools.partial(_inproj_kernel, q_scale=math.log2(math.e) / math.sqrt(MLA_NOPE + MLA_ROPE), n_lat=n_lat,
                          n_src=len(src_args)),
        grid=(b, s // tm),
        in_specs=src_specs + [
            pl.BlockSpec((1, 6, d), lambda i, j: (i, 0, 0)),
            pl.BlockSpec((1, 6, d), lambda i, j: (ctx_row, 0, 0)),
            _const_spec((1, d)),
            _RESIDENT,
            _const_spec((1, MLA_Q_LORA)),
            _RESIDENT,
            _const_spec((1, MLA_KV_LORA)),
            _RESIDENT,
            _RESIDENT,
            pl.BlockSpec((tm, 128), lambda i, j: (j, 0)),
            pl.BlockSpec((tm, 128), lambda i, j: (j, 0)),
        ],
        out_specs=[row(MLA_HEADS * HEAD_PAD), row(MLA_HEADS * HEAD_PAD),
                   pl.BlockSpec((1, MLA_WIDTH, tm), lambda i, j: (i, 0, j)), row(HG_IN_WIDTH), row(FN_WIDTH)],
        out_shape=[
            jax.ShapeDtypeStruct((b, s, MLA_HEADS * HEAD_PAD), BF16),
            jax.ShapeDtypeStruct((b, s, MLA_HEADS * HEAD_PAD), BF16),
            jax.ShapeDtypeStruct((b, MLA_WIDTH, s), BF16),
            jax.ShapeDtypeStruct((b, s, HG_IN_WIDTH), BF16),
            jax.ShapeDtypeStruct((b, s, FN_WIDTH), BF16),
        ],
        compiler_params=_params(2),
        name="inproj",
    )(*src_args, mod, mod, g1.reshape(1, d), w_in_p, qg.reshape(1, -1), wq_all, kvg.reshape(1, -1), wk_p, wvt_p,
      cos_t, sin_t)


def _attend(q_ref, k_ref, vt_ref, o_ref, key_lo, key_hi):
    for h in range(MLA_HEADS):
        qh = q_ref[0, :, h * HEAD_PAD:(h + 1) * HEAD_PAD]
        kh = k_ref[0, key_lo:key_hi, h * HEAD_PAD:(h + 1) * HEAD_PAD]
        vth = vt_ref[0, h * MLA_V:(h + 1) * MLA_V, key_lo:key_hi]
        st = _dot_nt(kh, qh)
        e = jnp.exp2(st - jnp.max(st, axis=0, keepdims=True))
        denom = jnp.sum(e, axis=0, keepdims=True)
        ot = _dot(vth, e.astype(BF16)) * (1.0 / denom)
        o_ref[0, :, h * MLA_V:(h + 1) * MLA_V] = ot.T.astype(BF16)


def _attn_kernel(q_ref, k_ref, vt_ref, o_ref):
    _attend(q_ref, k_ref, vt_ref, o_ref, 0, k_ref.shape[1])


def _attn_ctx_kernel(q_ref, qc_ref, k_ref, vt_ref, o_ref, oc_ref, *, n_lat):
    s_len = k_ref.shape[1]
    _attend(q_ref, k_ref, vt_ref, o_ref, 0, s_len)

    @pl.when(pl.program_id(1) == pl.num_programs(1) - 1)
    def _():
        _attend(qc_ref, k_ref, vt_ref, oc_ref, n_lat, s_len)


def _attention(q, k, vt, n_lat, with_ctx_queries):
    b, s, _ = q.shape
    tq = ATTN_TILE
    n_ctx = s - n_lat
    v = vt
    q_spec = pl.BlockSpec((1, tq, q.shape[2]), lambda i, j: (i, j, 0))
    kv_specs = [pl.BlockSpec((1, s, k.shape[2]), lambda i, j: (i, 0, 0)),
                pl.BlockSpec((1, vt.shape[1], s), lambda i, j: (i, 0, 0))]
    o_spec = pl.BlockSpec((1, tq, MLA_WIDTH), lambda i, j: (i, j, 0))
    o_shape = jax.ShapeDtypeStruct((b, n_lat, MLA_WIDTH), BF16)
    if not with_ctx_queries:
        out = pl.pallas_call(
            _attn_kernel,
            grid=(b, n_lat // tq),
            in_specs=[q_spec] + kv_specs,
            out_specs=o_spec,
            out_shape=o_shape,
            compiler_params=_params(2),
            name="attention",
        )(q, k, v)
        return out, None
    assert n_lat % n_ctx == 0
    return pl.pallas_call(
        functools.partial(_attn_ctx_kernel, n_lat=n_lat),
        grid=(b, n_lat // tq),
        in_specs=[q_spec, pl.BlockSpec((1, n_ctx, q.shape[2]), lambda i, j: (i, n_lat // n_ctx, 0))] + kv_specs,
        out_specs=[o_spec, pl.BlockSpec((1, n_ctx, MLA_WIDTH), lambda i, j: (i, 0, 0))],
        out_shape=[o_shape, jax.ShapeDtypeStruct((b, n_ctx, MLA_WIDTH), BF16)],
        compiler_params=_params(2),
        name="attention_ctx",
    )(q, q, k, v)


def _hgrn_constants():
    c = HG_CHUNK
    t = np.arange(c)[:, None]
    u = np.arange(c)[None, :]
    blocks = [(u <= t)]
    for m in (8, 4, 2):
        half = m // 2
        mid = (t // m) * m + half
        lower = (t % m) >= half
        blocks.append(np.where(lower, (u >= mid) & (u <= t), (u >= t + 1) & (u <= mid - 1)))
    mf = np.stack([blk.astype(np.float32) for blk in blocks])
    mb = np.ascontiguousarray(mf[:, ::-1, ::-1])
    msb = np.floor(np.log2(np.maximum(t ^ u, 1))).astype(np.int32)
    lvl = np.where(t > u, HG_LEVELS - 1 - msb, np.where(t == u, HG_LEVELS, -1)).astype(np.int32)
    lvl_f = np.tile(lvl, (1, 2))
    lvl_b = np.tile(np.ascontiguousarray(lvl.T), (1, 2))
    return mf.reshape(-1, c), mb.reshape(-1, c), lvl_f, lvl_b


def _pair_weights(x, p):
    zeros = jnp.zeros((x.shape[0], HG_K), x.dtype)
    xa = x[:, 2 * p * HG_K:(2 * p + 1) * HG_K]
    xb = x[:, (2 * p + 1) * HG_K:(2 * p + 2) * HG_K]
    blk = jnp.concatenate([jnp.concatenate([xa, zeros], axis=1), jnp.concatenate([zeros, xb], axis=1)], axis=0)
    return blk.T


def _hgrn_chunk(hq, z, vv, lb, m_ref, lvl_masks, v_lo, v_hi, state_ref, fwd):
    c = HG_CHUNK
    f = lb + (1.0 - lb) * _sigmoid(z)
    g = jnp.log2(f)
    kk = 1.0 - f
    qq = _silu(hq)
    g_hi = g.astype(BF16)
    g_lo = (g - g_hi.astype(F32)).astype(BF16)
    m_all = m_ref[...]
    eb = _dot(m_all, g_hi) + _dot(m_all, g_lo)
    bc = eb[0:c]
    total = bc[c - 1:c] if fwd else bc[0:1]
    row = lax.broadcasted_iota(jnp.int32, (c, HG_KW), 0)
    xs = []
    for l in range(HG_LEVELS):
        m = c >> l
        half = m // 2
        bit = (row & half) != 0
        is_query = bit if fwd else jnp.logical_not(bit)
        if l < HG_LEVELS - HG_MATMUL_LEVELS:
            refs = [bc[b0 + half - 1:b0 + half] if fwd else bc[b0 + half:b0 + half + 1] for b0 in range(0, c, m)]
            ref = jnp.concatenate([jnp.broadcast_to(r, (m, HG_KW)) for r in refs], axis=0)
            diff = bc - ref
            expo = jnp.where(is_query, diff, -diff)
        else:
            idx = 1 + l - (HG_LEVELS - HG_MATMUL_LEVELS)
            expo = eb[idx * c:(idx + 1) * c]
        xs.append((jnp.where(is_query, qq, kk) * jnp.exp2(expo)).astype(BF16))
    q_in = (qq * jnp.exp2(bc)).astype(BF16)
    k_up = (kk * jnp.exp2(total - bc)).astype(BF16)
    qb = qq.astype(BF16)
    kb = kk.astype(BF16)
    chunk_decay = jnp.exp2(total)
    outs = []
    for p in range(HG_HEADS // 2):
        kp = slice(2 * p * HG_K, (2 * p + 2) * HG_K)
        vp = slice(2 * p * HG_V, (2 * p + 2) * HG_V)
        sc = jnp.where(lvl_masks[HG_LEVELS], _dot(qb[:, kp], _pair_weights(kb, p)), 0.0)
        for l in range(HG_LEVELS):
            sc = jnp.where(lvl_masks[l], _dot(xs[l][:, kp], _pair_weights(xs[l], p)), sc)
        v_pair = vv[:, vp]
        v_diag = jnp.concatenate([v_pair * v_lo, v_pair * v_hi], axis=0)
        st_pair = state_ref[2 * p * HG_V:(2 * p + 2) * HG_V, kp]
        outs.append(_dot(sc.astype(BF16), v_diag) + _dot(q_in[:, kp], st_pair.astype(BF16).T))
        for h in (2 * p, 2 * p + 1):
            ks = slice(h * HG_K, (h + 1) * HG_K)
            vs = slice(h * HG_V, (h + 1) * HG_V)
            state_ref[vs, ks] = chunk_decay[:, ks] * state_ref[vs, ks] + _dot_tn(vv[:, vs], k_up[:, ks])
    return jnp.concatenate(outs, axis=1)


def _hgrn_kernel(x_ref, lb_ref, gain_ref, mf_ref, mb_ref, lvlf_ref, lvlb_ref, ones_ref, o_ref,
                 of_ref, ob_ref, sf_ref, sb_ref, *, n_lat_chunks):
    c = HG_CHUNK
    s_len = x_ref.shape[1]
    n_chunks = s_len // c
    n_ctx_chunks = n_chunks - n_lat_chunks
    sf_ref[...] = jnp.zeros_like(sf_ref)
    sb_ref[...] = jnp.zeros_like(sb_ref)
    lvl_f = lvlf_ref[...]
    lvl_b = lvlb_ref[...]
    masks_f = [lvl_f == l for l in range(HG_LEVELS + 1)]
    masks_b = [lvl_b == l for l in range(HG_LEVELS + 1)]
    lb_f = lb_ref[0:1, :]
    lb_b = lb_ref[1:2, :]
    v_lane = lax.broadcasted_iota(jnp.int32, (c, 2 * HG_V), 1)
    v_lo = (v_lane < HG_V).astype(F32).astype(BF16)
    v_hi = (v_lane >= HG_V).astype(F32).astype(BF16)

    def load(ci):
        r0 = pl.multiple_of(ci * c, c)
        hq = x_ref[0, pl.ds(r0, c), 0:HG_KW].astype(F32)
        vv = x_ref[0, pl.ds(r0, c), 3 * HG_KW:3 * HG_KW + HG_WIDTH]
        return r0, hq, vv

    def step(i, carry):
        cf = jnp.where(i < n_ctx_chunks, n_lat_chunks + i, i - n_ctx_chunks)
        r0, hq, vv = load(cf)
        zf = x_ref[0, pl.ds(r0, c), HG_KW:2 * HG_KW].astype(F32)
        of_ref[pl.ds(r0, c), :] = _hgrn_chunk(hq, zf, vv, lb_f, mf_ref, masks_f, v_lo, v_hi, sf_ref, True)
        cb = jnp.where(i < n_ctx_chunks, n_chunks - 1 - i, n_lat_chunks - 1 - (i - n_ctx_chunks))
        r0, hq, vv = load(cb)
        zb = x_ref[0, pl.ds(r0, c), 2 * HG_KW:3 * HG_KW].astype(F32)
        ob_ref[pl.ds(r0, c), :] = _hgrn_chunk(hq, zb, vv, lb_b, mb_ref, masks_b, v_lo, v_hi, sb_ref, False)
        return carry

    lax.fori_loop(0, n_chunks, step, 0, unroll=4)

    rt = 256
    gain = gain_ref[...]
    ones = ones_ref[...]

    def readout(i, carry):
        r0 = pl.multiple_of(i * rt, rt)
        o = of_ref[pl.ds(r0, rt), :] + ob_ref[pl.ds(r0, rt), :]
        sq_hi, sq_mid, sq_lo = _split3(o * o)
        msq = _dot(sq_hi, ones) + _dot(sq_mid, ones) + _dot(sq_lo, ones)
        zg = x_ref[0, pl.ds(r0, rt), 3 * HG_KW + HG_WIDTH:HG_IN_WIDTH].astype(F32)
        y = o * lax.rsqrt(msq + EPS) * gain * _silu(zg)
        o_ref[0, pl.ds(r0, rt), :] = y.astype(BF16)
        return carry

    lax.fori_loop(0, o_ref.shape[1] // rt, readout, 0)


def _hgrn(hg_in, lb, gain, n_lat, with_ctx_outputs):
    b, s, w = hg_in.shape
    mf, mb, lvl_f, lvl_b = _hgrn_constants()
    ones = np.kron(np.eye(HG_HEADS, dtype=np.float32), np.full((HG_V, HG_V), 1.0 / HG_V, np.float32))
    n_out = s if with_ctx_outputs else n_lat
    return pl.pallas_call(
        functools.partial(_hgrn_kernel, n_lat_chunks=n_lat // HG_CHUNK),
        grid=(b,),
        in_specs=[
            pl.BlockSpec((1, s, w), lambda i: (i, 0, 0)),
            _const_spec((2, HG_KW)),
            _const_spec((1, HG_WIDTH)),
            _const_spec(mf.shape),
            _const_spec(mb.shape),
            _const_spec(lvl_f.shape),
            _const_spec(lvl_b.shape),
            _const_spec(ones.shape),
        ],
        out_specs=pl.BlockSpec((1, n_out, HG_WIDTH), lambda i: (i, 0, 0)),
        out_shape=jax.ShapeDtypeStruct((b, n_out, HG_WIDTH), BF16),
        scratch_shapes=[
            pltpu.VMEM((s, HG_WIDTH), F32),
            pltpu.VMEM((s, HG_WIDTH), F32),
            pltpu.VMEM((HG_WIDTH, HG_KW), F32),
            pltpu.VMEM((HG_WIDTH, HG_KW), F32),
        ],
        compiler_params=_params(1),
        name="hgrn2",
    )(hg_in, lb, gain.reshape(1, HG_WIDTH), jnp.asarray(mf, BF16), jnp.asarray(mb, BF16),
      jnp.asarray(lvl_f), jnp.asarray(lvl_b), jnp.asarray(ones, BF16))


def _dft_tables(n):
    idx = np.arange(n, dtype=np.int64)
    ang = 2.0 * np.pi * ((idx[:, None] * idx[None, :]) % n).astype(np.float64) / n
    return np.cos(ang).astype(np.float32), np.sin(ang).astype(np.float32)


def _fourier_tile(z, table, mix, w, norm):
    tn = table.shape[0] // 2
    zz = _dot(table, z)
    zcat = jnp.concatenate([zz[0:tn], zz[tn:2 * tn]], axis=1).astype(BF16)
    mixed = _dot(zcat, mix) * norm
    return _dot(mixed.astype(BF16), w).astype(BF16)


def _fourier_kernel(z_ref, tl_ref, tc_ref, mix_ref, w_ref, o_ref, *maybe_oc_ref, n_lat):
    s_len = z_ref.shape[1]
    n_ctx = s_len - n_lat
    mix = mix_ref[...]
    w = w_ref[...]
    o_ref[0] = _fourier_tile(z_ref[0, 0:n_lat, :], tl_ref[0], mix, w, 1.0 / math.sqrt(n_lat * FN_GROUP_DIM))
    if maybe_oc_ref:
        @pl.when(pl.program_id(1) == pl.num_programs(1) - 1)
        def _():
            maybe_oc_ref[0][0] = _fourier_tile(z_ref[0, n_lat:s_len, :], tc_ref[...], mix, w,
                                               1.0 / math.sqrt(n_ctx * FN_GROUP_DIM))


def _stacked_dft_table(n, tile):
    cos, sin = _dft_tables(n)
    return np.concatenate([cos.reshape(n // tile, tile, n), sin.reshape(n // tile, tile, n)], axis=1)


def _fourier(fn_in, w_fourier, n_lat, with_ctx_outputs):
    b, s, w = fn_in.shape
    n_ctx = s - n_lat
    tn = FOURIER_TILE
    table_lat = _stacked_dft_table(n_lat, tn)
    table_ctx = _stacked_dft_table(n_ctx, n_ctx)[0]
    cgrp, sgrp = _dft_tables(FN_GROUP_DIM)
    eye = np.eye(FN_GROUPS, dtype=np.float32)
    mix = np.concatenate([np.kron(eye, cgrp), -np.kron(eye, sgrp)], axis=0)
    out_specs = [pl.BlockSpec((1, tn, FN_WIDTH), lambda i, j: (i, j, 0))]
    out_shape = [jax.ShapeDtypeStruct((b, n_lat, FN_WIDTH), BF16)]
    if with_ctx_outputs:
        out_specs.append(pl.BlockSpec((1, n_ctx, FN_WIDTH), lambda i, j: (i, 0, 0)))
        out_shape.append(jax.ShapeDtypeStruct((b, n_ctx, FN_WIDTH), BF16))
    outs = pl.pallas_call(
        functools.partial(_fourier_kernel, n_lat=n_lat),
        grid=(b, n_lat // tn),
        in_specs=[
            pl.BlockSpec((1, s, w), lambda i, j: (i, 0, 0)),
            pl.BlockSpec((1, 2 * tn, n_lat), lambda i, j: (j, 0, 0)),
            _const_spec(table_ctx.shape),
            _const_spec(mix.shape),
            _const_spec(w_fourier.shape),
        ],
        out_specs=out_specs,
        out_shape=out_shape,
        compiler_params=_params(2),
        name="fourier",
    )(fn_in, jnp.asarray(table_lat, BF16), jnp.asarray(table_ctx, BF16), jnp.asarray(mix, BF16), w_fourier)
    return (outs[0], outs[1]) if with_ctx_outputs else (outs[0], None)


def _ffn_kernel(*refs, final_norm, n_lat, select_ctx, n_srcs):
    streams = []
    pos = 0
    for n in n_srcs:
        streams.append(_stream_tile(refs[pos:pos + n]))
        pos += n
    x, attn, hg, fn = streams
    modb_ref, modc_ref, g2_ref, wo_ref, wgu_ref, wd_ref, gf_ref, o_ref, act_ref = refs[pos:]
    tm = x.shape[0]

    def mod_vec(r):
        if select_ctx:
            return _row_select(tm, n_lat, modc_ref[0, r:r + 1, :], modb_ref[0, r:r + 1, :])
        return modb_ref[0, r:r + 1, :]

    y = (_dot(attn, wo_ref[0:MLA_WIDTH, :])
         + _dot(hg, wo_ref[MLA_WIDTH:MLA_WIDTH + HG_WIDTH, :])
         + _dot(fn, wo_ref[MLA_WIDTH + HG_WIDTH:MLA_WIDTH + HG_WIDTH + FN_WIDTH, :]))
    x1 = x + mod_vec(2) * y
    o_ref[0] = x1
    xn = x1 * lax.rsqrt(jnp.mean(x1 * x1, axis=-1, keepdims=True) + EPS) * g2_ref[...]
    hm = (xn * (1.0 + mod_vec(4)) + mod_vec(3)).astype(BF16)

    ft = wgu_ref.shape[2] // 2
    for t in range(wgu_ref.shape[0]):
        gu = _dot(hm, wgu_ref[t])
        act_ref[:, t * ft:(t + 1) * ft] = (_silu(gu[:, 0:ft]) * gu[:, ft:2 * ft]).astype(BF16)
    out = o_ref[0] + mod_vec(5) * _dot(act_ref[...], wd_ref[...])
    if final_norm:
        out = out * lax.rsqrt(jnp.mean(out * out, axis=-1, keepdims=True) + EPS) * gf_ref[...]
    o_ref[0] = out


def _outproj_ffn(x_src, attn, hg, fn, mod, g2, w_out, w_gu, w_d, g_final, n_lat, n_ctx, ctx_row, with_ctx_rows,
                 final_norm):
    d = g2.shape[0]
    b = (x_src[0] if isinstance(x_src, tuple) else x_src).shape[0]
    tm = STREAM_TILE if with_ctx_rows else LATENT_TILE
    n_rows = n_lat + n_ctx if with_ctx_rows else n_lat
    hidden = w_d.shape[0]
    row = lambda w: pl.BlockSpec((1, tm, w), lambda i, j: (i, j, 0))
    src_specs, src_args, n_srcs = [], [], []
    for src in (x_src, attn, hg, fn):
        specs, args = _stream_specs(src, tm)
        src_specs += specs
        src_args += args
        n_srcs.append(len(args))
    return pl.pallas_call(
        functools.partial(_ffn_kernel, final_norm=final_norm, n_lat=n_lat, select_ctx=with_ctx_rows,
                          n_srcs=tuple(n_srcs)),
        grid=(b, n_rows // tm),
        in_specs=src_specs + [
            pl.BlockSpec((1, 6, d), lambda i, j: (i, 0, 0)),
            pl.BlockSpec((1, 6, d), lambda i, j: (ctx_row, 0, 0)),
            _const_spec((1, d)),
            _RESIDENT,
            _RESIDENT,
            _RESIDENT,
            _const_spec((1, d)),
        ],
        out_specs=row(d),
        out_shape=jax.ShapeDtypeStruct((b, n_rows, d), F32),
        scratch_shapes=[pltpu.VMEM((tm, hidden), BF16)],
        compiler_params=_params(2),
        name="outproj_ffn",
    )(*src_args, mod, mod, g2.reshape(1, d), w_out, w_gu, w_d, g_final.reshape(1, d))


def _rotate_half_cols(w):
    r1, r2, c1, c2 = jnp.split(w, 4, axis=-1)
    return jnp.concatenate([-r2, r1, -c2, c1], axis=-1)


def _pack_w_in(w):
    d = w.shape[0]
    zeros = jnp.zeros((d, 128 - MLA_ROPE), w.dtype)
    w_kr = w[:, O_KR:O_HQ]
    return jnp.concatenate([w[:, O_CQ:O_KR], w_kr, zeros, _rotate_half_cols(w_kr), zeros, w[:, O_HQ:]],
                           axis=1).astype(BF16)


def _pack_w_uq(w):
    r = w.shape[0]
    wh = w.reshape(r, MLA_HEADS, MLA_NOPE + MLA_ROPE)
    nope = wh[:, :, :MLA_NOPE].reshape(r, MLA_HEADS * MLA_NOPE)
    pe = wh[:, :, MLA_NOPE:]
    zeros = jnp.zeros((r, MLA_HEADS, 128 - MLA_ROPE), w.dtype)
    pe_a = jnp.concatenate([pe, zeros], axis=-1).reshape(r, MLA_HEADS * 128)
    pe_b = jnp.concatenate([_rotate_half_cols(pe), zeros], axis=-1).reshape(r, MLA_HEADS * 128)
    return jnp.concatenate([nope, pe_a, pe_b], axis=1).astype(BF16)


def _pack_w_ukv(w):
    r = w.shape[0]
    wh = w.reshape(r, MLA_HEADS, MLA_NOPE + MLA_V)
    k_nope = wh[:, :, :MLA_NOPE].reshape(r, MLA_HEADS * MLA_NOPE)
    v = wh[:, :, MLA_NOPE:].reshape(r, MLA_HEADS * MLA_V)
    return k_nope.astype(BF16), v.T.astype(BF16)


def _pack_w_gate_up(w, hidden):
    d = w.shape[0]
    n_ft = hidden // FFN_TILE
    gate = w[:, :hidden].reshape(d, n_ft, FFN_TILE)
    up = w[:, hidden:].reshape(d, n_ft, FFN_TILE)
    return jnp.concatenate([gate, up], axis=2).transpose(1, 0, 2).astype(BF16)


def _rope_tables(n_lat, n_ctx):
    rows = n_lat // GRID_W
    row_pos = np.repeat(np.arange(rows, dtype=np.float32), GRID_W)
    col_pos = np.tile(np.arange(GRID_W, dtype=np.float32), rows)
    axis_dim = MLA_ROPE // 2
    inv_freq = (ROPE_BASE ** (-np.arange(0, axis_dim, 2, dtype=np.float32) / axis_dim)).astype(np.float32)
    ang_r = row_pos[:, None] * inv_freq
    ang_c = col_pos[:, None] * inv_freq
    ang = np.concatenate([ang_r, ang_r, ang_c, ang_c], axis=-1)
    cos = np.ones((n_lat + n_ctx, 128), np.float32)
    sin = np.zeros((n_lat + n_ctx, 128), np.float32)
    cos[:n_lat, :MLA_ROPE] = np.cos(ang)
    sin[:n_lat, :MLA_ROPE] = np.sin(ang)
    return jnp.asarray(cos), jnp.asarray(sin)


def kernel(x, c, ctx, c_ctx, w_mod, b_mod, norm1_g, norm2_g, w_in, q_norm_g, w_uq, kv_norm_g, w_ukv, lb_param,
           hg_norm_g, w_fourier, w_out, w_gate_up, w_down, final_norm_g):
    b, t, d = x.shape
    n_ctx = ctx.shape[1]
    s = t + n_ctx
    depth = w_mod.shape[0]
    hidden = w_down.shape[1]
    assert s % STREAM_TILE == 0 and t % LATENT_TILE == 0 and t % ATTN_TILE == 0 and t % FOURIER_TILE == 0
    assert hidden % FFN_TILE == 0 and t % GRID_W == 0 and n_ctx % HG_CHUNK == 0

    mod_rows = -(-(b + 1) // 8) * 8
    c_all = jnp.zeros((mod_rows, d), F32).at[:b].set(c).at[b].set(c_ctx)
    mod_all = _modulation(c_all, w_mod, b_mod).reshape(depth, mod_rows, 6, d)
    lower = _lower_bounds(lb_param)
    cos_t, sin_t = _rope_tables(t, n_ctx)

    xa = (x, ctx)
    for l in range(depth):
        last = l == depth - 1
        q, k, v, hg_in, fn_in = _inproj(xa, mod_all[l], norm1_g[l], _pack_w_in(w_in[l]), q_norm_g[l],
                                        _pack_w_uq(w_uq[l]), kv_norm_g[l], *_pack_w_ukv(w_ukv[l]), cos_t, sin_t, t, b)
        attn, attn_ctx = _attention(q, k, v, t, not last)
        hg = _hgrn(hg_in, lower[l], hg_norm_g[l], t, not last)
        fn, fn_ctx = _fourier(fn_in, w_fourier[l].astype(BF16), t, not last)
        if not last:
            attn, fn = (attn, attn_ctx), (fn, fn_ctx)
        xa = _outproj_ffn(xa, attn, hg, fn, mod_all[l], norm2_g[l], w_out[l].astype(BF16),
                          _pack_w_gate_up(w_gate_up[l], hidden), w_down[l].astype(BF16), final_norm_g, t, n_ctx, b,
                          not last, last)
    return xa
```

```python
import functools
import math

import numpy as np
import jax
import jax.numpy as jnp
from jax import lax
from jax.experimental import pallas as pl
from jax.experimental.pallas import tpu as pltpu

EPS = 1e-6
GRID_W = 64
ROPE_BASE = 10000.0

MLA_HEADS = 4
MLA_Q_LORA = 256
MLA_KV_LORA = 128
MLA_NOPE = 128
MLA_ROPE = 64
MLA_V = 128
MLA_WIDTH = MLA_HEADS * MLA_V
HEAD_PAD = 256

HG_HEADS = 4
HG_K = 128
HG_V = 64
HG_WIDTH = HG_HEADS * HG_V
HG_KW = HG_HEADS * HG_K
HG_CHUNK = 64
HG_LEVELS = 6
HG_MATMUL_LEVELS = 3
HG_UNROLL = 8

FN_GROUPS = 4
FN_GROUP_DIM = 64
FN_WIDTH = FN_GROUPS * FN_GROUP_DIM

O_CQ = 0
O_CKV = O_CQ + MLA_Q_LORA
O_KR = O_CKV + MLA_KV_LORA
O_HQ = O_KR + MLA_ROPE

P_CQ = 0
P_CKV = P_CQ + MLA_Q_LORA
P_KPA = P_CKV + MLA_KV_LORA
P_KPB = P_KPA + 128
P_HG = P_KPB + 128
HG_IN_WIDTH = 3 * HG_KW + 2 * HG_WIDTH
P_FN = P_HG + HG_IN_WIDTH
P_WIDTH = P_FN + FN_WIDTH

STREAM_TILE = 768
LATENT_TILE = 1024
ATTN_TILE = 512
FOURIER_TILE = 512
FFN_TILE = 256
VMEM_LIMIT = 56 * 1024 * 1024

BF16 = jnp.bfloat16
F32 = jnp.float32


def _dot(a, b):
    return jnp.dot(a, b, preferred_element_type=F32)


def _dot_nt(a, b):
    return lax.dot_general(a, b, (((1,), (1,)), ((), ())), preferred_element_type=F32)


def _dot_tn(a, b):
    return lax.dot_general(a, b, (((0,), (0,)), ((), ())), preferred_element_type=F32)


def _silu(x):
    return x * (1.0 / (1.0 + jnp.exp(-x)))


def _sigmoid(x):
    return 1.0 / (1.0 + jnp.exp(-x))


def _split3(x):
    hi = x.astype(BF16)
    r1 = x - hi.astype(F32)
    mid = r1.astype(BF16)
    lo = (r1 - mid.astype(F32)).astype(BF16)
    return hi, mid, lo


def _const_spec(shape):
    nd = len(shape)
    return pl.BlockSpec(shape, lambda *_: (0,) * nd)


_RESIDENT = pl.BlockSpec(memory_space=pltpu.VMEM)


def _params(n_axes):
    return pltpu.CompilerParams(dimension_semantics=("arbitrary",) * n_axes, vmem_limit_bytes=VMEM_LIMIT)


def _mod_kernel(c_ref, w_ref, b_ref, o_ref):
    act = _silu(c_ref[...]).astype(BF16)
    o_ref[0] = _dot(act, w_ref[0].astype(BF16)) + b_ref[0]


def _modulation(c_all, w_mod, b_mod):
    depth, d, n6 = w_mod.shape
    rows = c_all.shape[0]
    tn = n6 // 6
    return pl.pallas_call(
        _mod_kernel,
        grid=(depth, n6 // tn),
        in_specs=[
            pl.BlockSpec((rows, d), lambda l, j: (0, 0)),
            pl.BlockSpec((1, d, tn), lambda l, j: (l, 0, j)),
            pl.BlockSpec((1, 1, tn), lambda l, j: (l, 0, j)),
        ],
        out_specs=pl.BlockSpec((1, rows, tn), lambda l, j: (l, 0, j)),
        out_shape=jax.ShapeDtypeStruct((depth, rows, n6), F32),
        name="modulation",
    )(c_all, w_mod, b_mod.reshape(depth, 1, n6))


def _lower_bound_kernel(lb_ref, o_ref):
    depth = lb_ref.shape[0]
    lp = [lb_ref[l] for l in range(depth)]
    m = lp[0]
    for l in range(1, depth):
        m = jnp.maximum(m, lp[l])
    e = [jnp.exp(v - m) for v in lp]
    tot = e[0]
    for l in range(1, depth):
        tot = tot + e[l]
    probs = [v / tot for v in e]
    cum = probs[0]
    o_ref[0] = cum - probs[0]
    for l in range(1, depth):
        cum = cum + probs[l]
        o_ref[l] = cum - probs[0]


def _lower_bounds(lb_param):
    return pl.pallas_call(
        _lower_bound_kernel,
        out_shape=jax.ShapeDtypeStruct(lb_param.shape, F32),
        name="hgrn_lower_bounds",
    )(lb_param.astype(F32))


def _row_select(tile_rows, n_lat, ctx_vec, lat_vec):
    row = pl.program_id(1) * tile_rows + lax.broadcasted_iota(jnp.int32, (tile_rows, 1), 0)
    return jnp.where(row >= n_lat, ctx_vec, lat_vec)


def _stream_specs(src, tm):
    if not isinstance(src, tuple):
        return [pl.BlockSpec((1, tm, src.shape[2]), lambda i, j: (i, j, 0))], [src]
    lat, ctx = src
    t, d = lat.shape[1:]
    n_ctx = ctx.shape[1]
    n_full = t // tm
    tail = t - n_full * tm
    assert n_full >= 1 and tail > 0 and tail + n_ctx == tm and (n_full * tm) % tail == 0
    specs = [pl.BlockSpec((1, tm, d), lambda i, j: (i, jnp.minimum(j, n_full - 1), 0)),
             pl.BlockSpec((1, tail, d), lambda i, j: (i, (n_full * tm) // tail, 0)),
             pl.BlockSpec((1, n_ctx, d), lambda i, j: (i, 0, 0))]
    return specs, [lat, lat, ctx]


def _stream_tile(refs):
    if len(refs) == 1:
        return refs[0][0]
    main_ref, tail_ref, ctx_ref = refs
    mixed = jnp.concatenate([tail_ref[0], ctx_ref[0]], axis=0)
    return jnp.where(pl.program_id(1) < pl.num_programs(1) - 1, main_ref[0], mixed)


def _inproj_kernel(*refs, q_scale, n_lat, n_src):
    (modb_ref, modc_ref, g1_ref, w_in_ref, qg_ref, wq_ref, kvg_ref, wk_ref, wvt_ref, cos_ref, sin_ref,
     q_ref, k_ref, vt_ref, hg_ref, fn_ref) = refs[n_src:]
    x = _stream_tile(refs[:n_src])
    tm = x.shape[0]
    shift = _row_select(tm, n_lat, modc_ref[0, 0:1, :], modb_ref[0, 0:1, :])
    scale = _row_select(tm, n_lat, modc_ref[0, 1:2, :], modb_ref[0, 1:2, :])
    xn = x * lax.rsqrt(jnp.mean(x * x, axis=-1, keepdims=True) + EPS) * g1_ref[...]
    xm = (xn * (1.0 + scale) + shift).astype(BF16)

    hg_ref[0] = _dot(xm, w_in_ref[:, P_HG:P_FN]).astype(BF16)
    fn_ref[0] = _dot(xm, w_in_ref[:, P_FN:P_WIDTH]).astype(BF16)
    p = _dot(xm, w_in_ref[:, 0:P_HG])

    cos = cos_ref[...]
    sin = sin_ref[...]

    cq = p[:, P_CQ:P_CKV]
    cqn = (cq * lax.rsqrt(jnp.mean(cq * cq, axis=-1, keepdims=True) + EPS) * qg_ref[...]).astype(BF16)
    qq = _dot(cqn, wq_ref[...])
    nw = MLA_HEADS * MLA_NOPE
    for h in range(MLA_HEADS):
        q_ref[0, :, h * HEAD_PAD:h * HEAD_PAD + 128] = (qq[:, h * 128:(h + 1) * 128] * q_scale).astype(BF16)
        pe = qq[:, nw + h * 128:nw + (h + 1) * 128] * cos + qq[:, 2 * nw + h * 128:2 * nw + (h + 1) * 128] * sin
        q_ref[0, :, h * HEAD_PAD + 128:(h + 1) * HEAD_PAD] = (pe * q_scale).astype(BF16)

    ckv = p[:, P_CKV:P_KPA]
    ckvn = (ckv * lax.rsqrt(jnp.mean(ckv * ckv, axis=-1, keepdims=True) + EPS) * kvg_ref[...]).astype(BF16)
    kv = _dot(ckvn, wk_ref[...])
    vt_ref[0] = _dot_nt(wvt_ref[...], ckvn).astype(BF16)
    kpe = (p[:, P_KPA:P_KPB] * cos + p[:, P_KPB:P_HG] * sin).astype(BF16)
    for h in range(MLA_HEADS):
        k_ref[0, :, h * HEAD_PAD:h * HEAD_PAD + 128] = kv[:, h * 128:(h + 1) * 128].astype(BF16)
        k_ref[0, :, h * HEAD_PAD + 128:(h + 1) * HEAD_PAD] = kpe


def _inproj(src, mod, g1, w_in_p, qg, wq_all, kvg, wk_p, wvt_p, cos_t, sin_t, n_lat, ctx_row):
    s = cos_t.shape[0]
    d = g1.shape[0]
    b = (src[0] if isinstance(src, tuple) else src).shape[0]
    tm = STREAM_TILE
    row = lambda w: pl.BlockSpec((1, tm, w), lambda i, j: (i, j, 0))
    src_specs, src_args = _stream_specs(src, tm)
    return pl.pallas_call(
        functools.partial(_inproj_kernel, q_scale=math.log2(math.e) / math.sqrt(MLA_NOPE + MLA_ROPE), n_lat=n_lat,
                          n_src=len(src_args)),
        grid=(b, s // tm),
        in_specs=src_specs + [
            pl.BlockSpec((1, 6, d), lambda i, j: (i, 0, 0)),
            pl.BlockSpec((1, 6, d), lambda i, j: (ctx_row, 0, 0)),
            _const_spec((1, d)),
            _RESIDENT,
            _const_spec((1, MLA_Q_LORA)),
            _RESIDENT,
            _const_spec((1, MLA_KV_LORA)),
            _RESIDENT,
            _RESIDENT,
            pl.BlockSpec((tm, 128), lambda i, j: (j, 0)),
            pl.BlockSpec((tm, 128), lambda i, j: (j, 0)),
        ],
        out_specs=[row(MLA_HEADS * HEAD_PAD), row(MLA_HEADS * HEAD_PAD),
                   pl.BlockSpec((1, MLA_WIDTH, tm), lambda i, j: (i, 0, j)), row(HG_IN_WIDTH), row(FN_WIDTH)],
        out_shape=[
            jax.ShapeDtypeStruct((b, s, MLA_HEADS * HEAD_PAD), BF16),
            jax.ShapeDtypeStruct((b, s, MLA_HEADS * HEAD_PAD), BF16),
            jax.ShapeDtypeStruct((b, MLA_WIDTH, s), BF16),
            jax.ShapeDtypeStruct((b, s, HG_IN_WIDTH), BF16),
            jax.ShapeDtypeStruct((b, s, FN_WIDTH), BF16),
        ],
        compiler_params=_params(2),
        name="inproj",
    )(*src_args, mod, mod, g1.reshape(1, d), w_in_p, qg.reshape(1, -1), wq_all, kvg.reshape(1, -1), wk_p, wvt_p,
      cos_t, sin_t)


def _attend(q_ref, k_ref, vt_ref, o_ref, key_lo, key_hi):
    def scores(h):
        qh = q_ref[0, :, h * HEAD_PAD:(h + 1) * HEAD_PAD]
        kh = k_ref[0, key_lo:key_hi, h * HEAD_PAD:(h + 1) * HEAD_PAD]
        return _dot_nt(kh, qh)

    all_scores = [scores(h) for h in range(MLA_HEADS)]
    probs = []
    for st in all_scores:
        e = jnp.exp2(st - jnp.max(st, axis=0, keepdims=True))
        probs.append((e.astype(BF16), 1.0 / jnp.sum(e, axis=0, keepdims=True)))
    for h, (e, inv_denom) in enumerate(probs):
        vth = vt_ref[0, h * MLA_V:(h + 1) * MLA_V, key_lo:key_hi]
        ot = _dot(vth, e) * inv_denom
        o_ref[0, :, h * MLA_V:(h + 1) * MLA_V] = ot.T.astype(BF16)


def _attn_kernel(q_ref, k_ref, vt_ref, o_ref):
    _attend(q_ref, k_ref, vt_ref, o_ref, 0, k_ref.shape[1])


def _attn_ctx_kernel(q_ref, qc_ref, k_ref, vt_ref, o_ref, oc_ref, *, n_lat):
    s_len = k_ref.shape[1]
    _attend(q_ref, k_ref, vt_ref, o_ref, 0, s_len)

    @pl.when(pl.program_id(1) == pl.num_programs(1) - 1)
    def _():
        _attend(qc_ref, k_ref, vt_ref, oc_ref, n_lat, s_len)


def _attention(q, k, vt, n_lat, with_ctx_queries):
    b, s, _ = q.shape
    tq = ATTN_TILE
    n_ctx = s - n_lat
    q_spec = pl.BlockSpec((1, tq, q.shape[2]), lambda i, j: (i, j, 0))
    kv_specs = [pl.BlockSpec((1, s, k.shape[2]), lambda i, j: (i, 0, 0)),
                pl.BlockSpec((1, vt.shape[1], s), lambda i, j: (i, 0, 0))]
    o_spec = pl.BlockSpec((1, tq, MLA_WIDTH), lambda i, j: (i, j, 0))
    o_shape = jax.ShapeDtypeStruct((b, n_lat, MLA_WIDTH), BF16)
    if not with_ctx_queries:
        out = pl.pallas_call(
            _attn_kernel,
            grid=(b, n_lat // tq),
            in_specs=[q_spec] + kv_specs,
            out_specs=o_spec,
            out_shape=o_shape,
            compiler_params=_params(2),
            name="attention",
        )(q, k, vt)
        return out, None
    assert n_lat % n_ctx == 0
    return pl.pallas_call(
        functools.partial(_attn_ctx_kernel, n_lat=n_lat),
        grid=(b, n_lat // tq),
        in_specs=[q_spec, pl.BlockSpec((1, n_ctx, q.shape[2]), lambda i, j: (i, n_lat // n_ctx, 0))] + kv_specs,
        out_specs=[o_spec, pl.BlockSpec((1, n_ctx, MLA_WIDTH), lambda i, j: (i, 0, 0))],
        out_shape=[o_shape, jax.ShapeDtypeStruct((b, n_ctx, MLA_WIDTH), BF16)],
        compiler_params=_params(2),
        name="attention_ctx",
    )(q, q, k, vt)


def _hgrn_constants():
    c = HG_CHUNK
    t = np.arange(c)[:, None]
    u = np.arange(c)[None, :]
    blocks = [(u <= t)]
    for m in (8, 4, 2):
        half = m // 2
        mid = (t // m) * m + half
        lower = (t % m) >= half
        blocks.append(np.where(lower, (u >= mid) & (u <= t), (u >= t + 1) & (u <= mid - 1)))
    mf = np.stack([blk.astype(np.float32) for blk in blocks])
    mb = np.ascontiguousarray(mf[:, ::-1, ::-1])
    msb = np.floor(np.log2(np.maximum(t ^ u, 1))).astype(np.int32)
    lvl = np.where(t > u, HG_LEVELS - 1 - msb, np.where(t == u, HG_LEVELS, -1)).astype(np.int32)
    lvl_f = np.tile(lvl, (1, 2))
    lvl_b = np.tile(np.ascontiguousarray(lvl.T), (1, 2))
    return mf.reshape(-1, c), mb.reshape(-1, c), lvl_f, lvl_b


def _pair_weights(x, p):
    zeros = jnp.zeros((x.shape[0], HG_K), x.dtype)
    xa = x[:, 2 * p * HG_K:(2 * p + 1) * HG_K]
    xb = x[:, (2 * p + 1) * HG_K:(2 * p + 2) * HG_K]
    blk = jnp.concatenate([jnp.concatenate([xa, zeros], axis=1), jnp.concatenate([zeros, xb], axis=1)], axis=0)
    return blk.T


def _hgrn_gates(z, lb, m_ref, with_output):
    f = lb + (1.0 - lb) * _sigmoid(z)
    g = jnp.log2(f)
    g_hi = g.astype(BF16)
    g_lo = (g - g_hi.astype(F32)).astype(BF16)
    m_all = m_ref[...] if with_output else m_ref[0:HG_CHUNK, :]
    return 1.0 - f, _dot(m_all, g_hi) + _dot(m_all, g_lo)


def _hgrn_decays(hq, kk, eb, fwd, with_output):
    c = HG_CHUNK
    bc = eb[0:c]
    total = bc[c - 1:c] if fwd else bc[0:1]
    k_up = (kk * jnp.exp2(total - bc)).astype(BF16)
    chunk_decay = jnp.exp2(total)
    if not with_output:
        return k_up, chunk_decay, None
    qq = _silu(hq)
    row = lax.broadcasted_iota(jnp.int32, (c, HG_KW), 0)
    xs = []
    for l in range(HG_LEVELS):
        m = c >> l
        half = m // 2
        bit = (row & half) != 0
        is_query = bit if fwd else jnp.logical_not(bit)
        if l < HG_LEVELS - HG_MATMUL_LEVELS:
            refs = [bc[b0 + half - 1:b0 + half] if fwd else bc[b0 + half:b0 + half + 1] for b0 in range(0, c, m)]
            ref = jnp.concatenate([jnp.broadcast_to(r, (m, HG_KW)) for r in refs], axis=0)
            diff = bc - ref
            expo = jnp.where(is_query, diff, -diff)
        else:
            idx = 1 + l - (HG_LEVELS - HG_MATMUL_LEVELS)
            expo = eb[idx * c:(idx + 1) * c]
        xs.append((jnp.where(is_query, qq, kk) * jnp.exp2(expo)).astype(BF16))
    q_in = (qq * jnp.exp2(bc)).astype(BF16)
    return k_up, chunk_decay, (xs, qq.astype(BF16), kk.astype(BF16), q_in)


def _hgrn_scores(operands, lvl_masks):
    xs, qb, kb, _ = operands
    scores = []
    for p in range(HG_HEADS // 2):
        kp = slice(2 * p * HG_K, (2 * p + 2) * HG_K)
        sc = jnp.where(lvl_masks[HG_LEVELS], _dot(qb[:, kp], _pair_weights(kb, p)), 0.0)
        for l in range(HG_LEVELS):
            sc = jnp.where(lvl_masks[l], _dot(xs[l][:, kp], _pair_weights(xs[l], p)), sc)
        scores.append(sc.astype(BF16))
    return scores


def _hgrn_apply(scores, operands, k_up, chunk_decay, vv, v_lo, v_hi, state_ref):
    outs = []
    for p in range(HG_HEADS // 2):
        kp = slice(2 * p * HG_K, (2 * p + 2) * HG_K)
        vp = slice(2 * p * HG_V, (2 * p + 2) * HG_V)
        if scores is not None:
            v_pair = vv[:, vp]
            v_diag = jnp.concatenate([v_pair * v_lo, v_pair * v_hi], axis=0)
            st_pair = state_ref[2 * p * HG_V:(2 * p + 2) * HG_V, kp]
            outs.append(_dot(scores[p], v_diag) + _dot(operands[3][:, kp], st_pair.astype(BF16).T))
        for h in (2 * p, 2 * p + 1):
            ks = slice(h * HG_K, (h + 1) * HG_K)
            vs = slice(h * HG_V, (h + 1) * HG_V)
            state_ref[vs, ks] = chunk_decay[:, ks] * state_ref[vs, ks] + _dot_tn(vv[:, vs], k_up[:, ks])
    return jnp.concatenate(outs, axis=1) if outs else None


def _hgrn_kernel(x_ref, lb_ref, gain_ref, mf_ref, mb_ref, lvlf_ref, lvlb_ref, ones_ref, o_ref,
                 of_ref, ob_ref, sf_ref, sb_ref, *, n_lat_chunks):
    c = HG_CHUNK
    s_len = x_ref.shape[1]
    n_chunks = s_len // c
    n_ctx_chunks = n_chunks - n_lat_chunks
    sf_ref[...] = jnp.zeros_like(sf_ref)
    sb_ref[...] = jnp.zeros_like(sb_ref)
    lvl_f = lvlf_ref[...]
    lvl_b = lvlb_ref[...]
    masks_f = [lvl_f == l for l in range(HG_LEVELS + 1)]
    masks_b = [lvl_b == l for l in range(HG_LEVELS + 1)]
    lb_f = lb_ref[0:1, :]
    lb_b = lb_ref[1:2, :]
    v_lane = lax.broadcasted_iota(jnp.int32, (c, 2 * HG_V), 1)
    v_lo = (v_lane < HG_V).astype(F32).astype(BF16)
    v_hi = (v_lane >= HG_V).astype(F32).astype(BF16)

    def load(ci):
        r0 = pl.multiple_of(ci * c, c)
        hq = x_ref[0, pl.ds(r0, c), 0:HG_KW].astype(F32)
        vv = x_ref[0, pl.ds(r0, c), 3 * HG_KW:3 * HG_KW + HG_WIDTH]
        return r0, hq, vv

    def scan(first_fwd, first_bwd, n, with_output):
        per_trip = HG_UNROLL if n % HG_UNROLL == 0 else n

        def step(i, carry):
            gated = []
            for u in range(per_trip):
                for fwd in (True, False):
                    ci = first_fwd + i * per_trip + u if fwd else first_bwd - i * per_trip - u
                    r0, hq, vv = load(ci)
                    zcol = HG_KW if fwd else 2 * HG_KW
                    z = x_ref[0, pl.ds(r0, c), zcol:zcol + HG_KW].astype(F32)
                    gated.append((fwd, r0, hq, vv, _hgrn_gates(z, lb_f if fwd else lb_b, mf_ref if fwd else mb_ref,
                                                               with_output)))
            jobs = [(fwd, r0, vv, _hgrn_decays(hq, kk, eb, fwd, with_output)) for fwd, r0, hq, vv, (kk, eb) in gated]
            scores = [_hgrn_scores(ph[2], masks_f if fwd else masks_b) if with_output else None
                      for fwd, _, _, ph in jobs]
            for (fwd, r0, vv, (k_up, chunk_decay, operands)), sc in zip(jobs, scores):
                out = _hgrn_apply(sc, operands, k_up, chunk_decay, vv, v_lo, v_hi, sf_ref if fwd else sb_ref)
                if with_output:
                    (of_ref if fwd else ob_ref)[pl.ds(r0, c), :] = out
            return carry

        lax.fori_loop(0, n // per_trip, step, 0)

    scan(n_lat_chunks, n_chunks - 1, n_ctx_chunks, o_ref.shape[1] == s_len)
    scan(0, n_lat_chunks - 1, n_lat_chunks, True)

    rt = 256
    gain = gain_ref[...]
    ones = ones_ref[...]

    def readout(i, carry):
        r0 = pl.multiple_of(i * rt, rt)
        o = of_ref[pl.ds(r0, rt), :] + ob_ref[pl.ds(r0, rt), :]
        sq_hi, sq_mid, sq_lo = _split3(o * o)
        msq = _dot(sq_hi, ones) + _dot(sq_mid, ones) + _dot(sq_lo, ones)
        zg = x_ref[0, pl.ds(r0, rt), 3 * HG_KW + HG_WIDTH:HG_IN_WIDTH].astype(F32)
        y = o * lax.rsqrt(msq + EPS) * gain * _silu(zg)
        o_ref[0, pl.ds(r0, rt), :] = y.astype(BF16)
        return carry

    lax.fori_loop(0, o_ref.shape[1] // rt, readout, 0)


def _hgrn(hg_in, lb, gain, n_lat, with_ctx_outputs):
    b, s, w = hg_in.shape
    mf, mb, lvl_f, lvl_b = _hgrn_constants()
    ones = np.kron(np.eye(HG_HEADS, dtype=np.float32), np.full((HG_V, HG_V), 1.0 / HG_V, np.float32))
    n_out = s if with_ctx_outputs else n_lat
    return pl.pallas_call(
        functools.partial(_hgrn_kernel, n_lat_chunks=n_lat // HG_CHUNK),
        grid=(b,),
        in_specs=[
            pl.BlockSpec((1, s, w), lambda i: (i, 0, 0)),
            _const_spec((2, HG_KW)),
            _const_spec((1, HG_WIDTH)),
            _const_spec(mf.shape),
            _const_spec(mb.shape),
            _const_spec(lvl_f.shape),
            _const_spec(lvl_b.shape),
            _const_spec(ones.shape),
        ],
        out_specs=pl.BlockSpec((1, n_out, HG_WIDTH), lambda i: (i, 0, 0)),
        out_shape=jax.ShapeDtypeStruct((b, n_out, HG_WIDTH), BF16),
        scratch_shapes=[
            pltpu.VMEM((s, HG_WIDTH), F32),
            pltpu.VMEM((s, HG_WIDTH), F32),
            pltpu.VMEM((HG_WIDTH, HG_KW), F32),
            pltpu.VMEM((HG_WIDTH, HG_KW), F32),
        ],
        compiler_params=_params(1),
        name="hgrn2",
    )(hg_in, lb, gain.reshape(1, HG_WIDTH), jnp.asarray(mf, BF16), jnp.asarray(mb, BF16),
      jnp.asarray(lvl_f), jnp.asarray(lvl_b), jnp.asarray(ones, BF16))


def _dft_tables(n):
    idx = np.arange(n, dtype=np.int64)
    ang = 2.0 * np.pi * ((idx[:, None] * idx[None, :]) % n).astype(np.float64) / n
    return np.cos(ang).astype(np.float32), np.sin(ang).astype(np.float32)


def _fourier_tile(z, table, mix, w, norm):
    tn = table.shape[0] // 2
    zz = _dot(table, z)
    zcat = jnp.concatenate([zz[0:tn], zz[tn:2 * tn]], axis=1).astype(BF16)
    mixed = _dot(zcat, mix) * norm
    return _dot(mixed.astype(BF16), w).astype(BF16)


def _fourier_kernel(z_ref, tl_ref, tc_ref, mix_ref, w_ref, o_ref, *maybe_oc_ref, n_lat):
    s_len = z_ref.shape[1]
    n_ctx = s_len - n_lat
    mix = mix_ref[...]
    w = w_ref[...]
    o_ref[0] = _fourier_tile(z_ref[0, 0:n_lat, :], tl_ref[0], mix, w, 1.0 / math.sqrt(n_lat * FN_GROUP_DIM))
    if maybe_oc_ref:
        @pl.when(pl.program_id(1) == pl.num_programs(1) - 1)
        def _():
            maybe_oc_ref[0][0] = _fourier_tile(z_ref[0, n_lat:s_len, :], tc_ref[...], mix, w,
                                               1.0 / math.sqrt(n_ctx * FN_GROUP_DIM))


def _stacked_dft_table(n, tile):
    cos, sin = _dft_tables(n)
    return np.concatenate([cos.reshape(n // tile, tile, n), sin.reshape(n // tile, tile, n)], axis=1)


def _fourier(fn_in, w_fourier, n_lat, with_ctx_outputs):
    b, s, w = fn_in.shape
    n_ctx = s - n_lat
    tn = FOURIER_TILE
    table_lat = _stacked_dft_table(n_lat, tn)
    table_ctx = _stacked_dft_table(n_ctx, n_ctx)[0]
    cgrp, sgrp = _dft_tables(FN_GROUP_DIM)
    eye = np.eye(FN_GROUPS, dtype=np.float32)
    mix = np.concatenate([np.kron(eye, cgrp), -np.kron(eye, sgrp)], axis=0)
    out_specs = [pl.BlockSpec((1, tn, FN_WIDTH), lambda i, j: (i, j, 0))]
    out_shape = [jax.ShapeDtypeStruct((b, n_lat, FN_WIDTH), BF16)]
    if with_ctx_outputs:
        out_specs.append(pl.BlockSpec((1, n_ctx, FN_WIDTH), lambda i, j: (i, 0, 0)))
        out_shape.append(jax.ShapeDtypeStruct((b, n_ctx, FN_WIDTH), BF16))
    outs = pl.pallas_call(
        functools.partial(_fourier_kernel, n_lat=n_lat),
        grid=(b, n_lat // tn),
        in_specs=[
            pl.BlockSpec((1, s, w), lambda i, j: (i, 0, 0)),
            pl.BlockSpec((1, 2 * tn, n_lat), lambda i, j: (j, 0, 0)),
            _const_spec(table_ctx.shape),
            _const_spec(mix.shape),
            _const_spec(w_fourier.shape),
        ],
        out_specs=out_specs,
        out_shape=out_shape,
        compiler_params=_params(2),
        name="fourier",
    )(fn_in, jnp.asarray(table_lat, BF16), jnp.asarray(table_ctx, BF16), jnp.asarray(mix, BF16), w_fourier)
    return (outs[0], outs[1]) if with_ctx_outputs else (outs[0], None)


def _ffn_kernel(*refs, final_norm, n_lat, select_ctx, n_srcs):
    streams = []
    pos = 0
    for n in n_srcs:
        streams.append(_stream_tile(refs[pos:pos + n]))
        pos += n
    x, attn, hg, fn = streams
    modb_ref, modc_ref, g2_ref, wo_ref, wgu_ref, wd_ref, gf_ref, o_ref, act_ref = refs[pos:]
    tm = x.shape[0]

    def mod_vec(r):
        if select_ctx:
            return _row_select(tm, n_lat, modc_ref[0, r:r + 1, :], modb_ref[0, r:r + 1, :])
        return modb_ref[0, r:r + 1, :]

    y = (_dot(attn, wo_ref[0:MLA_WIDTH, :])
         + _dot(hg, wo_ref[MLA_WIDTH:MLA_WIDTH + HG_WIDTH, :])
         + _dot(fn, wo_ref[MLA_WIDTH + HG_WIDTH:MLA_WIDTH + HG_WIDTH + FN_WIDTH, :]))
    x1 = x + mod_vec(2) * y
    o_ref[0] = x1
    xn = x1 * lax.rsqrt(jnp.mean(x1 * x1, axis=-1, keepdims=True) + EPS) * g2_ref[...]
    hm = (xn * (1.0 + mod_vec(4)) + mod_vec(3)).astype(BF16)

    ft = wgu_ref.shape[2] // 2
    for t in range(wgu_ref.shape[0]):
        gu = _dot(hm, wgu_ref[t])
        act_ref[:, t * ft:(t + 1) * ft] = (_silu(gu[:, 0:ft]) * gu[:, ft:2 * ft]).astype(BF16)
    out = o_ref[0] + mod_vec(5) * _dot(act_ref[...], wd_ref[...])
    if final_norm:
        out = out * lax.rsqrt(jnp.mean(out * out, axis=-1, keepdims=True) + EPS) * gf_ref[...]
    o_ref[0] = out


def _outproj_ffn(x_src, attn, hg, fn, mod, g2, w_out, w_gu, w_d, g_final, n_lat, n_ctx, ctx_row, with_ctx_rows,
                 final_norm):
    d = g2.shape[0]
    b = (x_src[0] if isinstance(x_src, tuple) else x_src).shape[0]
    tm = STREAM_TILE if with_ctx_rows else LATENT_TILE
    n_rows = n_lat + n_ctx if with_ctx_rows else n_lat
    hidden = w_d.shape[0]
    row = lambda w: pl.BlockSpec((1, tm, w), lambda i, j: (i, j, 0))
    src_specs, src_args, n_srcs = [], [], []
    for src in (x_src, attn, hg, fn):
        specs, args = _stream_specs(src, tm)
        src_specs += specs
        src_args += args
        n_srcs.append(len(args))
    return pl.pallas_call(
        functools.partial(_ffn_kernel, final_norm=final_norm, n_lat=n_lat, select_ctx=with_ctx_rows,
                          n_srcs=tuple(n_srcs)),
        grid=(b, n_rows // tm),
        in_specs=src_specs + [
            pl.BlockSpec((1, 6, d), lambda i, j: (i, 0, 0)),
            pl.BlockSpec((1, 6, d), lambda i, j: (ctx_row, 0, 0)),
            _const_spec((1, d)),
            _RESIDENT,
            _RESIDENT,
            _RESIDENT,
            _const_spec((1, d)),
        ],
        out_specs=row(d),
        out_shape=jax.ShapeDtypeStruct((b, n_rows, d), F32),
        scratch_shapes=[pltpu.VMEM((tm, hidden), BF16)],
        compiler_params=_params(2),
        name="outproj_ffn",
    )(*src_args, mod, mod, g2.reshape(1, d), w_out, w_gu, w_d, g_final.reshape(1, d))


def _rotate_half_cols(w):
    r1, r2, c1, c2 = jnp.split(w, 4, axis=-1)
    return jnp.concatenate([-r2, r1, -c2, c1], axis=-1)


def _pack_w_in(w):
    d = w.shape[0]
    zeros = jnp.zeros((d, 128 - MLA_ROPE), w.dtype)
    w_kr = w[:, O_KR:O_HQ]
    return jnp.concatenate([w[:, O_CQ:O_KR], w_kr, zeros, _rotate_half_cols(w_kr), zeros, w[:, O_HQ:]],
                           axis=1).astype(BF16)


def _pack_w_uq(w):
    r = w.shape[0]
    wh = w.reshape(r, MLA_HEADS, MLA_NOPE + MLA_ROPE)
    nope = wh[:, :, :MLA_NOPE].reshape(r, MLA_HEADS * MLA_NOPE)
    pe = wh[:, :, MLA_NOPE:]
    zeros = jnp.zeros((r, MLA_HEADS, 128 - MLA_ROPE), w.dtype)
    pe_a = jnp.concatenate([pe, zeros], axis=-1).reshape(r, MLA_HEADS * 128)
    pe_b = jnp.concatenate([_rotate_half_cols(pe), zeros], axis=-1).reshape(r, MLA_HEADS * 128)
    return jnp.concatenate([nope, pe_a, pe_b], axis=1).astype(BF16)


def _pack_w_ukv(w):
    r = w.shape[0]
    wh = w.reshape(r, MLA_HEADS, MLA_NOPE + MLA_V)
    k_nope = wh[:, :, :MLA_NOPE].reshape(r, MLA_HEADS * MLA_NOPE)
    v = wh[:, :, MLA_NOPE:].reshape(r, MLA_HEADS * MLA_V)
    return k_nope.astype(BF16), v.T.astype(BF16)


def _pack_w_gate_up(w, hidden):
    d = w.shape[0]
    n_ft = hidden // FFN_TILE
    gate = w[:, :hidden].reshape(d, n_ft, FFN_TILE)
    up = w[:, hidden:].reshape(d, n_ft, FFN_TILE)
    return jnp.concatenate([gate, up], axis=2).transpose(1, 0, 2).astype(BF16)


def _rope_tables(n_lat, n_ctx):
    rows = n_lat // GRID_W
    row_pos = np.repeat(np.arange(rows, dtype=np.float32), GRID_W)
    col_pos = np.tile(np.arange(GRID_W, dtype=np.float32), rows)
    axis_dim = MLA_ROPE // 2
    inv_freq = (ROPE_BASE ** (-np.arange(0, axis_dim, 2, dtype=np.float32) / axis_dim)).astype(np.float32)
    ang_r = row_pos[:, None] * inv_freq
    ang_c = col_pos[:, None] * inv_freq
    ang = np.concatenate([ang_r, ang_r, ang_c, ang_c], axis=-1)
    cos = np.ones((n_lat + n_ctx, 128), np.float32)
    sin = np.zeros((n_lat + n_ctx, 128), np.float32)
    cos[:n_lat, :MLA_ROPE] = np.cos(ang)
    sin[:n_lat, :MLA_ROPE] = np.sin(ang)
    return jnp.asarray(cos), jnp.asarray(sin)


def kernel(x, c, ctx, c_ctx, w_mod, b_mod, norm1_g, norm2_g, w_in, q_norm_g, w_uq, kv_norm_g, w_ukv, lb_param,
           hg_norm_g, w_fourier, w_out, w_gate_up, w_down, final_norm_g):
    b, t, d = x.shape
    n_ctx = ctx.shape[1]
    s = t + n_ctx
    depth = w_mod.shape[0]
    hidden = w_down.shape[1]
    assert s % STREAM_TILE == 0 and t % LATENT_TILE == 0 and t % ATTN_TILE == 0 and t % FOURIER_TILE == 0
    assert hidden % FFN_TILE == 0 and t % GRID_W == 0 and n_ctx % HG_CHUNK == 0

    mod_rows = -(-(b + 1) // 8) * 8
    c_all = jnp.zeros((mod_rows, d), F32).at[:b].set(c).at[b].set(c_ctx)
    mod_all = _modulation(c_all, w_mod, b_mod).reshape(depth, mod_rows, 6, d)
    lower = _lower_bounds(lb_param)
    cos_t, sin_t = _rope_tables(t, n_ctx)

    xa = (x, ctx)
    for l in range(depth):
        last = l == depth - 1
        q, k, v, hg_in, fn_in = _inproj(xa, mod_all[l], norm1_g[l], _pack_w_in(w_in[l]), q_norm_g[l],
                                        _pack_w_uq(w_uq[l]), kv_norm_g[l], *_pack_w_ukv(w_ukv[l]), cos_t, sin_t, t, b)
        attn, attn_ctx = _attention(q, k, v, t, not last)
        hg = _hgrn(hg_in, lower[l], hg_norm_g[l], t, not last)
        fn, fn_ctx = _fourier(fn_in, w_fourier[l].astype(BF16), t, not last)
        if not last:
            attn, fn = (attn, attn_ctx), (fn, fn_ctx)
        xa = _outproj_ffn(xa, attn, hg, fn, mod_all[l], norm2_g[l], w_out[l].astype(BF16),
                          _pack_w_gate_up(w_gate_up[l], hidden), w_down[l].astype(BF16), final_norm_g, t, n_ctx, b,
                          not last, last)
    return xa
```

```python
import functools
import math

import numpy as np
import jax
import jax.numpy as jnp
from jax import lax
from jax.experimental import pallas as pl
from jax.experimental.pallas import tpu as pltpu

EPS = 1e-6
GRID_W = 64
ROPE_BASE = 10000.0

MLA_HEADS = 4
MLA_Q_LORA = 256
MLA_KV_LORA = 128
MLA_NOPE = 128
MLA_ROPE = 64
MLA_V = 128
MLA_WIDTH = MLA_HEADS * MLA_V
HEAD_PAD = 256

HG_HEADS = 4
HG_K = 128
HG_V = 64
HG_WIDTH = HG_HEADS * HG_V
HG_KW = HG_HEADS * HG_K
HG_CHUNK = 64
HG_LEVELS = 6
HG_MATMUL_LEVELS = 2
HG_UNROLL = 8

FN_GROUPS = 4
FN_GROUP_DIM = 64
FN_WIDTH = FN_GROUPS * FN_GROUP_DIM

O_CQ = 0
O_CKV = O_CQ + MLA_Q_LORA
O_KR = O_CKV + MLA_KV_LORA
O_HQ = O_KR + MLA_ROPE

P_CQ = 0
P_CKV = P_CQ + MLA_Q_LORA
P_KPA = P_CKV + MLA_KV_LORA
P_KPB = P_KPA + 128
P_HG = P_KPB + 128
HG_IN_WIDTH = 3 * HG_KW + 2 * HG_WIDTH
P_FN = P_HG + HG_IN_WIDTH
P_WIDTH = P_FN + FN_WIDTH

STREAM_TILE = 768
LATENT_TILE = 1024
ATTN_TILE = 512
FOURIER_TILE = 512
FFN_TILE = 256
VMEM_LIMIT = 56 * 1024 * 1024

BF16 = jnp.bfloat16
F32 = jnp.float32


def _dot(a, b):
    return jnp.dot(a, b, preferred_element_type=F32)


def _dot_nt(a, b):
    return lax.dot_general(a, b, (((1,), (1,)), ((), ())), preferred_element_type=F32)


def _dot_tn(a, b):
    return lax.dot_general(a, b, (((0,), (0,)), ((), ())), preferred_element_type=F32)


def _silu(x):
    return x * (1.0 / (1.0 + jnp.exp(-x)))


def _sigmoid(x):
    return 1.0 / (1.0 + jnp.exp(-x))


def _split3(x):
    hi = x.astype(BF16)
    r1 = x - hi.astype(F32)
    mid = r1.astype(BF16)
    lo = (r1 - mid.astype(F32)).astype(BF16)
    return hi, mid, lo


def _const_spec(shape):
    nd = len(shape)
    return pl.BlockSpec(shape, lambda *_: (0,) * nd)


_RESIDENT = pl.BlockSpec(memory_space=pltpu.VMEM)


def _params(n_axes):
    return pltpu.CompilerParams(dimension_semantics=("arbitrary",) * n_axes, vmem_limit_bytes=VMEM_LIMIT)


def _mod_kernel(c_ref, w_ref, b_ref, o_ref):
    act = _silu(c_ref[...]).astype(BF16)
    o_ref[0] = _dot(act, w_ref[0].astype(BF16)) + b_ref[0]


def _modulation(c_all, w_mod, b_mod):
    depth, d, n6 = w_mod.shape
    rows = c_all.shape[0]
    tn = n6 // 6
    return pl.pallas_call(
        _mod_kernel,
        grid=(depth, n6 // tn),
        in_specs=[
            pl.BlockSpec((rows, d), lambda l, j: (0, 0)),
            pl.BlockSpec((1, d, tn), lambda l, j: (l, 0, j)),
            pl.BlockSpec((1, 1, tn), lambda l, j: (l, 0, j)),
        ],
        out_specs=pl.BlockSpec((1, rows, tn), lambda l, j: (l, 0, j)),
        out_shape=jax.ShapeDtypeStruct((depth, rows, n6), F32),
        name="modulation",
    )(c_all, w_mod, b_mod.reshape(depth, 1, n6))


def _lower_bound_kernel(lb_ref, o_ref):
    depth = lb_ref.shape[0]
    lp = [lb_ref[l] for l in range(depth)]
    m = lp[0]
    for l in range(1, depth):
        m = jnp.maximum(m, lp[l])
    e = [jnp.exp(v - m) for v in lp]
    tot = e[0]
    for l in range(1, depth):
        tot = tot + e[l]
    probs = [v / tot for v in e]
    cum = probs[0]
    o_ref[0] = cum - probs[0]
    for l in range(1, depth):
        cum = cum + probs[l]
        o_ref[l] = cum - probs[0]


def _lower_bounds(lb_param):
    return pl.pallas_call(
        _lower_bound_kernel,
        out_shape=jax.ShapeDtypeStruct(lb_param.shape, F32),
        name="hgrn_lower_bounds",
    )(lb_param.astype(F32))


def _row_select(tile_rows, n_lat, ctx_vec, lat_vec):
    row = pl.program_id(1) * tile_rows + lax.broadcasted_iota(jnp.int32, (tile_rows, 1), 0)
    return jnp.where(row >= n_lat, ctx_vec, lat_vec)


def _stream_specs(src, tm):
    if not isinstance(src, tuple):
        return [pl.BlockSpec((1, tm, src.shape[2]), lambda i, j: (i, j, 0))], [src]
    lat, ctx = src
    t, d = lat.shape[1:]
    n_ctx = ctx.shape[1]
    n_full = t // tm
    tail = t - n_full * tm
    assert n_full >= 1 and tail > 0 and tail + n_ctx == tm and (n_full * tm) % tail == 0
    specs = [pl.BlockSpec((1, tm, d), lambda i, j: (i, jnp.minimum(j, n_full - 1), 0)),
             pl.BlockSpec((1, tail, d), lambda i, j: (i, (n_full * tm) // tail, 0)),
             pl.BlockSpec((1, n_ctx, d), lambda i, j: (i, 0, 0))]
    return specs, [lat, lat, ctx]


def _stream_tile(refs):
    if len(refs) == 1:
        return refs[0][0]
    main_ref, tail_ref, ctx_ref = refs
    mixed = jnp.concatenate([tail_ref[0], ctx_ref[0]], axis=0)
    return jnp.where(pl.program_id(1) < pl.num_programs(1) - 1, main_ref[0], mixed)


def _inproj_kernel(*refs, q_scale, n_lat, n_src):
    (modb_ref, modc_ref, g1_ref, w_in_ref, qg_ref, wq_ref, kvg_ref, wk_ref, wvt_ref, cos_ref, sin_ref,
     q_ref, k_ref, vt_ref, hg_ref, fn_ref) = refs[n_src:]
    x = _stream_tile(refs[:n_src])
    tm = x.shape[0]
    shift = _row_select(tm, n_lat, modc_ref[0, 0:1, :], modb_ref[0, 0:1, :])
    scale = _row_select(tm, n_lat, modc_ref[0, 1:2, :], modb_ref[0, 1:2, :])
    xn = x * lax.rsqrt(jnp.mean(x * x, axis=-1, keepdims=True) + EPS) * g1_ref[...]
    xm = (xn * (1.0 + scale) + shift).astype(BF16)

    hg_ref[0] = _dot(xm, w_in_ref[:, P_HG:P_FN]).astype(BF16)
    fn_ref[0] = _dot(xm, w_in_ref[:, P_FN:P_WIDTH]).astype(BF16)
    p = _dot(xm, w_in_ref[:, 0:P_HG])

    cos = cos_ref[...]
    sin = sin_ref[...]

    cq = p[:, P_CQ:P_CKV]
    cqn = (cq * lax.rsqrt(jnp.mean(cq * cq, axis=-1, keepdims=True) + EPS) * qg_ref[...]).astype(BF16)
    qq = _dot(cqn, wq_ref[...])
    nw = MLA_HEADS * MLA_NOPE
    for h in range(MLA_HEADS):
        q_ref[0, :, h * HEAD_PAD:h * HEAD_PAD + 128] = (qq[:, h * 128:(h + 1) * 128] * q_scale).astype(BF16)
        pe = qq[:, nw + h * 128:nw + (h + 1) * 128] * cos + qq[:, 2 * nw + h * 128:2 * nw + (h + 1) * 128] * sin
        q_ref[0, :, h * HEAD_PAD + 128:(h + 1) * HEAD_PAD] = (pe * q_scale).astype(BF16)

    ckv = p[:, P_CKV:P_KPA]
    ckvn = (ckv * lax.rsqrt(jnp.mean(ckv * ckv, axis=-1, keepdims=True) + EPS) * kvg_ref[...]).astype(BF16)
    kv = _dot(ckvn, wk_ref[...])
    vt_ref[0] = _dot_nt(wvt_ref[...], ckvn).astype(BF16)
    kpe = (p[:, P_KPA:P_KPB] * cos + p[:, P_KPB:P_HG] * sin).astype(BF16)
    for h in range(MLA_HEADS):
        k_ref[0, :, h * HEAD_PAD:h * HEAD_PAD + 128] = kv[:, h * 128:(h + 1) * 128].astype(BF16)
        k_ref[0, :, h * HEAD_PAD + 128:(h + 1) * HEAD_PAD] = kpe


def _inproj(src, mod, g1, w_in_p, qg, wq_all, kvg, wk_p, wvt_p, cos_t, sin_t, n_lat, ctx_row):
    s = cos_t.shape[0]
    d = g1.shape[0]
    b = (src[0] if isinstance(src, tuple) else src).shape[0]
    tm = STREAM_TILE
    row = lambda w: pl.BlockSpec((1, tm, w), lambda i, j: (i, j, 0))
    src_specs, src_args = _stream_specs(src, tm)
    return pl.pallas_call(
        functools.partial(_inproj_kernel, q_scale=math.log2(math.e) / math.sqrt(MLA_NOPE + MLA_ROPE), n_lat=n_lat,
                          n_src=len(src_args)),
        grid=(b, s // tm),
        in_specs=src_specs + [
            pl.BlockSpec((1, 6, d), lambda i, j: (i, 0, 0)),
            pl.BlockSpec((1, 6, d), lambda i, j: (ctx_row, 0, 0)),
            _const_spec((1, d)),
            _RESIDENT,
            _const_spec((1, MLA_Q_LORA)),
            _RESIDENT,
            _const_spec((1, MLA_KV_LORA)),
            _RESIDENT,
            _RESIDENT,
            pl.BlockSpec((tm, 128), lambda i, j: (j, 0)),
            pl.BlockSpec((tm, 128), lambda i, j: (j, 0)),
        ],
        out_specs=[row(MLA_HEADS * HEAD_PAD), row(MLA_HEADS * HEAD_PAD),
                   pl.BlockSpec((1, MLA_WIDTH, tm), lambda i, j: (i, 0, j)), row(HG_IN_WIDTH), row(FN_WIDTH)],
        out_shape=[
            jax.ShapeDtypeStruct((b, s, MLA_HEADS * HEAD_PAD), BF16),
            jax.ShapeDtypeStruct((b, s, MLA_HEADS * HEAD_PAD), BF16),
            jax.ShapeDtypeStruct((b, MLA_WIDTH, s), BF16),
            jax.ShapeDtypeStruct((b, s, HG_IN_WIDTH), BF16),
            jax.ShapeDtypeStruct((b, s, FN_WIDTH), BF16),
        ],
        compiler_params=_params(2),
        name="inproj",
    )(*src_args, mod, mod, g1.reshape(1, d), w_in_p, qg.reshape(1, -1), wq_all, kvg.reshape(1, -1), wk_p, wvt_p,
      cos_t, sin_t)


def _attend(q_ref, k_ref, vt_ref, o_ref, key_lo, key_hi):
    def scores(h):
        qh = q_ref[0, :, h * HEAD_PAD:(h + 1) * HEAD_PAD]
        kh = k_ref[0, key_lo:key_hi, h * HEAD_PAD:(h + 1) * HEAD_PAD]
        return _dot_nt(kh, qh)

    all_scores = [scores(h) for h in range(MLA_HEADS)]
    probs = []
    for st in all_scores:
        e = jnp.exp2(st - jnp.max(st, axis=0, keepdims=True))
        probs.append((e.astype(BF16), 1.0 / jnp.sum(e, axis=0, keepdims=True)))
    for h, (e, inv_denom) in enumerate(probs):
        vth = vt_ref[0, h * MLA_V:(h + 1) * MLA_V, key_lo:key_hi]
        ot = _dot(vth, e) * inv_denom
        o_ref[0, :, h * MLA_V:(h + 1) * MLA_V] = ot.T.astype(BF16)


def _attn_kernel(q_ref, k_ref, vt_ref, o_ref):
    _attend(q_ref, k_ref, vt_ref, o_ref, 0, k_ref.shape[1])


def _attn_ctx_kernel(q_ref, qc_ref, k_ref, vt_ref, o_ref, oc_ref, *, n_lat):
    s_len = k_ref.shape[1]
    _attend(q_ref, k_ref, vt_ref, o_ref, 0, s_len)

    @pl.when(pl.program_id(1) == pl.num_programs(1) - 1)
    def _():
        _attend(qc_ref, k_ref, vt_ref, oc_ref, n_lat, s_len)


def _attention(q, k, vt, n_lat, with_ctx_queries):
    b, s, _ = q.shape
    tq = ATTN_TILE
    n_ctx = s - n_lat
    q_spec = pl.BlockSpec((1, tq, q.shape[2]), lambda i, j: (i, j, 0))
    kv_specs = [pl.BlockSpec((1, s, k.shape[2]), lambda i, j: (i, 0, 0)),
                pl.BlockSpec((1, vt.shape[1], s), lambda i, j: (i, 0, 0))]
    o_spec = pl.BlockSpec((1, tq, MLA_WIDTH), lambda i, j: (i, j, 0))
    o_shape = jax.ShapeDtypeStruct((b, n_lat, MLA_WIDTH), BF16)
    if not with_ctx_queries:
        out = pl.pallas_call(
            _attn_kernel,
            grid=(b, n_lat // tq),
            in_specs=[q_spec] + kv_specs,
            out_specs=o_spec,
            out_shape=o_shape,
            compiler_params=_params(2),
            name="attention",
        )(q, k, vt)
        return out, None
    assert n_lat % n_ctx == 0
    return pl.pallas_call(
        functools.partial(_attn_ctx_kernel, n_lat=n_lat),
        grid=(b, n_lat // tq),
        in_specs=[q_spec, pl.BlockSpec((1, n_ctx, q.shape[2]), lambda i, j: (i, n_lat // n_ctx, 0))] + kv_specs,
        out_specs=[o_spec, pl.BlockSpec((1, n_ctx, MLA_WIDTH), lambda i, j: (i, 0, 0))],
        out_shape=[o_shape, jax.ShapeDtypeStruct((b, n_ctx, MLA_WIDTH), BF16)],
        compiler_params=_params(2),
        name="attention_ctx",
    )(q, q, k, vt)


def _hgrn_constants():
    c = HG_CHUNK
    t = np.arange(c)[:, None]
    u = np.arange(c)[None, :]
    blocks = [(u <= t)]
    for m in (8, 4):
        half = m // 2
        mid = (t // m) * m + half
        lower = (t % m) >= half
        blocks.append(np.where(lower, (u >= mid) & (u <= t), (u >= t + 1) & (u <= mid - 1)))
    mf = np.stack([blk.astype(np.float32) for blk in blocks])
    mb = np.ascontiguousarray(mf[:, ::-1, ::-1])
    msb = np.floor(np.log2(np.maximum(t ^ u, 1))).astype(np.int32)
    lvl = np.where(t > u, HG_LEVELS - 1 - msb, np.where(t == u, HG_LEVELS, -1)).astype(np.int32)
    lvl_f = np.tile(lvl, (1, 2))
    lvl_b = np.tile(np.ascontiguousarray(lvl.T), (1, 2))
    return mf.reshape(-1, c), mb.reshape(-1, c), lvl_f, lvl_b


def _pair_weights(x, p):
    zeros = jnp.zeros((x.shape[0], HG_K), x.dtype)
    xa = x[:, 2 * p * HG_K:(2 * p + 1) * HG_K]
    xb = x[:, (2 * p + 1) * HG_K:(2 * p + 2) * HG_K]
    blk = jnp.concatenate([jnp.concatenate([xa, zeros], axis=1), jnp.concatenate([zeros, xb], axis=1)], axis=0)
    return blk.T


def _hgrn_gates(z, lb, m_ref, with_output):
    f = lb + (1.0 - lb) * _sigmoid(z)
    g = jnp.log2(f)
    g_hi = g.astype(BF16)
    g_lo = (g - g_hi.astype(F32)).astype(BF16)
    m_all = m_ref[...] if with_output else m_ref[0:HG_CHUNK, :]
    return f, _dot(m_all, g_hi) + _dot(m_all, g_lo)


def _hgrn_decays(hq, f, eb, fwd, with_output):
    c = HG_CHUNK
    kk = 1.0 - f
    bc = eb[0:c]
    total = bc[c - 1:c] if fwd else bc[0:1]
    k_up = (kk * jnp.exp2(total - bc)).astype(BF16)
    chunk_decay = jnp.exp2(total)
    if not with_output:
        return k_up, chunk_decay, None
    qq = _silu(hq)
    row = lax.broadcasted_iota(jnp.int32, (c, HG_KW), 0)
    xs = []
    for l in range(HG_LEVELS):
        m = c >> l
        half = m // 2
        bit = (row & half) != 0
        is_query = bit if fwd else jnp.logical_not(bit)
        if m == 2:
            xs.append(jnp.where(is_query, qq * f, kk).astype(BF16))
            continue
        if m >= 16:
            refs = [bc[b0 + half - 1:b0 + half] if fwd else bc[b0 + half:b0 + half + 1] for b0 in range(0, c, m)]
            ref = jnp.concatenate([jnp.broadcast_to(r, (m, HG_KW)) for r in refs], axis=0)
            diff = bc - ref
            expo = jnp.where(is_query, diff, -diff)
        else:
            idx = 1 + l - (HG_LEVELS - 1 - HG_MATMUL_LEVELS)
            expo = eb[idx * c:(idx + 1) * c]
        xs.append((jnp.where(is_query, qq, kk) * jnp.exp2(expo)).astype(BF16))
    q_in = (qq * jnp.exp2(bc)).astype(BF16)
    return k_up, chunk_decay, (xs, qq.astype(BF16), kk.astype(BF16), q_in)


def _hgrn_scores(operands, lvl_masks):
    xs, qb, kb, _ = operands
    scores = []
    for p in range(HG_HEADS // 2):
        kp = slice(2 * p * HG_K, (2 * p + 2) * HG_K)
        sc = jnp.where(lvl_masks[HG_LEVELS], _dot(qb[:, kp], _pair_weights(kb, p)), 0.0)
        for l in range(HG_LEVELS):
            sc = jnp.where(lvl_masks[l], _dot(xs[l][:, kp], _pair_weights(xs[l], p)), sc)
        scores.append(sc.astype(BF16))
    return scores


def _hgrn_apply(scores, operands, k_up, chunk_decay, vv, v_lo, v_hi, state_ref):
    outs = []
    for p in range(HG_HEADS // 2):
        kp = slice(2 * p * HG_K, (2 * p + 2) * HG_K)
        vp = slice(2 * p * HG_V, (2 * p + 2) * HG_V)
        if scores is not None:
            v_pair = vv[:, vp]
            v_diag = jnp.concatenate([v_pair * v_lo, v_pair * v_hi], axis=0)
            st_pair = state_ref[2 * p * HG_V:(2 * p + 2) * HG_V, kp]
            outs.append(_dot(scores[p], v_diag) + _dot(operands[3][:, kp], st_pair.astype(BF16).T))
        for h in (2 * p, 2 * p + 1):
            ks = slice(h * HG_K, (h + 1) * HG_K)
            vs = slice(h * HG_V, (h + 1) * HG_V)
            state_ref[vs, ks] = chunk_decay[:, ks] * state_ref[vs, ks] + _dot_tn(vv[:, vs], k_up[:, ks])
    return jnp.concatenate(outs, axis=1) if outs else None


def _hgrn_kernel(x_ref, lb_ref, gain_ref, mf_ref, mb_ref, lvlf_ref, lvlb_ref, ones_ref, o_ref,
                 of_ref, ob_ref, sf_ref, sb_ref, *, n_lat_chunks):
    c = HG_CHUNK
    s_len = x_ref.shape[1]
    n_chunks = s_len // c
    n_ctx_chunks = n_chunks - n_lat_chunks
    sf_ref[...] = jnp.zeros_like(sf_ref)
    sb_ref[...] = jnp.zeros_like(sb_ref)
    lvl_f = lvlf_ref[...]
    lvl_b = lvlb_ref[...]
    masks_f = [lvl_f == l for l in range(HG_LEVELS + 1)]
    masks_b = [lvl_b == l for l in range(HG_LEVELS + 1)]
    lb_f = lb_ref[0:1, :]
    lb_b = lb_ref[1:2, :]
    v_lane = lax.broadcasted_iota(jnp.int32, (c, 2 * HG_V), 1)
    v_lo = (v_lane < HG_V).astype(F32).astype(BF16)
    v_hi = (v_lane >= HG_V).astype(F32).astype(BF16)

    def load(ci):
        r0 = pl.multiple_of(ci * c, c)
        hq = x_ref[0, pl.ds(r0, c), 0:HG_KW].astype(F32)
        vv = x_ref[0, pl.ds(r0, c), 3 * HG_KW:3 * HG_KW + HG_WIDTH]
        return r0, hq, vv

    def scan(first_fwd, first_bwd, n, with_output):
        per_trip = HG_UNROLL if n % HG_UNROLL == 0 else n

        def step(i, carry):
            gated = []
            for u in range(per_trip):
                for fwd in (True, False):
                    ci = first_fwd + i * per_trip + u if fwd else first_bwd - i * per_trip - u
                    r0, hq, vv = load(ci)
                    zcol = HG_KW if fwd else 2 * HG_KW
                    z = x_ref[0, pl.ds(r0, c), zcol:zcol + HG_KW].astype(F32)
                    gated.append((fwd, r0, hq, vv, _hgrn_gates(z, lb_f if fwd else lb_b, mf_ref if fwd else mb_ref,
                                                               with_output)))
            jobs = [(fwd, r0, vv, _hgrn_decays(hq, f, eb, fwd, with_output)) for fwd, r0, hq, vv, (f, eb) in gated]
            scores = [_hgrn_scores(ph[2], masks_f if fwd else masks_b) if with_output else None
                      for fwd, _, _, ph in jobs]
            for (fwd, r0, vv, (k_up, chunk_decay, operands)), sc in zip(jobs, scores):
                out = _hgrn_apply(sc, operands, k_up, chunk_decay, vv, v_lo, v_hi, sf_ref if fwd else sb_ref)
                if with_output:
                    (of_ref if fwd else ob_ref)[pl.ds(r0, c), :] = out
            return carry

        lax.fori_loop(0, n // per_trip, step, 0)

    scan(n_lat_chunks, n_chunks - 1, n_ctx_chunks, o_ref.shape[1] == s_len)
    scan(0, n_lat_chunks - 1, n_lat_chunks, True)

    rt = 256
    gain = gain_ref[...]
    ones = ones_ref[...]

    def readout(i, carry):
        r0 = pl.multiple_of(i * rt, rt)
        o = of_ref[pl.ds(r0, rt), :] + ob_ref[pl.ds(r0, rt), :]
        sq_hi, sq_mid, sq_lo = _split3(o * o)
        msq = _dot(sq_hi, ones) + _dot(sq_mid, ones) + _dot(sq_lo, ones)
        zg = x_ref[0, pl.ds(r0, rt), 3 * HG_KW + HG_WIDTH:HG_IN_WIDTH].astype(F32)
        y = o * lax.rsqrt(msq + EPS) * gain * _silu(zg)
        o_ref[0, pl.ds(r0, rt), :] = y.astype(BF16)
        return carry

    lax.fori_loop(0, o_ref.shape[1] // rt, readout, 0)


def _hgrn(hg_in, lb, gain, n_lat, with_ctx_outputs):
    b, s, w = hg_in.shape
    mf, mb, lvl_f, lvl_b = _hgrn_constants()
    ones = np.kron(np.eye(HG_HEADS, dtype=np.float32), np.full((HG_V, HG_V), 1.0 / HG_V, np.float32))
    n_out = s if with_ctx_outputs else n_lat
    return pl.pallas_call(
        functools.partial(_hgrn_kernel, n_lat_chunks=n_lat // HG_CHUNK),
        grid=(b,),
        in_specs=[
            pl.BlockSpec((1, s, w), lambda i: (i, 0, 0)),
            _const_spec((2, HG_KW)),
            _const_spec((1, HG_WIDTH)),
            _const_spec(mf.shape),
            _const_spec(mb.shape),
            _const_spec(lvl_f.shape),
            _const_spec(lvl_b.shape),
            _const_spec(ones.shape),
        ],
        out_specs=pl.BlockSpec((1, n_out, HG_WIDTH), lambda i: (i, 0, 0)),
        out_shape=jax.ShapeDtypeStruct((b, n_out, HG_WIDTH), BF16),
        scratch_shapes=[
            pltpu.VMEM((s, HG_WIDTH), F32),
            pltpu.VMEM((s, HG_WIDTH), F32),
            pltpu.VMEM((HG_WIDTH, HG_KW), F32),
            pltpu.VMEM((HG_WIDTH, HG_KW), F32),
        ],
        compiler_params=_params(1),
        name="hgrn2",
    )(hg_in, lb, gain.reshape(1, HG_WIDTH), jnp.asarray(mf, BF16), jnp.asarray(mb, BF16),
      jnp.asarray(lvl_f), jnp.asarray(lvl_b), jnp.asarray(ones, BF16))


def _dft_tables(n):
    idx = np.arange(n, dtype=np.int64)
    ang = 2.0 * np.pi * ((idx[:, None] * idx[None, :]) % n).astype(np.float64) / n
    return np.cos(ang).astype(np.float32), np.sin(ang).astype(np.float32)


def _fourier_tile(z, table, mix, w, norm):
    tn = table.shape[0] // 2
    zz = _dot(table, z)
    zcat = jnp.concatenate([zz[0:tn], zz[tn:2 * tn]], axis=1).astype(BF16)
    mixed = _dot(zcat, mix) * norm
    return _dot(mixed.astype(BF16), w).astype(BF16)


def _fourier_kernel(z_ref, tl_ref, tc_ref, mix_ref, w_ref, o_ref, *maybe_oc_ref, n_lat):
    s_len = z_ref.shape[1]
    n_ctx = s_len - n_lat
    mix = mix_ref[...]
    w = w_ref[...]
    table = tl_ref[pl.program_id(1)]
    o_ref[0] = _fourier_tile(z_ref[0, 0:n_lat, :], table, mix, w, 1.0 / math.sqrt(n_lat * FN_GROUP_DIM))
    if maybe_oc_ref:
        @pl.when(pl.program_id(1) == pl.num_programs(1) - 1)
        def _():
            maybe_oc_ref[0][0] = _fourier_tile(z_ref[0, n_lat:s_len, :], tc_ref[...], mix, w,
                                               1.0 / math.sqrt(n_ctx * FN_GROUP_DIM))


def _stacked_dft_table(n, tile):
    cos, sin = _dft_tables(n)
    return np.concatenate([cos.reshape(n // tile, tile, n), sin.reshape(n // tile, tile, n)], axis=1)


def _fourier(fn_in, w_fourier, n_lat, with_ctx_outputs):
    b, s, w = fn_in.shape
    n_ctx = s - n_lat
    tn = FOURIER_TILE
    table_lat = _stacked_dft_table(n_lat, tn)
    table_ctx = _stacked_dft_table(n_ctx, n_ctx)[0]
    cgrp, sgrp = _dft_tables(FN_GROUP_DIM)
    eye = np.eye(FN_GROUPS, dtype=np.float32)
    mix = np.concatenate([np.kron(eye, cgrp), -np.kron(eye, sgrp)], axis=0)
    out_specs = [pl.BlockSpec((1, tn, FN_WIDTH), lambda i, j: (i, j, 0))]
    out_shape = [jax.ShapeDtypeStruct((b, n_lat, FN_WIDTH), BF16)]
    if with_ctx_outputs:
        out_specs.append(pl.BlockSpec((1, n_ctx, FN_WIDTH), lambda i, j: (i, 0, 0)))
        out_shape.append(jax.ShapeDtypeStruct((b, n_ctx, FN_WIDTH), BF16))
    outs = pl.pallas_call(
        functools.partial(_fourier_kernel, n_lat=n_lat),
        grid=(b, n_lat // tn),
        in_specs=[
            pl.BlockSpec((1, s, w), lambda i, j: (i, 0, 0)),
            _RESIDENT,
            _const_spec(table_ctx.shape),
            _const_spec(mix.shape),
            _const_spec(w_fourier.shape),
        ],
        out_specs=out_specs,
        out_shape=out_shape,
        compiler_params=_params(2),
        name="fourier",
    )(fn_in, jnp.asarray(table_lat, BF16), jnp.asarray(table_ctx, BF16), jnp.asarray(mix, BF16), w_fourier)
    return (outs[0], outs[1]) if with_ctx_outputs else (outs[0], None)


def _ffn_kernel(*refs, final_norm, n_lat, select_ctx, n_srcs):
    streams = []
    pos = 0
    for n in n_srcs:
        streams.append(_stream_tile(refs[pos:pos + n]))
        pos += n
    x, attn, hg, fn = streams
    modb_ref, modc_ref, g2_ref, wo_ref, wgu_ref, wd_ref, gf_ref, o_ref, act_ref = refs[pos:]
    tm = x.shape[0]

    def mod_vec(r):
        if select_ctx:
            return _row_select(tm, n_lat, modc_ref[0, r:r + 1, :], modb_ref[0, r:r + 1, :])
        return modb_ref[0, r:r + 1, :]

    y = (_dot(attn, wo_ref[0:MLA_WIDTH, :])
         + _dot(hg, wo_ref[MLA_WIDTH:MLA_WIDTH + HG_WIDTH, :])
         + _dot(fn, wo_ref[MLA_WIDTH + HG_WIDTH:MLA_WIDTH + HG_WIDTH + FN_WIDTH, :]))
    x1 = x + mod_vec(2) * y
    o_ref[0] = x1
    xn = x1 * lax.rsqrt(jnp.mean(x1 * x1, axis=-1, keepdims=True) + EPS) * g2_ref[...]
    hm = (xn * (1.0 + mod_vec(4)) + mod_vec(3)).astype(BF16)

    ft = wgu_ref.shape[2] // 2
    for t in range(wgu_ref.shape[0]):
        gu = _dot(hm, wgu_ref[t])
        act_ref[:, t * ft:(t + 1) * ft] = (_silu(gu[:, 0:ft]) * gu[:, ft:2 * ft]).astype(BF16)
    out = o_ref[0] + mod_vec(5) * _dot(act_ref[...], wd_ref[...])
    if final_norm:
        out = out * lax.rsqrt(jnp.mean(out * out, axis=-1, keepdims=True) + EPS) * gf_ref[...]
    o_ref[0] = out


def _outproj_ffn(x_src, attn, hg, fn, mod, g2, w_out, w_gu, w_d, g_final, n_lat, n_ctx, ctx_row, with_ctx_rows,
                 final_norm):
    d = g2.shape[0]
    b = (x_src[0] if isinstance(x_src, tuple) else x_src).shape[0]
    tm = STREAM_TILE if with_ctx_rows else LATENT_TILE
    n_rows = n_lat + n_ctx if with_ctx_rows else n_lat
    hidden = w_d.shape[0]
    row = lambda w: pl.BlockSpec((1, tm, w), lambda i, j: (i, j, 0))
    src_specs, src_args, n_srcs = [], [], []
    for src in (x_src, attn, hg, fn):
        specs, args = _stream_specs(src, tm)
        src_specs += specs
        src_args += args
        n_srcs.append(len(args))
    return pl.pallas_call(
        functools.partial(_ffn_kernel, final_norm=final_norm, n_lat=n_lat, select_ctx=with_ctx_rows,
                          n_srcs=tuple(n_srcs)),
        grid=(b, n_rows // tm),
        in_specs=src_specs + [
            pl.BlockSpec((1, 6, d), lambda i, j: (i, 0, 0)),
            pl.BlockSpec((1, 6, d), lambda i, j: (ctx_row, 0, 0)),
            _const_spec((1, d)),
            _RESIDENT,
            _RESIDENT,
            _RESIDENT,
            _const_spec((1, d)),
        ],
        out_specs=row(d),
        out_shape=jax.ShapeDtypeStruct((b, n_rows, d), F32),
        scratch_shapes=[pltpu.VMEM((tm, hidden), BF16)],
        compiler_params=_params(2),
        name="outproj_ffn",
    )(*src_args, mod, mod, g2.reshape(1, d), w_out, w_gu, w_d, g_final.reshape(1, d))


def _rotate_half_cols(w):
    r1, r2, c1, c2 = jnp.split(w, 4, axis=-1)
    return jnp.concatenate([-r2, r1, -c2, c1], axis=-1)


def _pack_w_in(w):
    d = w.shape[0]
    zeros = jnp.zeros((d, 128 - MLA_ROPE), w.dtype)
    w_kr = w[:, O_KR:O_HQ]
    return jnp.concatenate([w[:, O_CQ:O_KR], w_kr, zeros, _rotate_half_cols(w_kr), zeros, w[:, O_HQ:]],
                           axis=1).astype(BF16)


def _pack_w_uq(w):
    r = w.shape[0]
    wh = w.reshape(r, MLA_HEADS, MLA_NOPE + MLA_ROPE)
    nope = wh[:, :, :MLA_NOPE].reshape(r, MLA_HEADS * MLA_NOPE)
    pe = wh[:, :, MLA_NOPE:]
    zeros = jnp.zeros((r, MLA_HEADS, 128 - MLA_ROPE), w.dtype)
    pe_a = jnp.concatenate([pe, zeros], axis=-1).reshape(r, MLA_HEADS * 128)
    pe_b = jnp.concatenate([_rotate_half_cols(pe), zeros], axis=-1).reshape(r, MLA_HEADS * 128)
    return jnp.concatenate([nope, pe_a, pe_b], axis=1).astype(BF16)


def _pack_w_ukv(w):
    r = w.shape[0]
    wh = w.reshape(r, MLA_HEADS, MLA_NOPE + MLA_V)
    k_nope = wh[:, :, :MLA_NOPE].reshape(r, MLA_HEADS * MLA_NOPE)
    v = wh[:, :, MLA_NOPE:].reshape(r, MLA_HEADS * MLA_V)
    return k_nope.astype(BF16), v.T.astype(BF16)


def _pack_w_gate_up(w, hidden):
    d = w.shape[0]
    n_ft = hidden // FFN_TILE
    gate = w[:, :hidden].reshape(d, n_ft, FFN_TILE)
    up = w[:, hidden:].reshape(d, n_ft, FFN_TILE)
    return jnp.concatenate([gate, up], axis=2).transpose(1, 0, 2).astype(BF16)


def _rope_tables(n_lat, n_ctx):
    rows = n_lat // GRID_W
    row_pos = np.repeat(np.arange(rows, dtype=np.float32), GRID_W)
    col_pos = np.tile(np.arange(GRID_W, dtype=np.float32), rows)
    axis_dim = MLA_ROPE // 2
    inv_freq = (ROPE_BASE ** (-np.arange(0, axis_dim, 2, dtype=np.float32) / axis_dim)).astype(np.float32)
    ang_r = row_pos[:, None] * inv_freq
    ang_c = col_pos[:, None] * inv_freq
    ang = np.concatenate([ang_r, ang_r, ang_c, ang_c], axis=-1)
    cos = np.ones((n_lat + n_ctx, 128), np.float32)
    sin = np.zeros((n_lat + n_ctx, 128), np.float32)
    cos[:n_lat, :MLA_ROPE] = np.cos(ang)
    sin[:n_lat, :MLA_ROPE] = np.sin(ang)
    return jnp.asarray(cos), jnp.asarray(sin)


def kernel(x, c, ctx, c_ctx, w_mod, b_mod, norm1_g, norm2_g, w_in, q_norm_g, w_uq, kv_norm_g, w_ukv, lb_param,
           hg_norm_g, w_fourier, w_out, w_gate_up, w_down, final_norm_g):
    b, t, d = x.shape
    n_ctx = ctx.shape[1]
    s = t + n_ctx
    depth = w_mod.shape[0]
    hidden = w_down.shape[1]
    assert s % STREAM_TILE == 0 and t % LATENT_TILE == 0 and t % ATTN_TILE == 0 and t % FOURIER_TILE == 0
    assert hidden % FFN_TILE == 0 and t % GRID_W == 0 and n_ctx % HG_CHUNK == 0

    mod_rows = -(-(b + 1) // 8) * 8
    c_all = jnp.zeros((mod_rows, d), F32).at[:b].set(c).at[b].set(c_ctx)
    mod_all = _modulation(c_all, w_mod, b_mod).reshape(depth, mod_rows, 6, d)
    lower = _lower_bounds(lb_param)
    cos_t, sin_t = _rope_tables(t, n_ctx)

    xa = (x, ctx)
    for l in range(depth):
        last = l == depth - 1
        q, k, v, hg_in, fn_in = _inproj(xa, mod_all[l], norm1_g[l], _pack_w_in(w_in[l]), q_norm_g[l],
                                        _pack_w_uq(w_uq[l]), kv_norm_g[l], *_pack_w_ukv(w_ukv[l]), cos_t, sin_t, t, b)
        attn, attn_ctx = _attention(q, k, v, t, not last)
        hg = _hgrn(hg_in, lower[l], hg_norm_g[l], t, not last)
        fn, fn_ctx = _fourier(fn_in, w_fourier[l].astype(BF16), t, not last)
        if not last:
            attn, fn = (attn, attn_ctx), (fn, fn_ctx)
        xa = _outproj_ffn(xa, attn, hg, fn, mod_all[l], norm2_g[l], w_out[l].astype(BF16),
                          _pack_w_gate_up(w_gate_up[l], hidden), w_down[l].astype(BF16), final_norm_g, t, n_ctx, b,
                          not last, last)
    return xa
```

```python
import functools
import math

import numpy as np
import jax
import jax.numpy as jnp
from jax import lax
from jax.experimental import pallas as pl
from jax.experimental.pallas import tpu as pltpu

EPS = 1e-6
GRID_W = 64
ROPE_BASE = 10000.0

MLA_HEADS = 4
MLA_Q_LORA = 256
MLA_KV_LORA = 128
MLA_NOPE = 128
MLA_ROPE = 64
MLA_V = 128
MLA_WIDTH = MLA_HEADS * MLA_V
HEAD_PAD = 256

HG_HEADS = 4
HG_K = 128
HG_V = 64
HG_WIDTH = HG_HEADS * HG_V
HG_KW = HG_HEADS * HG_K
HG_CHUNK = 64
HG_LEVELS = 6
HG_MATMUL_LEVELS = 2
HG_UNROLL = 8

FN_GROUPS = 4
FN_GROUP_DIM = 64
FN_WIDTH = FN_GROUPS * FN_GROUP_DIM

O_CQ = 0
O_CKV = O_CQ + MLA_Q_LORA
O_KR = O_CKV + MLA_KV_LORA
O_HQ = O_KR + MLA_ROPE

P_CQ = 0
P_CKV = P_CQ + MLA_Q_LORA
P_KPA = P_CKV + MLA_KV_LORA
P_KPB = P_KPA + 128
P_HG = P_KPB + 128
HG_IN_WIDTH = 3 * HG_KW + 2 * HG_WIDTH
P_FN = P_HG + HG_IN_WIDTH
P_WIDTH = P_FN + FN_WIDTH

STREAM_TILE = 768
LATENT_TILE = 1024
ATTN_TILE = 512
FOURIER_TILE = 1024
FFN_TILE = 256
VMEM_LIMIT = 56 * 1024 * 1024

BF16 = jnp.bfloat16
F32 = jnp.float32


def _dot(a, b):
    return jnp.dot(a, b, preferred_element_type=F32)


def _dot_nt(a, b):
    return lax.dot_general(a, b, (((1,), (1,)), ((), ())), preferred_element_type=F32)


def _dot_tn(a, b):
    return lax.dot_general(a, b, (((0,), (0,)), ((), ())), preferred_element_type=F32)


def _silu(x):
    return x * (1.0 / (1.0 + jnp.exp(-x)))


def _sigmoid(x):
    return 1.0 / (1.0 + jnp.exp(-x))


def _split3(x):
    hi = x.astype(BF16)
    r1 = x - hi.astype(F32)
    mid = r1.astype(BF16)
    lo = (r1 - mid.astype(F32)).astype(BF16)
    return hi, mid, lo


def _const_spec(shape):
    nd = len(shape)
    return pl.BlockSpec(shape, lambda *_: (0,) * nd)


_RESIDENT = pl.BlockSpec(memory_space=pltpu.VMEM)


def _params(n_axes):
    return pltpu.CompilerParams(dimension_semantics=("arbitrary",) * n_axes, vmem_limit_bytes=VMEM_LIMIT)


def _mod_kernel(c_ref, w_ref, b_ref, o_ref):
    act = _silu(c_ref[...]).astype(BF16)
    o_ref[0] = _dot(act, w_ref[0].astype(BF16)) + b_ref[0]


def _modulation(c_all, w_mod, b_mod):
    depth, d, n6 = w_mod.shape
    rows = c_all.shape[0]
    tn = n6 // 6
    return pl.pallas_call(
        _mod_kernel,
        grid=(depth, n6 // tn),
        in_specs=[
            pl.BlockSpec((rows, d), lambda l, j: (0, 0)),
            pl.BlockSpec((1, d, tn), lambda l, j: (l, 0, j)),
            pl.BlockSpec((1, 1, tn), lambda l, j: (l, 0, j)),
        ],
        out_specs=pl.BlockSpec((1, rows, tn), lambda l, j: (l, 0, j)),
        out_shape=jax.ShapeDtypeStruct((depth, rows, n6), F32),
        name="modulation",
    )(c_all, w_mod, b_mod.reshape(depth, 1, n6))


def _lower_bound_kernel(lb_ref, o_ref):
    depth = lb_ref.shape[0]
    lp = [lb_ref[l] for l in range(depth)]
    m = lp[0]
    for l in range(1, depth):
        m = jnp.maximum(m, lp[l])
    e = [jnp.exp(v - m) for v in lp]
    tot = e[0]
    for l in range(1, depth):
        tot = tot + e[l]
    probs = [v / tot for v in e]
    cum = probs[0]
    o_ref[0] = cum - probs[0]
    for l in range(1, depth):
        cum = cum + probs[l]
        o_ref[l] = cum - probs[0]


def _lower_bounds(lb_param):
    return pl.pallas_call(
        _lower_bound_kernel,
        out_shape=jax.ShapeDtypeStruct(lb_param.shape, F32),
        name="hgrn_lower_bounds",
    )(lb_param.astype(F32))


def _modulated(tile_rows, n_lat, modc_ref, modb_ref, fn, *arrays, use_ctx=True):
    lat = modb_ref[0]
    if not use_ctx:
        return fn(lat, *arrays)
    tail = n_lat % tile_rows
    if tail == 0:
        return fn(jnp.where(pl.program_id(1) * tile_rows >= n_lat, modc_ref[0], lat), *arrays)
    is_last = pl.program_id(1) == pl.num_programs(1) - 1
    mixed = jnp.where(is_last, modc_ref[0], lat)
    return jnp.concatenate([fn(lat, *[a[0:tail] for a in arrays]),
                            fn(mixed, *[a[tail:tile_rows] for a in arrays])], axis=0)


def _stream_specs(src, tm):
    if not isinstance(src, tuple):
        return [pl.BlockSpec((1, tm, src.shape[2]), lambda i, j: (i, j, 0))], [src]
    lat, ctx = src
    t, d = lat.shape[1:]
    n_ctx = ctx.shape[1]
    n_full = t // tm
    tail = t - n_full * tm
    assert n_full >= 1 and tail > 0 and tail + n_ctx == tm and (n_full * tm) % tail == 0
    specs = [pl.BlockSpec((1, tm, d), lambda i, j: (i, jnp.minimum(j, n_full - 1), 0)),
             pl.BlockSpec((1, tail, d), lambda i, j: (i, (n_full * tm) // tail, 0)),
             pl.BlockSpec((1, n_ctx, d), lambda i, j: (i, 0, 0))]
    return specs, [lat, lat, ctx]


def _stream_tile(refs):
    if len(refs) == 1:
        return refs[0][0]
    main_ref, tail_ref, ctx_ref = refs
    mixed = jnp.concatenate([tail_ref[0], ctx_ref[0]], axis=0)
    return jnp.where(pl.program_id(1) < pl.num_programs(1) - 1, main_ref[0], mixed)


def _inproj_kernel(*refs, q_scale, n_lat, n_src):
    (modb_ref, modc_ref, g1_ref, w_in_ref, qg_ref, wq_ref, kvg_ref, wk_ref, wvt_ref, cos_ref, sin_ref,
     q_ref, k_ref, vt_ref, hg_ref, fn_ref) = refs[n_src:]
    x = _stream_tile(refs[:n_src])
    tm = x.shape[0]
    xn = x * lax.rsqrt(jnp.mean(x * x, axis=-1, keepdims=True) + EPS) * g1_ref[...]
    xm = _modulated(tm, n_lat, modc_ref, modb_ref,
                    lambda mod, rows: (rows * (1.0 + mod[1:2, :]) + mod[0:1, :]).astype(BF16), xn)

    hg_ref[0] = _dot(xm, w_in_ref[:, P_HG:P_FN]).astype(BF16)
    fn_ref[0] = _dot(xm, w_in_ref[:, P_FN:P_WIDTH]).astype(BF16)
    p = _dot(xm, w_in_ref[:, 0:P_HG])

    cos = cos_ref[...]
    sin = sin_ref[...]

    cq = p[:, P_CQ:P_CKV]
    cqn = (cq * lax.rsqrt(jnp.mean(cq * cq, axis=-1, keepdims=True) + EPS) * qg_ref[...]).astype(BF16)
    qq = _dot(cqn, wq_ref[...])
    nw = MLA_HEADS * MLA_NOPE
    for h in range(MLA_HEADS):
        q_ref[0, :, h * HEAD_PAD:h * HEAD_PAD + 128] = (qq[:, h * 128:(h + 1) * 128] * q_scale).astype(BF16)
        pe = qq[:, nw + h * 128:nw + (h + 1) * 128] * cos + qq[:, 2 * nw + h * 128:2 * nw + (h + 1) * 128] * sin
        q_ref[0, :, h * HEAD_PAD + 128:(h + 1) * HEAD_PAD] = (pe * q_scale).astype(BF16)

    ckv = p[:, P_CKV:P_KPA]
    ckvn = (ckv * lax.rsqrt(jnp.mean(ckv * ckv, axis=-1, keepdims=True) + EPS) * kvg_ref[...]).astype(BF16)
    kv = _dot(ckvn, wk_ref[...])
    vt_ref[0] = _dot_nt(wvt_ref[...], ckvn).astype(BF16)
    kpe = (p[:, P_KPA:P_KPB] * cos + p[:, P_KPB:P_HG] * sin).astype(BF16)
    for h in range(MLA_HEADS):
        k_ref[0, :, h * HEAD_PAD:h * HEAD_PAD + 128] = kv[:, h * 128:(h + 1) * 128].astype(BF16)
        k_ref[0, :, h * HEAD_PAD + 128:(h + 1) * HEAD_PAD] = kpe


def _inproj(src, mod, g1, w_in_p, qg, wq_all, kvg, wk_p, wvt_p, cos_t, sin_t, n_lat, ctx_row):
    s = cos_t.shape[0]
    d = g1.shape[0]
    b = (src[0] if isinstance(src, tuple) else src).shape[0]
    tm = STREAM_TILE
    row = lambda w: pl.BlockSpec((1, tm, w), lambda i, j: (i, j, 0))
    src_specs, src_args = _stream_specs(src, tm)
    return pl.pallas_call(
        functools.partial(_inproj_kernel, q_scale=math.log2(math.e) / math.sqrt(MLA_NOPE + MLA_ROPE), n_lat=n_lat,
                          n_src=len(src_args)),
        grid=(b, s // tm),
        in_specs=src_specs + [
            pl.BlockSpec((1, 6, d), lambda i, j: (i, 0, 0)),
            pl.BlockSpec((1, 6, d), lambda i, j: (ctx_row, 0, 0)),
            _const_spec((1, d)),
            _RESIDENT,
            _const_spec((1, MLA_Q_LORA)),
            _RESIDENT,
            _const_spec((1, MLA_KV_LORA)),
            _RESIDENT,
            _RESIDENT,
            pl.BlockSpec((tm, 128), lambda i, j: (j, 0)),
            pl.BlockSpec((tm, 128), lambda i, j: (j, 0)),
        ],
        out_specs=[row(MLA_HEADS * HEAD_PAD), row(MLA_HEADS * HEAD_PAD),
                   pl.BlockSpec((1, MLA_WIDTH, tm), lambda i, j: (i, 0, j)), row(HG_IN_WIDTH), row(FN_WIDTH)],
        out_shape=[
            jax.ShapeDtypeStruct((b, s, MLA_HEADS * HEAD_PAD), BF16),
            jax.ShapeDtypeStruct((b, s, MLA_HEADS * HEAD_PAD), BF16),
            jax.ShapeDtypeStruct((b, MLA_WIDTH, s), BF16),
            jax.ShapeDtypeStruct((b, s, HG_IN_WIDTH), BF16),
            jax.ShapeDtypeStruct((b, s, FN_WIDTH), BF16),
        ],
        compiler_params=_params(2),
        name="inproj",
    )(*src_args, mod, mod, g1.reshape(1, d), w_in_p, qg.reshape(1, -1), wq_all, kvg.reshape(1, -1), wk_p, wvt_p,
      cos_t, sin_t)


def _attend(q_ref, k_ref, vt_ref, o_ref, key_lo, key_hi):
    def scores(h):
        qh = q_ref[0, :, h * HEAD_PAD:(h + 1) * HEAD_PAD]
        kh = k_ref[0, key_lo:key_hi, h * HEAD_PAD:(h + 1) * HEAD_PAD]
        return _dot_nt(kh, qh)

    all_scores = [scores(h) for h in range(MLA_HEADS)]
    probs = []
    for st in all_scores:
        e = jnp.exp2(st - jnp.max(st, axis=0, keepdims=True))
        probs.append((e.astype(BF16), 1.0 / jnp.sum(e, axis=0, keepdims=True)))
    for h, (e, inv_denom) in enumerate(probs):
        vth = vt_ref[0, h * MLA_V:(h + 1) * MLA_V, key_lo:key_hi]
        ot = _dot(vth, e) * inv_denom
        o_ref[0, :, h * MLA_V:(h + 1) * MLA_V] = ot.T.astype(BF16)


def _attn_kernel(q_ref, k_ref, vt_ref, o_ref):
    _attend(q_ref, k_ref, vt_ref, o_ref, 0, k_ref.shape[1])


def _attn_ctx_kernel(q_ref, qc_ref, k_ref, vt_ref, o_ref, oc_ref, *, n_lat):
    s_len = k_ref.shape[1]
    _attend(q_ref, k_ref, vt_ref, o_ref, 0, s_len)

    @pl.when(pl.program_id(1) == pl.num_programs(1) - 1)
    def _():
        _attend(qc_ref, k_ref, vt_ref, oc_ref, n_lat, s_len)


def _attention(q, k, vt, n_lat, with_ctx_queries):
    b, s, _ = q.shape
    tq = ATTN_TILE
    n_ctx = s - n_lat
    q_spec = pl.BlockSpec((1, tq, q.shape[2]), lambda i, j: (i, j, 0))
    kv_specs = [pl.BlockSpec((1, s, k.shape[2]), lambda i, j: (i, 0, 0)),
                pl.BlockSpec((1, vt.shape[1], s), lambda i, j: (i, 0, 0))]
    o_spec = pl.BlockSpec((1, tq, MLA_WIDTH), lambda i, j: (i, j, 0))
    o_shape = jax.ShapeDtypeStruct((b, n_lat, MLA_WIDTH), BF16)
    if not with_ctx_queries:
        out = pl.pallas_call(
            _attn_kernel,
            grid=(b, n_lat // tq),
            in_specs=[q_spec] + kv_specs,
            out_specs=o_spec,
            out_shape=o_shape,
            compiler_params=_params(2),
            name="attention",
        )(q, k, vt)
        return out, None
    assert n_lat % n_ctx == 0
    return pl.pallas_call(
        functools.partial(_attn_ctx_kernel, n_lat=n_lat),
        grid=(b, n_lat // tq),
        in_specs=[q_spec, pl.BlockSpec((1, n_ctx, q.shape[2]), lambda i, j: (i, n_lat // n_ctx, 0))] + kv_specs,
        out_specs=[o_spec, pl.BlockSpec((1, n_ctx, MLA_WIDTH), lambda i, j: (i, 0, 0))],
        out_shape=[o_shape, jax.ShapeDtypeStruct((b, n_ctx, MLA_WIDTH), BF16)],
        compiler_params=_params(2),
        name="attention_ctx",
    )(q, q, k, vt)


def _hgrn_constants():
    c = HG_CHUNK
    t = np.arange(c)[:, None]
    u = np.arange(c)[None, :]
    blocks = [(u <= t)]
    for m in (8, 4):
        half = m // 2
        mid = (t // m) * m + half
        lower = (t % m) >= half
        blocks.append(np.where(lower, (u >= mid) & (u <= t), (u >= t + 1) & (u <= mid - 1)))
    mf = np.stack([blk.astype(np.float32) for blk in blocks])
    mb = np.ascontiguousarray(mf[:, ::-1, ::-1])
    msb = np.floor(np.log2(np.maximum(t ^ u, 1))).astype(np.int32)
    lvl = np.where(t > u, HG_LEVELS - 1 - msb, np.where(t == u, HG_LEVELS, -1)).astype(np.int32)
    lvl_f = np.tile(lvl, (1, 2))
    lvl_b = np.tile(np.ascontiguousarray(lvl.T), (1, 2))
    return mf.reshape(-1, c), mb.reshape(-1, c), lvl_f, lvl_b


def _pair_weights(x, p):
    zeros = jnp.zeros((x.shape[0], HG_K), x.dtype)
    xa = x[:, 2 * p * HG_K:(2 * p + 1) * HG_K]
    xb = x[:, (2 * p + 1) * HG_K:(2 * p + 2) * HG_K]
    blk = jnp.concatenate([jnp.concatenate([xa, zeros], axis=1), jnp.concatenate([zeros, xb], axis=1)], axis=0)
    return blk.T


def _hgrn_gates(z, lb, m_ref, with_output):
    f = lb + (1.0 - lb) * _sigmoid(z)
    g = jnp.log2(f)
    g_hi = g.astype(BF16)
    g_lo = (g - g_hi.astype(F32)).astype(BF16)
    m_all = m_ref[...] if with_output else m_ref[0:HG_CHUNK, :]
    return f, _dot(m_all, g_hi) + _dot(m_all, g_lo)


def _hgrn_decays(hq, f, eb, fwd, with_output):
    c = HG_CHUNK
    kk = 1.0 - f
    bc = eb[0:c]
    total = bc[c - 1:c] if fwd else bc[0:1]
    k_up = (kk * jnp.exp2(total - bc)).astype(BF16)
    chunk_decay = jnp.exp2(total)
    if not with_output:
        return k_up, chunk_decay, None
    qq = _silu(hq)
    row = lax.broadcasted_iota(jnp.int32, (c, HG_KW), 0)
    xs = []
    for l in range(HG_LEVELS):
        m = c >> l
        half = m // 2
        bit = (row & half) != 0
        is_query = bit if fwd else jnp.logical_not(bit)
        if m == 2:
            xs.append(jnp.where(is_query, qq * f, kk).astype(BF16))
            continue
        if m >= 16:
            refs = [bc[b0 + half - 1:b0 + half] if fwd else bc[b0 + half:b0 + half + 1] for b0 in range(0, c, m)]
            ref = jnp.concatenate([jnp.broadcast_to(r, (m, HG_KW)) for r in refs], axis=0)
            diff = bc - ref
            expo = jnp.where(is_query, diff, -diff)
        else:
            idx = 1 + l - (HG_LEVELS - 1 - HG_MATMUL_LEVELS)
            expo = eb[idx * c:(idx + 1) * c]
        xs.append((jnp.where(is_query, qq, kk) * jnp.exp2(expo)).astype(BF16))
    q_in = (qq * jnp.exp2(bc)).astype(BF16)
    return k_up, chunk_decay, (xs, qq.astype(BF16), kk.astype(BF16), q_in)


def _hgrn_scores(operands, lvl_masks):
    xs, qb, kb, _ = operands
    scores = []
    for p in range(HG_HEADS // 2):
        kp = slice(2 * p * HG_K, (2 * p + 2) * HG_K)
        sc = jnp.where(lvl_masks[HG_LEVELS], _dot(qb[:, kp], _pair_weights(kb, p)), 0.0)
        for l in range(HG_LEVELS):
            sc = jnp.where(lvl_masks[l], _dot(xs[l][:, kp], _pair_weights(xs[l], p)), sc)
        scores.append(sc.astype(BF16))
    return scores


def _hgrn_apply(scores, operands, k_up, chunk_decay, vv, v_lo, v_hi, state_ref):
    outs = []
    for p in range(HG_HEADS // 2):
        kp = slice(2 * p * HG_K, (2 * p + 2) * HG_K)
        vp = slice(2 * p * HG_V, (2 * p + 2) * HG_V)
        if scores is not None:
            v_pair = vv[:, vp]
            v_diag = jnp.concatenate([v_pair * v_lo, v_pair * v_hi], axis=0)
            st_pair = state_ref[2 * p * HG_V:(2 * p + 2) * HG_V, kp]
            outs.append(_dot(scores[p], v_diag) + _dot(operands[3][:, kp], st_pair.astype(BF16).T))
        for h in (2 * p, 2 * p + 1):
            ks = slice(h * HG_K, (h + 1) * HG_K)
            vs = slice(h * HG_V, (h + 1) * HG_V)
            state_ref[vs, ks] = chunk_decay[:, ks] * state_ref[vs, ks] + _dot_tn(vv[:, vs], k_up[:, ks])
    return jnp.concatenate(outs, axis=1) if outs else None


def _hgrn_kernel(x_ref, lb_ref, gain_ref, mf_ref, mb_ref, lvlf_ref, lvlb_ref, ones_ref, o_ref,
                 of_ref, ob_ref, sf_ref, sb_ref, *, n_lat_chunks):
    c = HG_CHUNK
    s_len = x_ref.shape[1]
    n_chunks = s_len // c
    n_ctx_chunks = n_chunks - n_lat_chunks
    sf_ref[...] = jnp.zeros_like(sf_ref)
    sb_ref[...] = jnp.zeros_like(sb_ref)
    lvl_f = lvlf_ref[...]
    lvl_b = lvlb_ref[...]
    masks_f = [lvl_f == l for l in range(HG_LEVELS + 1)]
    masks_b = [lvl_b == l for l in range(HG_LEVELS + 1)]
    lb_f = lb_ref[0:1, :]
    lb_b = lb_ref[1:2, :]
    v_lane = lax.broadcasted_iota(jnp.int32, (c, 2 * HG_V), 1)
    v_lo = (v_lane < HG_V).astype(F32).astype(BF16)
    v_hi = (v_lane >= HG_V).astype(F32).astype(BF16)

    def load(ci):
        r0 = pl.multiple_of(ci * c, c)
        hq = x_ref[0, pl.ds(r0, c), 0:HG_KW].astype(F32)
        vv = x_ref[0, pl.ds(r0, c), 3 * HG_KW:3 * HG_KW + HG_WIDTH]
        return r0, hq, vv

    def scan(first_fwd, first_bwd, n, with_output):
        per_trip = HG_UNROLL if n % HG_UNROLL == 0 else n

        def step(i, carry):
            gated = []
            for u in range(per_trip):
                for fwd in (True, False):
                    ci = first_fwd + i * per_trip + u if fwd else first_bwd - i * per_trip - u
                    r0, hq, vv = load(ci)
                    zcol = HG_KW if fwd else 2 * HG_KW
                    z = x_ref[0, pl.ds(r0, c), zcol:zcol + HG_KW].astype(F32)
                    gated.append((fwd, r0, hq, vv, _hgrn_gates(z, lb_f if fwd else lb_b, mf_ref if fwd else mb_ref,
                                                               with_output)))
            jobs = [(fwd, r0, vv, _hgrn_decays(hq, f, eb, fwd, with_output)) for fwd, r0, hq, vv, (f, eb) in gated]
            scores = [_hgrn_scores(ph[2], masks_f if fwd else masks_b) if with_output else None
                      for fwd, _, _, ph in jobs]
            for (fwd, r0, vv, (k_up, chunk_decay, operands)), sc in zip(jobs, scores):
                out = _hgrn_apply(sc, operands, k_up, chunk_decay, vv, v_lo, v_hi, sf_ref if fwd else sb_ref)
                if with_output:
                    (of_ref if fwd else ob_ref)[pl.ds(r0, c), :] = out
            return carry

        lax.fori_loop(0, n // per_trip, step, 0)

    scan(n_lat_chunks, n_chunks - 1, n_ctx_chunks, o_ref.shape[1] == s_len)
    scan(0, n_lat_chunks - 1, n_lat_chunks, True)

    rt = 256
    gain = gain_ref[...]
    ones = ones_ref[...]

    def readout(i, carry):
        r0 = pl.multiple_of(i * rt, rt)
        o = of_ref[pl.ds(r0, rt), :] + ob_ref[pl.ds(r0, rt), :]
        sq_hi, sq_mid, sq_lo = _split3(o * o)
        msq = _dot(sq_hi, ones) + _dot(sq_mid, ones) + _dot(sq_lo, ones)
        zg = x_ref[0, pl.ds(r0, rt), 3 * HG_KW + HG_WIDTH:HG_IN_WIDTH].astype(F32)
        y = o * lax.rsqrt(msq + EPS) * gain * _silu(zg)
        o_ref[0, pl.ds(r0, rt), :] = y.astype(BF16)
        return carry

    lax.fori_loop(0, o_ref.shape[1] // rt, readout, 0)


def _hgrn(hg_in, lb, gain, n_lat, with_ctx_outputs):
    b, s, w = hg_in.shape
    mf, mb, lvl_f, lvl_b = _hgrn_constants()
    ones = np.kron(np.eye(HG_HEADS, dtype=np.float32), np.full((HG_V, HG_V), 1.0 / HG_V, np.float32))
    n_out = s if with_ctx_outputs else n_lat
    return pl.pallas_call(
        functools.partial(_hgrn_kernel, n_lat_chunks=n_lat // HG_CHUNK),
        grid=(b,),
        in_specs=[
            pl.BlockSpec((1, s, w), lambda i: (i, 0, 0)),
            _const_spec((2, HG_KW)),
            _const_spec((1, HG_WIDTH)),
            _const_spec(mf.shape),
            _const_spec(mb.shape),
            _const_spec(lvl_f.shape),
            _const_spec(lvl_b.shape),
            _const_spec(ones.shape),
        ],
        out_specs=pl.BlockSpec((1, n_out, HG_WIDTH), lambda i: (i, 0, 0)),
        out_shape=jax.ShapeDtypeStruct((b, n_out, HG_WIDTH), BF16),
        scratch_shapes=[
            pltpu.VMEM((s, HG_WIDTH), F32),
            pltpu.VMEM((s, HG_WIDTH), F32),
            pltpu.VMEM((HG_WIDTH, HG_KW), F32),
            pltpu.VMEM((HG_WIDTH, HG_KW), F32),
        ],
        compiler_params=_params(1),
        name="hgrn2",
    )(hg_in, lb, gain.reshape(1, HG_WIDTH), jnp.asarray(mf, BF16), jnp.asarray(mb, BF16),
      jnp.asarray(lvl_f), jnp.asarray(lvl_b), jnp.asarray(ones, BF16))


def _dft_tables(n):
    idx = np.arange(n, dtype=np.int64)
    ang = 2.0 * np.pi * ((idx[:, None] * idx[None, :]) % n).astype(np.float64) / n
    return np.cos(ang).astype(np.float32), np.sin(ang).astype(np.float32)


def _fourier_tile(z, table, mix, w, norm):
    tn = table.shape[0] // 2
    zz = _dot(table, z)
    zcat = jnp.concatenate([zz[0:tn], zz[tn:2 * tn]], axis=1).astype(BF16)
    mixed = _dot(zcat, mix) * norm
    return _dot(mixed.astype(BF16), w).astype(BF16)


def _fourier_kernel(z_ref, tl_ref, tc_ref, mix_ref, w_ref, o_ref, *maybe_oc_ref, n_lat):
    s_len = z_ref.shape[1]
    n_ctx = s_len - n_lat
    mix = mix_ref[...]
    w = w_ref[...]
    table = tl_ref[pl.program_id(1)]
    o_ref[0] = _fourier_tile(z_ref[0, 0:n_lat, :], table, mix, w, 1.0 / math.sqrt(n_lat * FN_GROUP_DIM))
    if maybe_oc_ref:
        @pl.when(pl.program_id(1) == pl.num_programs(1) - 1)
        def _():
            maybe_oc_ref[0][0] = _fourier_tile(z_ref[0, n_lat:s_len, :], tc_ref[...], mix, w,
                                               1.0 / math.sqrt(n_ctx * FN_GROUP_DIM))


def _stacked_dft_table(n, tile):
    cos, sin = _dft_tables(n)
    return np.concatenate([cos.reshape(n // tile, tile, n), sin.reshape(n // tile, tile, n)], axis=1)


def _fourier(fn_in, w_fourier, n_lat, with_ctx_outputs):
    b, s, w = fn_in.shape
    n_ctx = s - n_lat
    tn = FOURIER_TILE
    table_lat = _stacked_dft_table(n_lat, tn)
    table_ctx = _stacked_dft_table(n_ctx, n_ctx)[0]
    cgrp, sgrp = _dft_tables(FN_GROUP_DIM)
    eye = np.eye(FN_GROUPS, dtype=np.float32)
    mix = np.concatenate([np.kron(eye, cgrp), -np.kron(eye, sgrp)], axis=0)
    out_specs = [pl.BlockSpec((1, tn, FN_WIDTH), lambda i, j: (i, j, 0))]
    out_shape = [jax.ShapeDtypeStruct((b, n_lat, FN_WIDTH), BF16)]
    if with_ctx_outputs:
        out_specs.append(pl.BlockSpec((1, n_ctx, FN_WIDTH), lambda i, j: (i, 0, 0)))
        out_shape.append(jax.ShapeDtypeStruct((b, n_ctx, FN_WIDTH), BF16))
    outs = pl.pallas_call(
        functools.partial(_fourier_kernel, n_lat=n_lat),
        grid=(b, n_lat // tn),
        in_specs=[
            pl.BlockSpec((1, s, w), lambda i, j: (i, 0, 0)),
            _RESIDENT,
            _const_spec(table_ctx.shape),
            _const_spec(mix.shape),
            _const_spec(w_fourier.shape),
        ],
        out_specs=out_specs,
        out_shape=out_shape,
        compiler_params=_params(2),
        name="fourier",
    )(fn_in, jnp.asarray(table_lat, BF16), jnp.asarray(table_ctx, BF16), jnp.asarray(mix, BF16), w_fourier)
    return (outs[0], outs[1]) if with_ctx_outputs else (outs[0], None)


def _ffn_kernel(*refs, final_norm, n_lat, select_ctx, n_srcs):
    streams = []
    pos = 0
    for n in n_srcs:
        streams.append(_stream_tile(refs[pos:pos + n]))
        pos += n
    x, attn, hg, fn = streams
    modb_ref, modc_ref, g2_ref, wo_ref, wgu_ref, wd_ref, gf_ref, o_ref, act_ref = refs[pos:]
    tm = x.shape[0]

    modulated = functools.partial(_modulated, tm, n_lat, modc_ref, modb_ref, use_ctx=select_ctx)

    y = (_dot(attn, wo_ref[0:MLA_WIDTH, :])
         + _dot(hg, wo_ref[MLA_WIDTH:MLA_WIDTH + HG_WIDTH, :])
         + _dot(fn, wo_ref[MLA_WIDTH + HG_WIDTH:MLA_WIDTH + HG_WIDTH + FN_WIDTH, :]))
    x1 = modulated(lambda mod, xr, yr: xr + mod[2:3, :] * yr, x, y)
    o_ref[0] = x1
    xn = x1 * lax.rsqrt(jnp.mean(x1 * x1, axis=-1, keepdims=True) + EPS) * g2_ref[...]
    hm = modulated(lambda mod, rows: (rows * (1.0 + mod[4:5, :]) + mod[3:4, :]).astype(BF16), xn)

    ft = wgu_ref.shape[2] // 2
    for t in range(wgu_ref.shape[0]):
        gu = _dot(hm, wgu_ref[t])
        act_ref[:, t * ft:(t + 1) * ft] = (_silu(gu[:, 0:ft]) * gu[:, ft:2 * ft]).astype(BF16)
    out = modulated(lambda mod, xr, ar: xr + mod[5:6, :] * ar, o_ref[0], _dot(act_ref[...], wd_ref[...]))
    if final_norm:
        out = out * lax.rsqrt(jnp.mean(out * out, axis=-1, keepdims=True) + EPS) * gf_ref[...]
    o_ref[0] = out


def _outproj_ffn(x_src, attn, hg, fn, mod, g2, w_out, w_gu, w_d, g_final, n_lat, n_ctx, ctx_row, with_ctx_rows,
                 final_norm):
    d = g2.shape[0]
    b = (x_src[0] if isinstance(x_src, tuple) else x_src).shape[0]
    tm = STREAM_TILE if with_ctx_rows else LATENT_TILE
    n_rows = n_lat + n_ctx if with_ctx_rows else n_lat
    hidden = w_d.shape[0]
    row = lambda w: pl.BlockSpec((1, tm, w), lambda i, j: (i, j, 0))
    src_specs, src_args, n_srcs = [], [], []
    for src in (x_src, attn, hg, fn):
        specs, args = _stream_specs(src, tm)
        src_specs += specs
        src_args += args
        n_srcs.append(len(args))
    return pl.pallas_call(
        functools.partial(_ffn_kernel, final_norm=final_norm, n_lat=n_lat, select_ctx=with_ctx_rows,
                          n_srcs=tuple(n_srcs)),
        grid=(b, n_rows // tm),
        in_specs=src_specs + [
            pl.BlockSpec((1, 6, d), lambda i, j: (i, 0, 0)),
            pl.BlockSpec((1, 6, d), lambda i, j: (ctx_row, 0, 0)),
            _const_spec((1, d)),
            _RESIDENT,
            _RESIDENT,
            _RESIDENT,
            _const_spec((1, d)),
        ],
        out_specs=row(d),
        out_shape=jax.ShapeDtypeStruct((b, n_rows, d), F32),
        scratch_shapes=[pltpu.VMEM((tm, hidden), BF16)],
        compiler_params=_params(2),
        name="outproj_ffn",
    )(*src_args, mod, mod, g2.reshape(1, d), w_out, w_gu, w_d, g_final.reshape(1, d))


def _rotate_half_cols(w):
    r1, r2, c1, c2 = jnp.split(w, 4, axis=-1)
    return jnp.concatenate([-r2, r1, -c2, c1], axis=-1)


def _pack_w_in(w):
    d = w.shape[0]
    zeros = jnp.zeros((d, 128 - MLA_ROPE), w.dtype)
    w_kr = w[:, O_KR:O_HQ]
    return jnp.concatenate([w[:, O_CQ:O_KR], w_kr, zeros, _rotate_half_cols(w_kr), zeros, w[:, O_HQ:]],
                           axis=1).astype(BF16)


def _pack_w_uq(w):
    r = w.shape[0]
    wh = w.reshape(r, MLA_HEADS, MLA_NOPE + MLA_ROPE)
    nope = wh[:, :, :MLA_NOPE].reshape(r, MLA_HEADS * MLA_NOPE)
    pe = wh[:, :, MLA_NOPE:]
    zeros = jnp.zeros((r, MLA_HEADS, 128 - MLA_ROPE), w.dtype)
    pe_a = jnp.concatenate([pe, zeros], axis=-1).reshape(r, MLA_HEADS * 128)
    pe_b = jnp.concatenate([_rotate_half_cols(pe), zeros], axis=-1).reshape(r, MLA_HEADS * 128)
    return jnp.concatenate([nope, pe_a, pe_b], axis=1).astype(BF16)


def _pack_w_ukv(w):
    r = w.shape[0]
    wh = w.reshape(r, MLA_HEADS, MLA_NOPE + MLA_V)
    k_nope = wh[:, :, :MLA_NOPE].reshape(r, MLA_HEADS * MLA_NOPE)
    v = wh[:, :, MLA_NOPE:].reshape(r, MLA_HEADS * MLA_V)
    return k_nope.astype(BF16), v.T.astype(BF16)


def _pack_w_gate_up(w, hidden):
    d = w.shape[0]
    n_ft = hidden // FFN_TILE
    gate = w[:, :hidden].reshape(d, n_ft, FFN_TILE)
    up = w[:, hidden:].reshape(d, n_ft, FFN_TILE)
    return jnp.concatenate([gate, up], axis=2).transpose(1, 0, 2).astype(BF16)


def _rope_tables(n_lat, n_ctx):
    rows = n_lat // GRID_W
    row_pos = np.repeat(np.arange(rows, dtype=np.float32), GRID_W)
    col_pos = np.tile(np.arange(GRID_W, dtype=np.float32), rows)
    axis_dim = MLA_ROPE // 2
    inv_freq = (ROPE_BASE ** (-np.arange(0, axis_dim, 2, dtype=np.float32) / axis_dim)).astype(np.float32)
    ang_r = row_pos[:, None] * inv_freq
    ang_c = col_pos[:, None] * inv_freq
    ang = np.concatenate([ang_r, ang_r, ang_c, ang_c], axis=-1)
    cos = np.ones((n_lat + n_ctx, 128), np.float32)
    sin = np.zeros((n_lat + n_ctx, 128), np.float32)
    cos[:n_lat, :MLA_ROPE] = np.cos(ang)
    sin[:n_lat, :MLA_ROPE] = np.sin(ang)
    return jnp.asarray(cos), jnp.asarray(sin)


def kernel(x, c, ctx, c_ctx, w_mod, b_mod, norm1_g, norm2_g, w_in, q_norm_g, w_uq, kv_norm_g, w_ukv, lb_param,
           hg_norm_g, w_fourier, w_out, w_gate_up, w_down, final_norm_g):
    b, t, d = x.shape
    n_ctx = ctx.shape[1]
    s = t + n_ctx
    depth = w_mod.shape[0]
    hidden = w_down.shape[1]
    assert s % STREAM_TILE == 0 and t % LATENT_TILE == 0 and t % ATTN_TILE == 0 and t % FOURIER_TILE == 0
    assert hidden % FFN_TILE == 0 and t % GRID_W == 0 and n_ctx % HG_CHUNK == 0

    mod_rows = -(-(b + 1) // 8) * 8
    c_all = jnp.zeros((mod_rows, d), F32).at[:b].set(c).at[b].set(c_ctx)
    mod_all = _modulation(c_all, w_mod, b_mod).reshape(depth, mod_rows, 6, d)
    lower = _lower_bounds(lb_param)
    cos_t, sin_t = _rope_tables(t, n_ctx)

    xa = (x, ctx)
    for l in range(depth):
        last = l == depth - 1
        q, k, v, hg_in, fn_in = _inproj(xa, mod_all[l], norm1_g[l], _pack_w_in(w_in[l]), q_norm_g[l],
                                        _pack_w_uq(w_uq[l]), kv_norm_g[l], *_pack_w_ukv(w_ukv[l]), cos_t, sin_t, t, b)
        attn, attn_ctx = _attention(q, k, v, t, not last)
        hg = _hgrn(hg_in, lower[l], hg_norm_g[l], t, not last)
        fn, fn_ctx = _fourier(fn_in, w_fourier[l].astype(BF16), t, not last)
        if not last:
            attn, fn = (attn, attn_ctx), (fn, fn_ctx)
        xa = _outproj_ffn(xa, attn, hg, fn, mod_all[l], norm2_g[l], w_out[l].astype(BF16),
                          _pack_w_gate_up(w_gate_up[l], hidden), w_down[l].astype(BF16), final_norm_g, t, n_ctx, b,
                          not last, last)
    return xa
```

```python
import functools
import math

import numpy as np
import jax
import jax.numpy as jnp
from jax import lax
from jax.experimental import pallas as pl
from jax.experimental.pallas import tpu as pltpu

EPS = 1e-6
GRID_W = 64
ROPE_BASE = 10000.0

MLA_HEADS = 4
MLA_Q_LORA = 256
MLA_KV_LORA = 128
MLA_NOPE = 128
MLA_ROPE = 64
MLA_V = 128
MLA_WIDTH = MLA_HEADS * MLA_V
HEAD_PAD = 256

HG_HEADS = 4
HG_K = 128
HG_V = 64
HG_WIDTH = HG_HEADS * HG_V
HG_KW = HG_HEADS * HG_K
HG_CHUNK = 64
HG_LEVELS = 6
HG_MATMUL_LEVELS = 2
HG_UNROLL = 8

FN_GROUPS = 4
FN_GROUP_DIM = 64
FN_WIDTH = FN_GROUPS * FN_GROUP_DIM

O_CQ = 0
O_CKV = O_CQ + MLA_Q_LORA
O_KR = O_CKV + MLA_KV_LORA
O_HQ = O_KR + MLA_ROPE

P_CQ = 0
P_CKV = P_CQ + MLA_Q_LORA
P_KPA = P_CKV + MLA_KV_LORA
P_KPB = P_KPA + 128
P_HG = P_KPB + 128
HG_IN_WIDTH = 3 * HG_KW + 2 * HG_WIDTH
P_FN = P_HG + HG_IN_WIDTH
P_WIDTH = P_FN + FN_WIDTH

STREAM_TILE = 768
LATENT_TILE = 1024
ATTN_TILE = 512
FOURIER_TILE = 1024
FFN_TILE = 256
VMEM_LIMIT = 56 * 1024 * 1024

BF16 = jnp.bfloat16
F32 = jnp.float32


def _dot(a, b):
    return jnp.dot(a, b, preferred_element_type=F32)


def _dot_nt(a, b):
    return lax.dot_general(a, b, (((1,), (1,)), ((), ())), preferred_element_type=F32)


def _dot_tn(a, b):
    return lax.dot_general(a, b, (((0,), (0,)), ((), ())), preferred_element_type=F32)


def _silu(x):
    return x * (1.0 / (1.0 + jnp.exp(-x)))


def _sigmoid(x):
    return 1.0 / (1.0 + jnp.exp(-x))


def _split3(x):
    hi = x.astype(BF16)
    r1 = x - hi.astype(F32)
    mid = r1.astype(BF16)
    lo = (r1 - mid.astype(F32)).astype(BF16)
    return hi, mid, lo


def _const_spec(shape):
    nd = len(shape)
    return pl.BlockSpec(shape, lambda *_: (0,) * nd)


_RESIDENT = pl.BlockSpec(memory_space=pltpu.VMEM)


def _params(n_axes):
    return pltpu.CompilerParams(dimension_semantics=("arbitrary",) * n_axes, vmem_limit_bytes=VMEM_LIMIT)


def _mod_kernel(c_ref, w_ref, b_ref, o_ref):
    act = _silu(c_ref[...]).astype(BF16)
    o_ref[0] = _dot(act, w_ref[0].astype(BF16)) + b_ref[0]


def _modulation(c_all, w_mod, b_mod):
    depth, d, n6 = w_mod.shape
    rows = c_all.shape[0]
    tn = n6 // 6
    return pl.pallas_call(
        _mod_kernel,
        grid=(depth, n6 // tn),
        in_specs=[
            pl.BlockSpec((rows, d), lambda l, j: (0, 0)),
            pl.BlockSpec((1, d, tn), lambda l, j: (l, 0, j)),
            pl.BlockSpec((1, 1, tn), lambda l, j: (l, 0, j)),
        ],
        out_specs=pl.BlockSpec((1, rows, tn), lambda l, j: (l, 0, j)),
        out_shape=jax.ShapeDtypeStruct((depth, rows, n6), F32),
        name="modulation",
    )(c_all, w_mod, b_mod.reshape(depth, 1, n6))


def _lower_bound_kernel(lb_ref, o_ref):
    depth = lb_ref.shape[0]
    lp = [lb_ref[l] for l in range(depth)]
    m = lp[0]
    for l in range(1, depth):
        m = jnp.maximum(m, lp[l])
    e = [jnp.exp(v - m) for v in lp]
    tot = e[0]
    for l in range(1, depth):
        tot = tot + e[l]
    probs = [v / tot for v in e]
    cum = probs[0]
    o_ref[0] = cum - probs[0]
    for l in range(1, depth):
        cum = cum + probs[l]
        o_ref[l] = cum - probs[0]


def _lower_bounds(lb_param):
    return pl.pallas_call(
        _lower_bound_kernel,
        out_shape=jax.ShapeDtypeStruct(lb_param.shape, F32),
        name="hgrn_lower_bounds",
    )(lb_param.astype(F32))


def _modulated(tile_rows, n_lat, modc_ref, modb_ref, fn, *arrays, use_ctx=True):
    lat = modb_ref[0]
    if not use_ctx:
        return fn(lat, *arrays)
    tail = n_lat % tile_rows
    if tail == 0:
        return fn(jnp.where(pl.program_id(1) * tile_rows >= n_lat, modc_ref[0], lat), *arrays)
    is_last = pl.program_id(1) == pl.num_programs(1) - 1
    mixed = jnp.where(is_last, modc_ref[0], lat)
    return jnp.concatenate([fn(lat, *[a[0:tail] for a in arrays]),
                            fn(mixed, *[a[tail:tile_rows] for a in arrays])], axis=0)


def _stream_specs(src, tm):
    if not isinstance(src, tuple):
        return [pl.BlockSpec((1, tm, src.shape[2]), lambda i, j: (i, j, 0))], [src]
    lat, ctx = src
    t, d = lat.shape[1:]
    n_ctx = ctx.shape[1]
    n_full = t // tm
    tail = t - n_full * tm
    assert n_full >= 1 and tail > 0 and tail + n_ctx == tm and (n_full * tm) % tail == 0
    specs = [pl.BlockSpec((1, tm, d), lambda i, j: (i, jnp.minimum(j, n_full - 1), 0)),
             pl.BlockSpec((1, tail, d), lambda i, j: (i, (n_full * tm) // tail, 0)),
             pl.BlockSpec((1, n_ctx, d), lambda i, j: (i, 0, 0))]
    return specs, [lat, lat, ctx]


def _stream_tile(refs):
    if len(refs) == 1:
        return refs[0][0]
    main_ref, tail_ref, ctx_ref = refs
    mixed = jnp.concatenate([tail_ref[0], ctx_ref[0]], axis=0)
    return jnp.where(pl.program_id(1) < pl.num_programs(1) - 1, main_ref[0], mixed)


def _inproj_kernel(*refs, q_scale, n_lat, n_src):
    (modb_ref, modc_ref, g1_ref, w_in_ref, qg_ref, wq_ref, kvg_ref, wk_ref, wvt_ref, cos_ref, sin_ref,
     q_ref, k_ref, vt_ref, hg_ref, fn_ref) = refs[n_src:]
    x = _stream_tile(refs[:n_src])
    tm = x.shape[0]
    xn = x * lax.rsqrt(jnp.mean(x * x, axis=-1, keepdims=True) + EPS) * g1_ref[...]
    xm = _modulated(tm, n_lat, modc_ref, modb_ref,
                    lambda mod, rows: (rows * (1.0 + mod[1:2, :]) + mod[0:1, :]).astype(BF16), xn)

    hg_ref[0] = _dot(xm, w_in_ref[:, P_HG:P_FN]).astype(BF16)
    fn_ref[0] = _dot(xm, w_in_ref[:, P_FN:P_WIDTH]).astype(BF16)
    p = _dot(xm, w_in_ref[:, 0:P_HG])

    cos = cos_ref[...]
    sin = sin_ref[...]

    cq = p[:, P_CQ:P_CKV]
    cqn = (cq * lax.rsqrt(jnp.mean(cq * cq, axis=-1, keepdims=True) + EPS) * qg_ref[...]).astype(BF16)
    qq = _dot(cqn, wq_ref[...])
    nw = MLA_HEADS * MLA_NOPE
    for h in range(MLA_HEADS):
        q_ref[0, :, h * HEAD_PAD:h * HEAD_PAD + 128] = (qq[:, h * 128:(h + 1) * 128] * q_scale).astype(BF16)
        pe = qq[:, nw + h * 128:nw + (h + 1) * 128] * cos + qq[:, 2 * nw + h * 128:2 * nw + (h + 1) * 128] * sin
        q_ref[0, :, h * HEAD_PAD + 128:(h + 1) * HEAD_PAD] = (pe * q_scale).astype(BF16)

    ckv = p[:, P_CKV:P_KPA]
    ckvn = (ckv * lax.rsqrt(jnp.mean(ckv * ckv, axis=-1, keepdims=True) + EPS) * kvg_ref[...]).astype(BF16)
    kv = _dot(ckvn, wk_ref[...])
    vt_ref[0] = _dot_nt(wvt_ref[...], ckvn).astype(BF16)
    kpe = (p[:, P_KPA:P_KPB] * cos + p[:, P_KPB:P_HG] * sin).astype(BF16)
    for h in range(MLA_HEADS):
        k_ref[0, :, h * HEAD_PAD:h * HEAD_PAD + 128] = kv[:, h * 128:(h + 1) * 128].astype(BF16)
        k_ref[0, :, h * HEAD_PAD + 128:(h + 1) * HEAD_PAD] = kpe


def _inproj(src, mod, g1, w_in_p, qg, wq_all, kvg, wk_p, wvt_p, cos_t, sin_t, n_lat, ctx_row):
    s = cos_t.shape[0]
    d = g1.shape[0]
    b = (src[0] if isinstance(src, tuple) else src).shape[0]
    tm = STREAM_TILE
    row = lambda w: pl.BlockSpec((1, tm, w), lambda i, j: (i, j, 0))
    src_specs, src_args = _stream_specs(src, tm)
    return pl.pallas_call(
        functools.partial(_inproj_kernel, q_scale=math.log2(math.e) / math.sqrt(MLA_NOPE + MLA_ROPE), n_lat=n_lat,
                          n_src=len(src_args)),
        grid=(b, s // tm),
        in_specs=src_specs + [
            pl.BlockSpec((1, 6, d), lambda i, j: (i, 0, 0)),
            pl.BlockSpec((1, 6, d), lambda i, j: (ctx_row, 0, 0)),
            _const_spec((1, d)),
            _RESIDENT,
            _const_spec((1, MLA_Q_LORA)),
            _RESIDENT,
            _const_spec((1, MLA_KV_LORA)),
            _RESIDENT,
            _RESIDENT,
            pl.BlockSpec((tm, 128), lambda i, j: (j, 0)),
            pl.BlockSpec((tm, 128), lambda i, j: (j, 0)),
        ],
        out_specs=[row(MLA_HEADS * HEAD_PAD), row(MLA_HEADS * HEAD_PAD),
                   pl.BlockSpec((1, MLA_WIDTH, tm), lambda i, j: (i, 0, j)), row(HG_IN_WIDTH), row(FN_WIDTH)],
        out_shape=[
            jax.ShapeDtypeStruct((b, s, MLA_HEADS * HEAD_PAD), BF16),
            jax.ShapeDtypeStruct((b, s, MLA_HEADS * HEAD_PAD), BF16),
            jax.ShapeDtypeStruct((b, MLA_WIDTH, s), BF16),
            jax.ShapeDtypeStruct((b, s, HG_IN_WIDTH), BF16),
            jax.ShapeDtypeStruct((b, s, FN_WIDTH), BF16),
        ],
        compiler_params=_params(2),
        name="inproj",
    )(*src_args, mod, mod, g1.reshape(1, d), w_in_p, qg.reshape(1, -1), wq_all, kvg.reshape(1, -1), wk_p, wvt_p,
      cos_t, sin_t)


def _attend(q_ref, k_ref, vt_ref, o_ref, key_lo, key_hi):
    def scores(h):
        qh = q_ref[0, :, h * HEAD_PAD:(h + 1) * HEAD_PAD]
        kh = k_ref[0, key_lo:key_hi, h * HEAD_PAD:(h + 1) * HEAD_PAD]
        return _dot_nt(kh, qh)

    all_scores = [scores(h) for h in range(MLA_HEADS)]
    probs = []
    for st in all_scores:
        e = jnp.exp2(st - jnp.max(st, axis=0, keepdims=True))
        probs.append((e.astype(BF16), 1.0 / jnp.sum(e, axis=0, keepdims=True)))
    for h, (e, inv_denom) in enumerate(probs):
        vth = vt_ref[0, h * MLA_V:(h + 1) * MLA_V, key_lo:key_hi]
        ot = _dot(vth, e) * inv_denom
        o_ref[0, :, h * MLA_V:(h + 1) * MLA_V] = ot.T.astype(BF16)


def _attn_kernel(q_ref, k_ref, vt_ref, o_ref):
    _attend(q_ref, k_ref, vt_ref, o_ref, 0, k_ref.shape[1])


def _attn_ctx_kernel(q_ref, qc_ref, k_ref, vt_ref, o_ref, oc_ref, *, n_lat):
    s_len = k_ref.shape[1]
    _attend(q_ref, k_ref, vt_ref, o_ref, 0, s_len)

    @pl.when(pl.program_id(1) == pl.num_programs(1) - 1)
    def _():
        _attend(qc_ref, k_ref, vt_ref, oc_ref, n_lat, s_len)


def _attention(q, k, vt, n_lat, with_ctx_queries):
    b, s, _ = q.shape
    tq = ATTN_TILE
    n_ctx = s - n_lat
    q_spec = pl.BlockSpec((1, tq, q.shape[2]), lambda i, j: (i, j, 0))
    kv_specs = [pl.BlockSpec((1, s, k.shape[2]), lambda i, j: (i, 0, 0)),
                pl.BlockSpec((1, vt.shape[1], s), lambda i, j: (i, 0, 0))]
    o_spec = pl.BlockSpec((1, tq, MLA_WIDTH), lambda i, j: (i, j, 0))
    o_shape = jax.ShapeDtypeStruct((b, n_lat, MLA_WIDTH), BF16)
    if not with_ctx_queries:
        out = pl.pallas_call(
            _attn_kernel,
            grid=(b, n_lat // tq),
            in_specs=[q_spec] + kv_specs,
            out_specs=o_spec,
            out_shape=o_shape,
            compiler_params=_params(2),
            name="attention",
        )(q, k, vt)
        return out, None
    assert n_lat % n_ctx == 0
    return pl.pallas_call(
        functools.partial(_attn_ctx_kernel, n_lat=n_lat),
        grid=(b, n_lat // tq),
        in_specs=[q_spec, pl.BlockSpec((1, n_ctx, q.shape[2]), lambda i, j: (i, n_lat // n_ctx, 0))] + kv_specs,
        out_specs=[o_spec, pl.BlockSpec((1, n_ctx, MLA_WIDTH), lambda i, j: (i, 0, 0))],
        out_shape=[o_shape, jax.ShapeDtypeStruct((b, n_ctx, MLA_WIDTH), BF16)],
        compiler_params=_params(2),
        name="attention_ctx",
    )(q, q, k, vt)


def _hgrn_constants():
    c = HG_CHUNK
    t = np.arange(c)[:, None]
    u = np.arange(c)[None, :]
    blocks = [(u <= t)]
    for m in (8, 4):
        half = m // 2
        mid = (t // m) * m + half
        lower = (t % m) >= half
        blocks.append(np.where(lower, (u >= mid) & (u <= t), (u >= t + 1) & (u <= mid - 1)))
    mf = np.stack([blk.astype(np.float32) for blk in blocks])
    mb = np.ascontiguousarray(mf[:, ::-1, ::-1])
    msb = np.floor(np.log2(np.maximum(t ^ u, 1))).astype(np.int32)
    lvl = np.where(t > u, HG_LEVELS - 1 - msb, np.where(t == u, HG_LEVELS, -1)).astype(np.int32)
    lvl_f = np.tile(lvl, (1, 2))
    lvl_b = np.tile(np.ascontiguousarray(lvl.T), (1, 2))
    mf, mb = (np.tile(m.reshape(-1, c), (1, 2)) for m in (mf, mb))
    return mf, mb, lvl_f, lvl_b


def _pair_weights(x, p):
    zeros = jnp.zeros((x.shape[0], HG_K), x.dtype)
    xa = x[:, 2 * p * HG_K:(2 * p + 1) * HG_K]
    xb = x[:, (2 * p + 1) * HG_K:(2 * p + 2) * HG_K]
    blk = jnp.concatenate([jnp.concatenate([xa, zeros], axis=1), jnp.concatenate([zeros, xb], axis=1)], axis=0)
    return blk.T


def _hgrn_gates(z, lb, m_ref, with_output):
    f = lb + (1.0 - lb) * _sigmoid(z)
    g = jnp.log2(f)
    g_hi = g.astype(BF16)
    g_lo = (g - g_hi.astype(F32)).astype(BF16)
    m_all = m_ref[...] if with_output else m_ref[0:HG_CHUNK, :]
    return f, _dot(m_all, jnp.concatenate([g_hi, g_lo], axis=0))


def _hgrn_decays(hq, f, eb, fwd, with_output):
    c = HG_CHUNK
    kk = 1.0 - f
    bc = eb[0:c]
    total = bc[c - 1:c] if fwd else bc[0:1]
    k_up = (kk * jnp.exp2(total - bc)).astype(BF16)
    chunk_decay = jnp.exp2(total)
    if not with_output:
        return k_up, chunk_decay, None
    qq = _silu(hq)
    row = lax.broadcasted_iota(jnp.int32, (c, HG_KW), 0)
    xs = []
    for l in range(HG_LEVELS):
        m = c >> l
        half = m // 2
        bit = (row & half) != 0
        is_query = bit if fwd else jnp.logical_not(bit)
        if m == 2:
            xs.append(jnp.where(is_query, qq * f, kk).astype(BF16))
            continue
        if m >= 16:
            refs = [bc[b0 + half - 1:b0 + half] if fwd else bc[b0 + half:b0 + half + 1] for b0 in range(0, c, m)]
            ref = jnp.concatenate([jnp.broadcast_to(r, (m, HG_KW)) for r in refs], axis=0)
            diff = bc - ref
            expo = jnp.where(is_query, diff, -diff)
        else:
            idx = 1 + l - (HG_LEVELS - 1 - HG_MATMUL_LEVELS)
            expo = eb[idx * c:(idx + 1) * c]
        xs.append((jnp.where(is_query, qq, kk) * jnp.exp2(expo)).astype(BF16))
    q_in = (qq * jnp.exp2(bc)).astype(BF16)
    return k_up, chunk_decay, (xs, qq.astype(BF16), kk.astype(BF16), q_in)


def _hgrn_scores(operands, lvl_masks):
    xs, qb, kb, _ = operands
    scores = []
    for p in range(HG_HEADS // 2):
        kp = slice(2 * p * HG_K, (2 * p + 2) * HG_K)
        sc = jnp.where(lvl_masks[HG_LEVELS], _dot(qb[:, kp], _pair_weights(kb, p)), 0.0)
        for l in range(HG_LEVELS):
            sc = jnp.where(lvl_masks[l], _dot(xs[l][:, kp], _pair_weights(xs[l], p)), sc)
        scores.append(sc.astype(BF16))
    return scores


def _hgrn_apply(scores, operands, k_up, chunk_decay, vv, v_lo, v_hi, state_ref):
    outs = []
    for p in range(HG_HEADS // 2):
        kp = slice(2 * p * HG_K, (2 * p + 2) * HG_K)
        vp = slice(2 * p * HG_V, (2 * p + 2) * HG_V)
        if scores is not None:
            v_pair = vv[:, vp]
            v_diag = jnp.concatenate([v_pair * v_lo, v_pair * v_hi], axis=0)
            st_pair = state_ref[2 * p * HG_V:(2 * p + 2) * HG_V, kp]
            outs.append(_dot(scores[p], v_diag) + _dot(operands[3][:, kp], st_pair.astype(BF16).T))
        for h in (2 * p, 2 * p + 1):
            ks = slice(h * HG_K, (h + 1) * HG_K)
            vs = slice(h * HG_V, (h + 1) * HG_V)
            state_ref[vs, ks] = chunk_decay[:, ks] * state_ref[vs, ks] + _dot_tn(vv[:, vs], k_up[:, ks])
    return jnp.concatenate(outs, axis=1) if outs else None


def _hgrn_kernel(x_ref, lb_ref, gain_ref, mf_ref, mb_ref, lvlf_ref, lvlb_ref, ones_ref, o_ref,
                 of_ref, ob_ref, sf_ref, sb_ref, *, n_lat_chunks):
    c = HG_CHUNK
    s_len = x_ref.shape[1]
    n_chunks = s_len // c
    n_ctx_chunks = n_chunks - n_lat_chunks
    sf_ref[...] = jnp.zeros_like(sf_ref)
    sb_ref[...] = jnp.zeros_like(sb_ref)
    lvl_f = lvlf_ref[...]
    lvl_b = lvlb_ref[...]
    masks_f = [lvl_f == l for l in range(HG_LEVELS + 1)]
    masks_b = [lvl_b == l for l in range(HG_LEVELS + 1)]
    lb_f = lb_ref[0:1, :]
    lb_b = lb_ref[1:2, :]
    v_lane = lax.broadcasted_iota(jnp.int32, (c, 2 * HG_V), 1)
    v_lo = (v_lane < HG_V).astype(F32).astype(BF16)
    v_hi = (v_lane >= HG_V).astype(F32).astype(BF16)

    def load(ci):
        r0 = pl.multiple_of(ci * c, c)
        hq = x_ref[0, pl.ds(r0, c), 0:HG_KW].astype(F32)
        vv = x_ref[0, pl.ds(r0, c), 3 * HG_KW:3 * HG_KW + HG_WIDTH]
        return r0, hq, vv

    def scan(first_fwd, first_bwd, n, with_output):
        per_trip = HG_UNROLL if n % HG_UNROLL == 0 else n

        def step(i, carry):
            gated = []
            for u in range(per_trip):
                for fwd in (True, False):
                    ci = first_fwd + i * per_trip + u if fwd else first_bwd - i * per_trip - u
                    r0, hq, vv = load(ci)
                    zcol = HG_KW if fwd else 2 * HG_KW
                    z = x_ref[0, pl.ds(r0, c), zcol:zcol + HG_KW].astype(F32)
                    gated.append((fwd, r0, hq, vv, _hgrn_gates(z, lb_f if fwd else lb_b, mf_ref if fwd else mb_ref,
                                                               with_output)))
            jobs = [(fwd, r0, vv, _hgrn_decays(hq, f, eb, fwd, with_output)) for fwd, r0, hq, vv, (f, eb) in gated]
            scores = [_hgrn_scores(ph[2], masks_f if fwd else masks_b) if with_output else None
                      for fwd, _, _, ph in jobs]
            for (fwd, r0, vv, (k_up, chunk_decay, operands)), sc in zip(jobs, scores):
                out = _hgrn_apply(sc, operands, k_up, chunk_decay, vv, v_lo, v_hi, sf_ref if fwd else sb_ref)
                if with_output:
                    (of_ref if fwd else ob_ref)[pl.ds(r0, c), :] = out
            return carry

        lax.fori_loop(0, n // per_trip, step, 0)

    scan(n_lat_chunks, n_chunks - 1, n_ctx_chunks, o_ref.shape[1] == s_len)
    scan(0, n_lat_chunks - 1, n_lat_chunks, True)

    rt = 256
    gain = gain_ref[...]
    ones = ones_ref[...]

    def readout(i, carry):
        r0 = pl.multiple_of(i * rt, rt)
        o = of_ref[pl.ds(r0, rt), :] + ob_ref[pl.ds(r0, rt), :]
        sq_hi, sq_mid, sq_lo = _split3(o * o)
        msq = _dot(sq_hi, ones) + _dot(sq_mid, ones) + _dot(sq_lo, ones)
        zg = x_ref[0, pl.ds(r0, rt), 3 * HG_KW + HG_WIDTH:HG_IN_WIDTH].astype(F32)
        y = o * lax.rsqrt(msq + EPS) * gain * _silu(zg)
        o_ref[0, pl.ds(r0, rt), :] = y.astype(BF16)
        return carry

    lax.fori_loop(0, o_ref.shape[1] // rt, readout, 0)


def _hgrn(hg_in, lb, gain, n_lat, with_ctx_outputs):
    b, s, w = hg_in.shape
    mf, mb, lvl_f, lvl_b = _hgrn_constants()
    ones = np.kron(np.eye(HG_HEADS, dtype=np.float32), np.full((HG_V, HG_V), 1.0 / HG_V, np.float32))
    n_out = s if with_ctx_outputs else n_lat
    return pl.pallas_call(
        functools.partial(_hgrn_kernel, n_lat_chunks=n_lat // HG_CHUNK),
        grid=(b,),
        in_specs=[
            pl.BlockSpec((1, s, w), lambda i: (i, 0, 0)),
            _const_spec((2, HG_KW)),
            _const_spec((1, HG_WIDTH)),
            _const_spec(mf.shape),
            _const_spec(mb.shape),
            _const_spec(lvl_f.shape),
            _const_spec(lvl_b.shape),
            _const_spec(ones.shape),
        ],
        out_specs=pl.BlockSpec((1, n_out, HG_WIDTH), lambda i: (i, 0, 0)),
        out_shape=jax.ShapeDtypeStruct((b, n_out, HG_WIDTH), BF16),
        scratch_shapes=[
            pltpu.VMEM((s, HG_WIDTH), F32),
            pltpu.VMEM((s, HG_WIDTH), F32),
            pltpu.VMEM((HG_WIDTH, HG_KW), F32),
            pltpu.VMEM((HG_WIDTH, HG_KW), F32),
        ],
        compiler_params=_params(1),
        name="hgrn2",
    )(hg_in, lb, gain.reshape(1, HG_WIDTH), jnp.asarray(mf, BF16), jnp.asarray(mb, BF16),
      jnp.asarray(lvl_f), jnp.asarray(lvl_b), jnp.asarray(ones, BF16))


def _dft_tables(n):
    idx = np.arange(n, dtype=np.int64)
    ang = 2.0 * np.pi * ((idx[:, None] * idx[None, :]) % n).astype(np.float64) / n
    return np.cos(ang).astype(np.float32), np.sin(ang).astype(np.float32)


def _fourier_tile(z, table, mix, w, norm):
    tn = table.shape[0] // 2
    zz = _dot(table, z)
    zcat = jnp.concatenate([zz[0:tn], zz[tn:2 * tn]], axis=1).astype(BF16)
    mixed = _dot(zcat, mix) * norm
    return _dot(mixed.astype(BF16), w).astype(BF16)


def _fourier_kernel(z_ref, tl_ref, tc_ref, mix_ref, w_ref, o_ref, *maybe_oc_ref, n_lat):
    s_len = z_ref.shape[1]
    n_ctx = s_len - n_lat
    mix = mix_ref[...]
    w = w_ref[...]
    table = tl_ref[pl.program_id(1)]
    o_ref[0] = _fourier_tile(z_ref[0, 0:n_lat, :], table, mix, w, 1.0 / math.sqrt(n_lat * FN_GROUP_DIM))
    if maybe_oc_ref:
        @pl.when(pl.program_id(1) == pl.num_programs(1) - 1)
        def _():
            maybe_oc_ref[0][0] = _fourier_tile(z_ref[0, n_lat:s_len, :], tc_ref[...], mix, w,
                                               1.0 / math.sqrt(n_ctx * FN_GROUP_DIM))


def _stacked_dft_table(n, tile):
    cos, sin = _dft_tables(n)
    return np.concatenate([cos.reshape(n // tile, tile, n), sin.reshape(n // tile, tile, n)], axis=1)


def _fourier_tables(n_lat, n_ctx):
    cgrp, sgrp = _dft_tables(FN_GROUP_DIM)
    eye = np.eye(FN_GROUPS, dtype=np.float32)
    mix = np.concatenate([np.kron(eye, cgrp), -np.kron(eye, sgrp)], axis=0)
    tables = (_stacked_dft_table(n_lat, FOURIER_TILE), _stacked_dft_table(n_ctx, n_ctx)[0], mix)
    return tuple(jnp.asarray(t).astype(BF16) for t in tables)


def _fourier(fn_in, tables, w_fourier, n_lat, with_ctx_outputs):
    b, s, w = fn_in.shape
    n_ctx = s - n_lat
    tn = FOURIER_TILE
    table_lat, table_ctx, mix = tables
    out_specs = [pl.BlockSpec((1, tn, FN_WIDTH), lambda i, j: (i, j, 0))]
    out_shape = [jax.ShapeDtypeStruct((b, n_lat, FN_WIDTH), BF16)]
    if with_ctx_outputs:
        out_specs.append(pl.BlockSpec((1, n_ctx, FN_WIDTH), lambda i, j: (i, 0, 0)))
        out_shape.append(jax.ShapeDtypeStruct((b, n_ctx, FN_WIDTH), BF16))
    outs = pl.pallas_call(
        functools.partial(_fourier_kernel, n_lat=n_lat),
        grid=(b, n_lat // tn),
        in_specs=[
            pl.BlockSpec((1, s, w), lambda i, j: (i, 0, 0)),
            _RESIDENT,
            _const_spec(table_ctx.shape),
            _const_spec(mix.shape),
            _const_spec(w_fourier.shape),
        ],
        out_specs=out_specs,
        out_shape=out_shape,
        compiler_params=_params(2),
        name="fourier",
    )(fn_in, table_lat, table_ctx, mix, w_fourier)
    return (outs[0], outs[1]) if with_ctx_outputs else (outs[0], None)


def _ffn_kernel(*refs, final_norm, n_lat, select_ctx, n_srcs):
    streams = []
    pos = 0
    for n in n_srcs:
        streams.append(_stream_tile(refs[pos:pos + n]))
        pos += n
    x, attn, hg, fn = streams
    modb_ref, modc_ref, g2_ref, wo_ref, wgu_ref, wd_ref, gf_ref, o_ref, act_ref = refs[pos:]
    tm = x.shape[0]

    modulated = functools.partial(_modulated, tm, n_lat, modc_ref, modb_ref, use_ctx=select_ctx)

    y = (_dot(attn, wo_ref[0:MLA_WIDTH, :])
         + _dot(hg, wo_ref[MLA_WIDTH:MLA_WIDTH + HG_WIDTH, :])
         + _dot(fn, wo_ref[MLA_WIDTH + HG_WIDTH:MLA_WIDTH + HG_WIDTH + FN_WIDTH, :]))
    x1 = modulated(lambda mod, xr, yr: xr + mod[2:3, :] * yr, x, y)
    o_ref[0] = x1
    xn = x1 * lax.rsqrt(jnp.mean(x1 * x1, axis=-1, keepdims=True) + EPS) * g2_ref[...]
    hm = modulated(lambda mod, rows: (rows * (1.0 + mod[4:5, :]) + mod[3:4, :]).astype(BF16), xn)

    ft = wgu_ref.shape[2] // 2
    for t in range(wgu_ref.shape[0]):
        gu = _dot(hm, wgu_ref[t])
        act_ref[:, t * ft:(t + 1) * ft] = (_silu(gu[:, 0:ft]) * gu[:, ft:2 * ft]).astype(BF16)
    out = modulated(lambda mod, xr, ar: xr + mod[5:6, :] * ar, o_ref[0], _dot(act_ref[...], wd_ref[...]))
    if final_norm:
        out = out * lax.rsqrt(jnp.mean(out * out, axis=-1, keepdims=True) + EPS) * gf_ref[...]
    o_ref[0] = out


def _outproj_ffn(x_src, attn, hg, fn, mod, g2, w_out, w_gu, w_d, g_final, n_lat, n_ctx, ctx_row, with_ctx_rows,
                 final_norm):
    d = g2.shape[0]
    b = (x_src[0] if isinstance(x_src, tuple) else x_src).shape[0]
    tm = STREAM_TILE if with_ctx_rows else LATENT_TILE
    n_rows = n_lat + n_ctx if with_ctx_rows else n_lat
    hidden = w_d.shape[0]
    row = lambda w: pl.BlockSpec((1, tm, w), lambda i, j: (i, j, 0))
    src_specs, src_args, n_srcs = [], [], []
    for src in (x_src, attn, hg, fn):
        specs, args = _stream_specs(src, tm)
        src_specs += specs
        src_args += args
        n_srcs.append(len(args))
    return pl.pallas_call(
        functools.partial(_ffn_kernel, final_norm=final_norm, n_lat=n_lat, select_ctx=with_ctx_rows,
                          n_srcs=tuple(n_srcs)),
        grid=(b, n_rows // tm),
        in_specs=src_specs + [
            pl.BlockSpec((1, 6, d), lambda i, j: (i, 0, 0)),
            pl.BlockSpec((1, 6, d), lambda i, j: (ctx_row, 0, 0)),
            _const_spec((1, d)),
            _RESIDENT,
            _RESIDENT,
            _RESIDENT,
            _const_spec((1, d)),
        ],
        out_specs=row(d),
        out_shape=jax.ShapeDtypeStruct((b, n_rows, d), F32),
        scratch_shapes=[pltpu.VMEM((tm, hidden), BF16)],
        compiler_params=_params(2),
        name="outproj_ffn",
    )(*src_args, mod, mod, g2.reshape(1, d), w_out, w_gu, w_d, g_final.reshape(1, d))


def _rotate_half_cols(w):
    r1, r2, c1, c2 = jnp.split(w, 4, axis=-1)
    return jnp.concatenate([-r2, r1, -c2, c1], axis=-1)


def _pack_w_in(w):
    d = w.shape[0]
    zeros = jnp.zeros((d, 128 - MLA_ROPE), w.dtype)
    w_kr = w[:, O_KR:O_HQ]
    return jnp.concatenate([w[:, O_CQ:O_KR], w_kr, zeros, _rotate_half_cols(w_kr), zeros, w[:, O_HQ:]],
                           axis=1).astype(BF16)


def _pack_w_uq(w):
    r = w.shape[0]
    wh = w.reshape(r, MLA_HEADS, MLA_NOPE + MLA_ROPE)
    nope = wh[:, :, :MLA_NOPE].reshape(r, MLA_HEADS * MLA_NOPE)
    pe = wh[:, :, MLA_NOPE:]
    zeros = jnp.zeros((r, MLA_HEADS, 128 - MLA_ROPE), w.dtype)
    pe_a = jnp.concatenate([pe, zeros], axis=-1).reshape(r, MLA_HEADS * 128)
    pe_b = jnp.concatenate([_rotate_half_cols(pe), zeros], axis=-1).reshape(r, MLA_HEADS * 128)
    return jnp.concatenate([nope, pe_a, pe_b], axis=1).astype(BF16)


def _pack_w_ukv(w):
    r = w.shape[0]
    wh = w.reshape(r, MLA_HEADS, MLA_NOPE + MLA_V)
    k_nope = wh[:, :, :MLA_NOPE].reshape(r, MLA_HEADS * MLA_NOPE)
    v = wh[:, :, MLA_NOPE:].reshape(r, MLA_HEADS * MLA_V)
    return k_nope.astype(BF16), v.T.astype(BF16)


def _pack_w_gate_up(w, hidden):
    d = w.shape[0]
    n_ft = hidden // FFN_TILE
    gate = w[:, :hidden].reshape(d, n_ft, FFN_TILE)
    up = w[:, hidden:].reshape(d, n_ft, FFN_TILE)
    return jnp.concatenate([gate, up], axis=2).transpose(1, 0, 2).astype(BF16)


def _rope_tables(n_lat, n_ctx):
    rows = n_lat // GRID_W
    row_pos = np.repeat(np.arange(rows, dtype=np.float32), GRID_W)
    col_pos = np.tile(np.arange(GRID_W, dtype=np.float32), rows)
    axis_dim = MLA_ROPE // 2
    inv_freq = (ROPE_BASE ** (-np.arange(0, axis_dim, 2, dtype=np.float32) / axis_dim)).astype(np.float32)
    ang_r = row_pos[:, None] * inv_freq
    ang_c = col_pos[:, None] * inv_freq
    ang = np.concatenate([ang_r, ang_r, ang_c, ang_c], axis=-1)
    cos = np.ones((n_lat + n_ctx, 128), np.float32)
    sin = np.zeros((n_lat + n_ctx, 128), np.float32)
    cos[:n_lat, :MLA_ROPE] = np.cos(ang)
    sin[:n_lat, :MLA_ROPE] = np.sin(ang)
    return jnp.asarray(cos), jnp.asarray(sin)


def kernel(x, c, ctx, c_ctx, w_mod, b_mod, norm1_g, norm2_g, w_in, q_norm_g, w_uq, kv_norm_g, w_ukv, lb_param,
           hg_norm_g, w_fourier, w_out, w_gate_up, w_down, final_norm_g):
    b, t, d = x.shape
    n_ctx = ctx.shape[1]
    s = t + n_ctx
    depth = w_mod.shape[0]
    hidden = w_down.shape[1]
    assert s % STREAM_TILE == 0 and t % LATENT_TILE == 0 and t % ATTN_TILE == 0 and t % FOURIER_TILE == 0
    assert hidden % FFN_TILE == 0 and t % GRID_W == 0 and n_ctx % HG_CHUNK == 0

    mod_rows = -(-(b + 1) // 8) * 8
    c_all = jnp.zeros((mod_rows, d), F32).at[:b].set(c).at[b].set(c_ctx)
    mod_all = _modulation(c_all, w_mod, b_mod).reshape(depth, mod_rows, 6, d)
    lower = _lower_bounds(lb_param)
    cos_t, sin_t = _rope_tables(t, n_ctx)
    dft_tables = _fourier_tables(t, n_ctx)

    xa = (x, ctx)
    for l in range(depth):
        last = l == depth - 1
        q, k, v, hg_in, fn_in = _inproj(xa, mod_all[l], norm1_g[l], _pack_w_in(w_in[l]), q_norm_g[l],
                                        _pack_w_uq(w_uq[l]), kv_norm_g[l], *_pack_w_ukv(w_ukv[l]), cos_t, sin_t, t, b)
        attn, attn_ctx = _attention(q, k, v, t, not last)
        hg = _hgrn(hg_in, lower[l], hg_norm_g[l], t, not last)
        fn, fn_ctx = _fourier(fn_in, dft_tables, w_fourier[l].astype(BF16), t, not last)
        if not last:
            attn, fn = (attn, attn_ctx), (fn, fn_ctx)
        xa = _outproj_ffn(xa, attn, hg, fn, mod_all[l], norm2_g[l], w_out[l].astype(BF16),
                          _pack_w_gate_up(w_gate_up[l], hidden), w_down[l].astype(BF16), final_norm_g, t, n_ctx, b,
                          not last, last)
    return xa
```

```python
import functools
import math

import numpy as np
import jax
import jax.numpy as jnp
from jax import lax
from jax.experimental import pallas as pl
from jax.experimental.pallas import tpu as pltpu

EPS = 1e-6
GRID_W = 64
ROPE_BASE = 10000.0

MLA_HEADS = 4
MLA_Q_LORA = 256
MLA_KV_LORA = 128
MLA_NOPE = 128
MLA_ROPE = 64
MLA_V = 128
MLA_WIDTH = MLA_HEADS * MLA_V
HEAD_PAD = 256

HG_HEADS = 4
HG_K = 128
HG_V = 64
HG_WIDTH = HG_HEADS * HG_V
HG_KW = HG_HEADS * HG_K
HG_CHUNK = 64
HG_LEVELS = 6
HG_MATMUL_LEVELS = 2
HG_UNROLL = 8

FN_GROUPS = 4
FN_GROUP_DIM = 64
FN_WIDTH = FN_GROUPS * FN_GROUP_DIM

O_CQ = 0
O_CKV = O_CQ + MLA_Q_LORA
O_KR = O_CKV + MLA_KV_LORA
O_HQ = O_KR + MLA_ROPE

P_CQ = 0
P_CKV = P_CQ + MLA_Q_LORA
P_KPA = P_CKV + MLA_KV_LORA
P_KPB = P_KPA + 128
P_HG = P_KPB + 128
HG_IN_WIDTH = 3 * HG_KW + 2 * HG_WIDTH
P_FN = P_HG + HG_IN_WIDTH
P_WIDTH = P_FN + FN_WIDTH

STREAM_TILE = 768
LATENT_TILE = 1024
ATTN_TILE = 512
FOURIER_TILE = 1024
FFN_TILE = 256
VMEM_LIMIT = 56 * 1024 * 1024

BF16 = jnp.bfloat16
F32 = jnp.float32


def _dot(a, b):
    return jnp.dot(a, b, preferred_element_type=F32)


def _dot_nt(a, b):
    return lax.dot_general(a, b, (((1,), (1,)), ((), ())), preferred_element_type=F32)


def _dot_tn(a, b):
    return lax.dot_general(a, b, (((0,), (0,)), ((), ())), preferred_element_type=F32)


def _silu(x):
    return x * (1.0 / (1.0 + jnp.exp(-x)))


def _sigmoid(x):
    return 1.0 / (1.0 + jnp.exp(-x))


def _split3(x):
    hi = x.astype(BF16)
    r1 = x - hi.astype(F32)
    mid = r1.astype(BF16)
    lo = (r1 - mid.astype(F32)).astype(BF16)
    return hi, mid, lo


def _const_spec(shape):
    nd = len(shape)
    return pl.BlockSpec(shape, lambda *_: (0,) * nd)


_RESIDENT = pl.BlockSpec(memory_space=pltpu.VMEM)


def _params(n_axes):
    return pltpu.CompilerParams(dimension_semantics=("arbitrary",) * n_axes, vmem_limit_bytes=VMEM_LIMIT)


def _mod_kernel(c_ref, w_ref, b_ref, o_ref):
    act = _silu(c_ref[...]).astype(BF16)
    o_ref[0] = _dot(act, w_ref[0].astype(BF16)) + b_ref[0]


def _modulation(c_all, w_mod, b_mod):
    depth, d, n6 = w_mod.shape
    rows = c_all.shape[0]
    tn = n6 // 6
    return pl.pallas_call(
        _mod_kernel,
        grid=(depth, n6 // tn),
        in_specs=[
            pl.BlockSpec((rows, d), lambda l, j: (0, 0)),
            pl.BlockSpec((1, d, tn), lambda l, j: (l, 0, j)),
            pl.BlockSpec((1, 1, tn), lambda l, j: (l, 0, j)),
        ],
        out_specs=pl.BlockSpec((1, rows, tn), lambda l, j: (l, 0, j)),
        out_shape=jax.ShapeDtypeStruct((depth, rows, n6), F32),
        name="modulation",
    )(c_all, w_mod, b_mod.reshape(depth, 1, n6))


def _lower_bound_kernel(lb_ref, o_ref):
    depth = lb_ref.shape[0]
    lp = [lb_ref[l] for l in range(depth)]
    m = lp[0]
    for l in range(1, depth):
        m = jnp.maximum(m, lp[l])
    e = [jnp.exp(v - m) for v in lp]
    tot = e[0]
    for l in range(1, depth):
        tot = tot + e[l]
    probs = [v / tot for v in e]
    cum = probs[0]
    o_ref[0] = cum - probs[0]
    for l in range(1, depth):
        cum = cum + probs[l]
        o_ref[l] = cum - probs[0]


def _lower_bounds(lb_param):
    return pl.pallas_call(
        _lower_bound_kernel,
        out_shape=jax.ShapeDtypeStruct(lb_param.shape, F32),
        name="hgrn_lower_bounds",
    )(lb_param.astype(F32))


def _modulated(tile_rows, n_lat, modc_ref, modb_ref, fn, *arrays, use_ctx=True):
    lat = modb_ref[0]
    if not use_ctx:
        return fn(lat, *arrays)
    tail = n_lat % tile_rows
    if tail == 0:
        return fn(jnp.where(pl.program_id(1) * tile_rows >= n_lat, modc_ref[0], lat), *arrays)
    is_last = pl.program_id(1) == pl.num_programs(1) - 1
    mixed = jnp.where(is_last, modc_ref[0], lat)
    return jnp.concatenate([fn(lat, *[a[0:tail] for a in arrays]),
                            fn(mixed, *[a[tail:tile_rows] for a in arrays])], axis=0)


def _stream_specs(src, tm):
    if not isinstance(src, tuple):
        return [pl.BlockSpec((1, tm, src.shape[2]), lambda i, j: (i, j, 0))], [src]
    lat, ctx = src
    t, d = lat.shape[1:]
    n_ctx = ctx.shape[1]
    n_full = t // tm
    tail = t - n_full * tm
    assert n_full >= 1 and tail > 0 and tail + n_ctx == tm and (n_full * tm) % tail == 0
    specs = [pl.BlockSpec((1, tm, d), lambda i, j: (i, jnp.minimum(j, n_full - 1), 0)),
             pl.BlockSpec((1, tail, d), lambda i, j: (i, (n_full * tm) // tail, 0)),
             pl.BlockSpec((1, n_ctx, d), lambda i, j: (i, 0, 0))]
    return specs, [lat, lat, ctx]


def _stream_tile(refs):
    if len(refs) == 1:
        return refs[0][0]
    main_ref, tail_ref, ctx_ref = refs
    mixed = jnp.concatenate([tail_ref[0], ctx_ref[0]], axis=0)
    return jnp.where(pl.program_id(1) < pl.num_programs(1) - 1, main_ref[0], mixed)


def _inproj_kernel(*refs, q_scale, n_lat, n_src):
    (modb_ref, modc_ref, g1_ref, w_in_ref, qg_ref, wq_ref, kvg_ref, wk_ref, wvt_ref, cos_ref, sin_ref,
     q_ref, k_ref, vt_ref, hg_ref, fn_ref) = refs[n_src:]
    x = _stream_tile(refs[:n_src])
    tm = x.shape[0]
    xn = x * lax.rsqrt(jnp.mean(x * x, axis=-1, keepdims=True) + EPS) * g1_ref[...]
    xm = _modulated(tm, n_lat, modc_ref, modb_ref,
                    lambda mod, rows: (rows * (1.0 + mod[1:2, :]) + mod[0:1, :]).astype(BF16), xn)

    hg_ref[0] = _dot(xm, w_in_ref[:, P_HG:P_FN]).astype(BF16)
    fn_ref[0] = _dot(xm, w_in_ref[:, P_FN:P_WIDTH]).astype(BF16)
    p = _dot(xm, w_in_ref[:, 0:P_HG])

    cos = cos_ref[...]
    sin = sin_ref[...]

    cq = p[:, P_CQ:P_CKV]
    cqn = (cq * lax.rsqrt(jnp.mean(cq * cq, axis=-1, keepdims=True) + EPS) * qg_ref[...]).astype(BF16)
    qq = _dot(cqn, wq_ref[...])
    nw = MLA_HEADS * MLA_NOPE
    for h in range(MLA_HEADS):
        q_ref[0, :, h * HEAD_PAD:h * HEAD_PAD + 128] = (qq[:, h * 128:(h + 1) * 128] * q_scale).astype(BF16)
        pe = qq[:, nw + h * 128:nw + (h + 1) * 128] * cos + qq[:, 2 * nw + h * 128:2 * nw + (h + 1) * 128] * sin
        q_ref[0, :, h * HEAD_PAD + 128:(h + 1) * HEAD_PAD] = (pe * q_scale).astype(BF16)

    ckv = p[:, P_CKV:P_KPA]
    ckvn = (ckv * lax.rsqrt(jnp.mean(ckv * ckv, axis=-1, keepdims=True) + EPS) * kvg_ref[...]).astype(BF16)
    kv = _dot(ckvn, wk_ref[...])
    vt_ref[0] = _dot_nt(wvt_ref[...], ckvn).astype(BF16)
    kpe = (p[:, P_KPA:P_KPB] * cos + p[:, P_KPB:P_HG] * sin).astype(BF16)
    for h in range(MLA_HEADS):
        k_ref[0, :, h * HEAD_PAD:h * HEAD_PAD + 128] = kv[:, h * 128:(h + 1) * 128].astype(BF16)
        k_ref[0, :, h * HEAD_PAD + 128:(h + 1) * HEAD_PAD] = kpe


def _inproj(src, mod, g1, w_in_p, qg, wq_all, kvg, wk_p, wvt_p, cos_t, sin_t, n_lat, ctx_row):
    s = cos_t.shape[0]
    d = g1.shape[0]
    b = (src[0] if isinstance(src, tuple) else src).shape[0]
    tm = STREAM_TILE
    row = lambda w: pl.BlockSpec((1, tm, w), lambda i, j: (i, j, 0))
    src_specs, src_args = _stream_specs(src, tm)
    return pl.pallas_call(
        functools.partial(_inproj_kernel, q_scale=math.log2(math.e) / math.sqrt(MLA_NOPE + MLA_ROPE), n_lat=n_lat,
                          n_src=len(src_args)),
        grid=(b, s // tm),
        in_specs=src_specs + [
            pl.BlockSpec((1, 6, d), lambda i, j: (i, 0, 0)),
            pl.BlockSpec((1, 6, d), lambda i, j: (ctx_row, 0, 0)),
            _const_spec((1, d)),
            _RESIDENT,
            _const_spec((1, MLA_Q_LORA)),
            _RESIDENT,
            _const_spec((1, MLA_KV_LORA)),
            _RESIDENT,
            _RESIDENT,
            pl.BlockSpec((tm, 128), lambda i, j: (j, 0)),
            pl.BlockSpec((tm, 128), lambda i, j: (j, 0)),
        ],
        out_specs=[row(MLA_HEADS * HEAD_PAD), row(MLA_HEADS * HEAD_PAD),
                   pl.BlockSpec((1, MLA_WIDTH, tm), lambda i, j: (i, 0, j)), row(HG_IN_WIDTH), row(FN_WIDTH)],
        out_shape=[
            jax.ShapeDtypeStruct((b, s, MLA_HEADS * HEAD_PAD), BF16),
            jax.ShapeDtypeStruct((b, s, MLA_HEADS * HEAD_PAD), BF16),
            jax.ShapeDtypeStruct((b, MLA_WIDTH, s), BF16),
            jax.ShapeDtypeStruct((b, s, HG_IN_WIDTH), BF16),
            jax.ShapeDtypeStruct((b, s, FN_WIDTH), BF16),
        ],
        compiler_params=_params(2),
        name="inproj",
    )(*src_args, mod, mod, g1.reshape(1, d), w_in_p, qg.reshape(1, -1), wq_all, kvg.reshape(1, -1), wk_p, wvt_p,
      cos_t, sin_t)


def _attend(q_ref, k_ref, vt_ref, o_ref, key_lo, key_hi):
    def scores(h):
        qh = q_ref[0, :, h * HEAD_PAD:(h + 1) * HEAD_PAD]
        kh = k_ref[0, key_lo:key_hi, h * HEAD_PAD:(h + 1) * HEAD_PAD]
        return _dot_nt(kh, qh)

    all_scores = [scores(h) for h in range(MLA_HEADS)]
    probs = []
    for st in all_scores:
        e = jnp.exp2(st - jnp.max(st, axis=0, keepdims=True))
        probs.append((e.astype(BF16), 1.0 / jnp.sum(e, axis=0, keepdims=True)))
    for h, (e, inv_denom) in enumerate(probs):
        vth = vt_ref[0, h * MLA_V:(h + 1) * MLA_V, key_lo:key_hi]
        ot = _dot(vth, e) * inv_denom
        o_ref[0, :, h * MLA_V:(h + 1) * MLA_V] = ot.T.astype(BF16)


def _attn_kernel(q_ref, k_ref, vt_ref, o_ref):
    _attend(q_ref, k_ref, vt_ref, o_ref, 0, k_ref.shape[1])


def _attn_ctx_kernel(q_ref, qc_ref, k_ref, vt_ref, o_ref, oc_ref, *, n_lat):
    s_len = k_ref.shape[1]
    _attend(q_ref, k_ref, vt_ref, o_ref, 0, s_len)

    @pl.when(pl.program_id(1) == pl.num_programs(1) - 1)
    def _():
        _attend(qc_ref, k_ref, vt_ref, oc_ref, n_lat, s_len)


def _attention(q, k, vt, n_lat, with_ctx_queries):
    b, s, _ = q.shape
    tq = ATTN_TILE
    n_ctx = s - n_lat
    q_spec = pl.BlockSpec((1, tq, q.shape[2]), lambda i, j: (i, j, 0))
    kv_specs = [pl.BlockSpec((1, s, k.shape[2]), lambda i, j: (i, 0, 0)),
                pl.BlockSpec((1, vt.shape[1], s), lambda i, j: (i, 0, 0))]
    o_spec = pl.BlockSpec((1, tq, MLA_WIDTH), lambda i, j: (i, j, 0))
    o_shape = jax.ShapeDtypeStruct((b, n_lat, MLA_WIDTH), BF16)
    if not with_ctx_queries:
        out = pl.pallas_call(
            _attn_kernel,
            grid=(b, n_lat // tq),
            in_specs=[q_spec] + kv_specs,
            out_specs=o_spec,
            out_shape=o_shape,
            compiler_params=_params(2),
            name="attention",
        )(q, k, vt)
        return out, None
    assert n_lat % n_ctx == 0
    return pl.pallas_call(
        functools.partial(_attn_ctx_kernel, n_lat=n_lat),
        grid=(b, n_lat // tq),
        in_specs=[q_spec, pl.BlockSpec((1, n_ctx, q.shape[2]), lambda i, j: (i, n_lat // n_ctx, 0))] + kv_specs,
        out_specs=[o_spec, pl.BlockSpec((1, n_ctx, MLA_WIDTH), lambda i, j: (i, 0, 0))],
        out_shape=[o_shape, jax.ShapeDtypeStruct((b, n_ctx, MLA_WIDTH), BF16)],
        compiler_params=_params(2),
        name="attention_ctx",
    )(q, q, k, vt)


def _hgrn_constants():
    c = HG_CHUNK
    t = np.arange(c)[:, None]
    u = np.arange(c)[None, :]
    blocks = [(u <= t)]
    for m in (8, 4):
        half = m // 2
        mid = (t // m) * m + half
        lower = (t % m) >= half
        blocks.append(np.where(lower, (u >= mid) & (u <= t), (u >= t + 1) & (u <= mid - 1)))
    mf = np.stack([blk.astype(np.float32) for blk in blocks])
    mb = np.ascontiguousarray(mf[:, ::-1, ::-1])
    msb = np.floor(np.log2(np.maximum(t ^ u, 1))).astype(np.int32)
    lvl = np.where(t > u, HG_LEVELS - 1 - msb, np.where(t == u, HG_LEVELS, -1)).astype(np.int32)
    lvl_f = np.tile(lvl, (1, 2))
    lvl_b = np.tile(np.ascontiguousarray(lvl.T), (1, 2))
    return mf.reshape(-1, c), mb.reshape(-1, c), lvl_f, lvl_b


def _pair_weights(x, p):
    zeros = jnp.zeros((x.shape[0], HG_K), x.dtype)
    xa = x[:, 2 * p * HG_K:(2 * p + 1) * HG_K]
    xb = x[:, (2 * p + 1) * HG_K:(2 * p + 2) * HG_K]
    blk = jnp.concatenate([jnp.concatenate([xa, zeros], axis=1), jnp.concatenate([zeros, xb], axis=1)], axis=0)
    return blk.T


def _hgrn_gates(z, lb, m_ref, with_output):
    f = lb + (1.0 - lb) * _sigmoid(z)
    g = jnp.log2(f)
    g_hi = g.astype(BF16)
    g_lo = (g - g_hi.astype(F32)).astype(BF16)
    m_all = m_ref[...] if with_output else m_ref[0:HG_CHUNK, :]
    return f, _dot(m_all, g_hi) + _dot(m_all, g_lo)


def _hgrn_decays(hq, f, eb, fwd, with_output):
    c = HG_CHUNK
    kk = 1.0 - f
    bc = eb[0:c]
    total = bc[c - 1:c] if fwd else bc[0:1]
    k_up = (kk * jnp.exp2(total - bc)).astype(BF16)
    chunk_decay = jnp.exp2(total)
    if not with_output:
        return k_up, chunk_decay, None
    qq = _silu(hq)
    row = lax.broadcasted_iota(jnp.int32, (c, HG_KW), 0)
    xs = []
    for l in range(HG_LEVELS):
        m = c >> l
        half = m // 2
        bit = (row & half) != 0
        is_query = bit if fwd else jnp.logical_not(bit)
        if m == 2:
            xs.append(jnp.where(is_query, qq * f, kk).astype(BF16))
            continue
        if m >= 16:
            refs = [bc[b0 + half - 1:b0 + half] if fwd else bc[b0 + half:b0 + half + 1] for b0 in range(0, c, m)]
            ref = jnp.concatenate([jnp.broadcast_to(r, (m, HG_KW)) for r in refs], axis=0)
            diff = bc - ref
            expo = jnp.where(is_query, diff, -diff)
        else:
            idx = 1 + l - (HG_LEVELS - 1 - HG_MATMUL_LEVELS)
            expo = eb[idx * c:(idx + 1) * c]
        xs.append((jnp.where(is_query, qq, kk) * jnp.exp2(expo)).astype(BF16))
    q_in = (qq * jnp.exp2(bc)).astype(BF16)
    return k_up, chunk_decay, (xs, qq.astype(BF16), kk.astype(BF16), q_in)


def _hgrn_scores(operands, lvl_masks):
    xs, qb, kb, _ = operands
    scores = []
    for p in range(HG_HEADS // 2):
        kp = slice(2 * p * HG_K, (2 * p + 2) * HG_K)
        sc = jnp.where(lvl_masks[HG_LEVELS], _dot(qb[:, kp], _pair_weights(kb, p)), 0.0)
        for l in range(HG_LEVELS):
            sc = jnp.where(lvl_masks[l], _dot(xs[l][:, kp], _pair_weights(xs[l], p)), sc)
        scores.append(sc.astype(BF16))
    return scores


def _hgrn_apply(scores, operands, k_up, chunk_decay, vv, v_lo, v_hi, state_ref):
    outs = []
    for p in range(HG_HEADS // 2):
        kp = slice(2 * p * HG_K, (2 * p + 2) * HG_K)
        vp = slice(2 * p * HG_V, (2 * p + 2) * HG_V)
        if scores is not None:
            v_pair = vv[:, vp]
            v_diag = jnp.concatenate([v_pair * v_lo, v_pair * v_hi], axis=0)
            st_pair = state_ref[2 * p * HG_V:(2 * p + 2) * HG_V, kp]
            outs.append(_dot(scores[p], v_diag) + _dot(operands[3][:, kp], st_pair.astype(BF16).T))
        for h in (2 * p, 2 * p + 1):
            ks = slice(h * HG_K, (h + 1) * HG_K)
            vs = slice(h * HG_V, (h + 1) * HG_V)
            state_ref[vs, ks] = chunk_decay[:, ks] * state_ref[vs, ks] + _dot_tn(vv[:, vs], k_up[:, ks])
    return jnp.concatenate(outs, axis=1) if outs else None


def _hgrn_kernel(x_ref, lb_ref, gain_ref, mf_ref, mb_ref, lvlf_ref, lvlb_ref, ones_ref, o_ref,
                 of_ref, ob_ref, sf_ref, sb_ref, *, n_lat_chunks):
    c = HG_CHUNK
    s_len = x_ref.shape[1]
    n_chunks = s_len // c
    n_ctx_chunks = n_chunks - n_lat_chunks
    sf_ref[...] = jnp.zeros_like(sf_ref)
    sb_ref[...] = jnp.zeros_like(sb_ref)
    lvl_f = lvlf_ref[...]
    lvl_b = lvlb_ref[...]
    masks_f = [lvl_f == l for l in range(HG_LEVELS + 1)]
    masks_b = [lvl_b == l for l in range(HG_LEVELS + 1)]
    lb_f = lb_ref[0:1, :]
    lb_b = lb_ref[1:2, :]
    v_lane = lax.broadcasted_iota(jnp.int32, (c, 2 * HG_V), 1)
    v_lo = (v_lane < HG_V).astype(F32).astype(BF16)
    v_hi = (v_lane >= HG_V).astype(F32).astype(BF16)

    def load(ci):
        r0 = pl.multiple_of(ci * c, c)
        hq = x_ref[0, pl.ds(r0, c), 0:HG_KW].astype(F32)
        vv = x_ref[0, pl.ds(r0, c), 3 * HG_KW:3 * HG_KW + HG_WIDTH]
        return r0, hq, vv

    def scan(first_fwd, first_bwd, n, with_output):
        per_trip = HG_UNROLL if n % HG_UNROLL == 0 else n

        def step(i, carry):
            gated = []
            for u in range(per_trip):
                for fwd in (True, False):
                    ci = first_fwd + i * per_trip + u if fwd else first_bwd - i * per_trip - u
                    r0, hq, vv = load(ci)
                    zcol = HG_KW if fwd else 2 * HG_KW
                    z = x_ref[0, pl.ds(r0, c), zcol:zcol + HG_KW].astype(F32)
                    gated.append((fwd, r0, hq, vv, _hgrn_gates(z, lb_f if fwd else lb_b, mf_ref if fwd else mb_ref,
                                                               with_output)))
            jobs = [(fwd, r0, vv, _hgrn_decays(hq, f, eb, fwd, with_output)) for fwd, r0, hq, vv, (f, eb) in gated]
            scores = [_hgrn_scores(ph[2], masks_f if fwd else masks_b) if with_output else None
                      for fwd, _, _, ph in jobs]
            for (fwd, r0, vv, (k_up, chunk_decay, operands)), sc in zip(jobs, scores):
                out = _hgrn_apply(sc, operands, k_up, chunk_decay, vv, v_lo, v_hi, sf_ref if fwd else sb_ref)
                if with_output:
                    (of_ref if fwd else ob_ref)[pl.ds(r0, c), :] = out
            return carry

        lax.fori_loop(0, n // per_trip, step, 0)

    scan(n_lat_chunks, n_chunks - 1, n_ctx_chunks, o_ref.shape[1] == s_len)
    scan(0, n_lat_chunks - 1, n_lat_chunks, True)

    rt = 256
    gain = gain_ref[...]
    ones = ones_ref[...]

    def readout(i, carry):
        r0 = pl.multiple_of(i * rt, rt)
        o = of_ref[pl.ds(r0, rt), :] + ob_ref[pl.ds(r0, rt), :]
        sq_hi, sq_mid, sq_lo = _split3(o * o)
        msq = _dot(sq_hi, ones) + _dot(sq_mid, ones) + _dot(sq_lo, ones)
        zg = x_ref[0, pl.ds(r0, rt), 3 * HG_KW + HG_WIDTH:HG_IN_WIDTH].astype(F32)
        y = o * lax.rsqrt(msq + EPS) * gain * _silu(zg)
        o_ref[0, pl.ds(r0, rt), :] = y.astype(BF16)
        return carry

    lax.fori_loop(0, o_ref.shape[1] // rt, readout, 0)


def _hgrn(hg_in, lb, gain, n_lat, with_ctx_outputs):
    b, s, w = hg_in.shape
    mf, mb, lvl_f, lvl_b = _hgrn_constants()
    ones = np.kron(np.eye(HG_HEADS, dtype=np.float32), np.full((HG_V, HG_V), 1.0 / HG_V, np.float32))
    n_out = s if with_ctx_outputs else n_lat
    return pl.pallas_call(
        functools.partial(_hgrn_kernel, n_lat_chunks=n_lat // HG_CHUNK),
        grid=(b,),
        in_specs=[
            pl.BlockSpec((1, s, w), lambda i: (i, 0, 0)),
            _const_spec((2, HG_KW)),
            _const_spec((1, HG_WIDTH)),
            _const_spec(mf.shape),
            _const_spec(mb.shape),
            _const_spec(lvl_f.shape),
            _const_spec(lvl_b.shape),
            _const_spec(ones.shape),
        ],
        out_specs=pl.BlockSpec((1, n_out, HG_WIDTH), lambda i: (i, 0, 0)),
        out_shape=jax.ShapeDtypeStruct((b, n_out, HG_WIDTH), BF16),
        scratch_shapes=[
            pltpu.VMEM((s, HG_WIDTH), F32),
            pltpu.VMEM((s, HG_WIDTH), F32),
            pltpu.VMEM((HG_WIDTH, HG_KW), F32),
            pltpu.VMEM((HG_WIDTH, HG_KW), F32),
        ],
        compiler_params=_params(1),
        name="hgrn2",
    )(hg_in, lb, gain.reshape(1, HG_WIDTH), jnp.asarray(mf, BF16), jnp.asarray(mb, BF16),
      jnp.asarray(lvl_f), jnp.asarray(lvl_b), jnp.asarray(ones, BF16))


def _dft_tables(n):
    idx = np.arange(n, dtype=np.int64)
    ang = 2.0 * np.pi * ((idx[:, None] * idx[None, :]) % n).astype(np.float64) / n
    return np.cos(ang).astype(np.float32), np.sin(ang).astype(np.float32)


def _fourier_tile(z, table, mix, w, norm):
    tn = table.shape[0] // 2
    zz = _dot(table, z)
    zcat = jnp.concatenate([zz[0:tn], zz[tn:2 * tn]], axis=1).astype(BF16)
    mixed = _dot(zcat, mix) * norm
    return _dot(mixed.astype(BF16), w).astype(BF16)


def _fourier_kernel(z_ref, tl_ref, tc_ref, mix_ref, w_ref, o_ref, *maybe_oc_ref, n_lat):
    s_len = z_ref.shape[1]
    n_ctx = s_len - n_lat
    mix = mix_ref[...]
    w = w_ref[...]
    table = tl_ref[pl.program_id(1)]
    o_ref[0] = _fourier_tile(z_ref[0, 0:n_lat, :], table, mix, w, 1.0 / math.sqrt(n_lat * FN_GROUP_DIM))
    if maybe_oc_ref:
        @pl.when(pl.program_id(1) == pl.num_programs(1) - 1)
        def _():
            maybe_oc_ref[0][0] = _fourier_tile(z_ref[0, n_lat:s_len, :], tc_ref[...], mix, w,
                                               1.0 / math.sqrt(n_ctx * FN_GROUP_DIM))


def _stacked_dft_table(n, tile):
    cos, sin = _dft_tables(n)
    return np.concatenate([cos.reshape(n // tile, tile, n), sin.reshape(n // tile, tile, n)], axis=1)


def _fourier_tables(n_lat, n_ctx):
    cgrp, sgrp = _dft_tables(FN_GROUP_DIM)
    eye = np.eye(FN_GROUPS, dtype=np.float32)
    mix = np.concatenate([np.kron(eye, cgrp), -np.kron(eye, sgrp)], axis=0)
    tables = (_stacked_dft_table(n_lat, FOURIER_TILE), _stacked_dft_table(n_ctx, n_ctx)[0], mix)
    return tuple(jnp.asarray(t).astype(BF16) for t in tables)


def _fourier(fn_in, tables, w_fourier, n_lat, with_ctx_outputs):
    b, s, w = fn_in.shape
    n_ctx = s - n_lat
    tn = FOURIER_TILE
    table_lat, table_ctx, mix = tables
    out_specs = [pl.BlockSpec((1, tn, FN_WIDTH), lambda i, j: (i, j, 0))]
    out_shape = [jax.ShapeDtypeStruct((b, n_lat, FN_WIDTH), BF16)]
    if with_ctx_outputs:
        out_specs.append(pl.BlockSpec((1, n_ctx, FN_WIDTH), lambda i, j: (i, 0, 0)))
        out_shape.append(jax.ShapeDtypeStruct((b, n_ctx, FN_WIDTH), BF16))
    outs = pl.pallas_call(
        functools.partial(_fourier_kernel, n_lat=n_lat),
        grid=(b, n_lat // tn),
        in_specs=[
            pl.BlockSpec((1, s, w), lambda i, j: (i, 0, 0)),
            _RESIDENT,
            _const_spec(table_ctx.shape),
            _const_spec(mix.shape),
            _const_spec(w_fourier.shape),
        ],
        out_specs=out_specs,
        out_shape=out_shape,
        compiler_params=_params(2),
        name="fourier",
    )(fn_in, table_lat, table_ctx, mix, w_fourier)
    return (outs[0], outs[1]) if with_ctx_outputs else (outs[0], None)


def _ffn_kernel(*refs, final_norm, n_lat, select_ctx, n_srcs):
    streams = []
    pos = 0
    for n in n_srcs:
        streams.append(_stream_tile(refs[pos:pos + n]))
        pos += n
    x, attn, hg, fn = streams
    modb_ref, modc_ref, g2_ref, wo_ref, wgu_ref, wd_ref, gf_ref, o_ref, act_ref = refs[pos:]
    tm = x.shape[0]

    modulated = functools.partial(_modulated, tm, n_lat, modc_ref, modb_ref, use_ctx=select_ctx)

    y = (_dot(attn, wo_ref[0:MLA_WIDTH, :])
         + _dot(hg, wo_ref[MLA_WIDTH:MLA_WIDTH + HG_WIDTH, :])
         + _dot(fn, wo_ref[MLA_WIDTH + HG_WIDTH:MLA_WIDTH + HG_WIDTH + FN_WIDTH, :]))
    x1 = modulated(lambda mod, xr, yr: xr + mod[2:3, :] * yr, x, y)
    o_ref[0] = x1
    xn = x1 * lax.rsqrt(jnp.mean(x1 * x1, axis=-1, keepdims=True) + EPS) * g2_ref[...]
    hm = modulated(lambda mod, rows: (rows * (1.0 + mod[4:5, :]) + mod[3:4, :]).astype(BF16), xn)

    ft = wgu_ref.shape[2] // 2
    for t in range(wgu_ref.shape[0]):
        gu = _dot(hm, wgu_ref[t])
        act_ref[:, t * ft:(t + 1) * ft] = (_silu(gu[:, 0:ft]) * gu[:, ft:2 * ft]).astype(BF16)
    out = modulated(lambda mod, xr, ar: xr + mod[5:6, :] * ar, o_ref[0], _dot(act_ref[...], wd_ref[...]))
    if final_norm:
        out = out * lax.rsqrt(jnp.mean(out * out, axis=-1, keepdims=True) + EPS) * gf_ref[...]
    o_ref[0] = out


def _outproj_ffn(x_src, attn, hg, fn, mod, g2, w_out, w_gu, w_d, g_final, n_lat, n_ctx, ctx_row, with_ctx_rows,
                 final_norm):
    d = g2.shape[0]
    b = (x_src[0] if isinstance(x_src, tuple) else x_src).shape[0]
    tm = STREAM_TILE if with_ctx_rows else LATENT_TILE
    n_rows = n_lat + n_ctx if with_ctx_rows else n_lat
    hidden = w_d.shape[0]
    row = lambda w: pl.BlockSpec((1, tm, w), lambda i, j: (i, j, 0))
    src_specs, src_args, n_srcs = [], [], []
    for src in (x_src, attn, hg, fn):
        specs, args = _stream_specs(src, tm)
        src_specs += specs
        src_args += args
        n_srcs.append(len(args))
    return pl.pallas_call(
        functools.partial(_ffn_kernel, final_norm=final_norm, n_lat=n_lat, select_ctx=with_ctx_rows,
                          n_srcs=tuple(n_srcs)),
        grid=(b, n_rows // tm),
        in_specs=src_specs + [
            pl.BlockSpec((1, 6, d), lambda i, j: (i, 0, 0)),
            pl.BlockSpec((1, 6, d), lambda i, j: (ctx_row, 0, 0)),
            _const_spec((1, d)),
            _RESIDENT,
            _RESIDENT,
            _RESIDENT,
            _const_spec((1, d)),
        ],
        out_specs=row(d),
        out_shape=jax.ShapeDtypeStruct((b, n_rows, d), F32),
        scratch_shapes=[pltpu.VMEM((tm, hidden), BF16)],
        compiler_params=_params(2),
        name="outproj_ffn",
    )(*src_args, mod, mod, g2.reshape(1, d), w_out, w_gu, w_d, g_final.reshape(1, d))


def _rotate_half_cols(w):
    r1, r2, c1, c2 = jnp.split(w, 4, axis=-1)
    return jnp.concatenate([-r2, r1, -c2, c1], axis=-1)


def _pack_w_in(w):
    d = w.shape[0]
    zeros = jnp.zeros((d, 128 - MLA_ROPE), w.dtype)
    w_kr = w[:, O_KR:O_HQ]
    return jnp.concatenate([w[:, O_CQ:O_KR], w_kr, zeros, _rotate_half_cols(w_kr), zeros, w[:, O_HQ:]],
                           axis=1).astype(BF16)


def _pack_w_uq(w):
    r = w.shape[0]
    wh = w.reshape(r, MLA_HEADS, MLA_NOPE + MLA_ROPE)
    nope = wh[:, :, :MLA_NOPE].reshape(r, MLA_HEADS * MLA_NOPE)
    pe = wh[:, :, MLA_NOPE:]
    zeros = jnp.zeros((r, MLA_HEADS, 128 - MLA_ROPE), w.dtype)
    pe_a = jnp.concatenate([pe, zeros], axis=-1).reshape(r, MLA_HEADS * 128)
    pe_b = jnp.concatenate([_rotate_half_cols(pe), zeros], axis=-1).reshape(r, MLA_HEADS * 128)
    return jnp.concatenate([nope, pe_a, pe_b], axis=1).astype(BF16)


def _pack_w_ukv(w):
    r = w.shape[0]
    wh = w.reshape(r, MLA_HEADS, MLA_NOPE + MLA_V)
    k_nope = wh[:, :, :MLA_NOPE].reshape(r, MLA_HEADS * MLA_NOPE)
    v = wh[:, :, MLA_NOPE:].reshape(r, MLA_HEADS * MLA_V)
    return k_nope.astype(BF16), v.T.astype(BF16)


def _pack_w_gate_up(w, hidden):
    d = w.shape[0]
    n_ft = hidden // FFN_TILE
    gate = w[:, :hidden].reshape(d, n_ft, FFN_TILE)
    up = w[:, hidden:].reshape(d, n_ft, FFN_TILE)
    return jnp.concatenate([gate, up], axis=2).transpose(1, 0, 2).astype(BF16)


def _rope_tables(n_lat, n_ctx):
    rows = n_lat // GRID_W
    row_pos = np.repeat(np.arange(rows, dtype=np.float32), GRID_W)
    col_pos = np.tile(np.arange(GRID_W, dtype=np.float32), rows)
    axis_dim = MLA_ROPE // 2
    inv_freq = (ROPE_BASE ** (-np.arange(0, axis_dim, 2, dtype=np.float32) / axis_dim)).astype(np.float32)
    ang_r = row_pos[:, None] * inv_freq
    ang_c = col_pos[:, None] * inv_freq
    ang = np.concatenate([ang_r, ang_r, ang_c, ang_c], axis=-1)
    cos = np.ones((n_lat + n_ctx, 128), np.float32)
    sin = np.zeros((n_lat + n_ctx, 128), np.float32)
    cos[:n_lat, :MLA_ROPE] = np.cos(ang)
    sin[:n_lat, :MLA_ROPE] = np.sin(ang)
    return jnp.asarray(cos), jnp.asarray(sin)


def kernel(x, c, ctx, c_ctx, w_mod, b_mod, norm1_g, norm2_g, w_in, q_norm_g, w_uq, kv_norm_g, w_ukv, lb_param,
           hg_norm_g, w_fourier, w_out, w_gate_up, w_down, final_norm_g):
    b, t, d = x.shape
    n_ctx = ctx.shape[1]
    s = t + n_ctx
    depth = w_mod.shape[0]
    hidden = w_down.shape[1]
    assert s % STREAM_TILE == 0 and t % LATENT_TILE == 0 and t % ATTN_TILE == 0 and t % FOURIER_TILE == 0
    assert hidden % FFN_TILE == 0 and t % GRID_W == 0 and n_ctx % HG_CHUNK == 0

    mod_rows = -(-(b + 1) // 8) * 8
    c_all = jnp.zeros((mod_rows, d), F32).at[:b].set(c).at[b].set(c_ctx)
    mod_all = _modulation(c_all, w_mod, b_mod).reshape(depth, mod_rows, 6, d)
    lower = _lower_bounds(lb_param)
    cos_t, sin_t = _rope_tables(t, n_ctx)
    dft_tables = _fourier_tables(t, n_ctx)

    xa = (x, ctx)
    for l in range(depth):
        last = l == depth - 1
        q, k, v, hg_in, fn_in = _inproj(xa, mod_all[l], norm1_g[l], _pack_w_in(w_in[l]), q_norm_g[l],
                                        _pack_w_uq(w_uq[l]), kv_norm_g[l], *_pack_w_ukv(w_ukv[l]), cos_t, sin_t, t, b)
        attn, attn_ctx = _attention(q, k, v, t, not last)
        hg = _hgrn(hg_in, lower[l], hg_norm_g[l], t, not last)
        fn, fn_ctx = _fourier(fn_in, dft_tables, w_fourier[l].astype(BF16), t, not last)
        if not last:
            attn, fn = (attn, attn_ctx), (fn, fn_ctx)
        xa = _outproj_ffn(xa, attn, hg, fn, mod_all[l], norm2_g[l], w_out[l].astype(BF16),
                          _pack_w_gate_up(w_gate_up[l], hidden), w_down[l].astype(BF16), final_norm_g, t, n_ctx, b,
                          not last, last)
    return xa
```

```python
import functools
import math

import numpy as np
import jax
import jax.numpy as jnp
from jax import lax
from jax.experimental import pallas as pl
from jax.experimental.pallas import tpu as pltpu

EPS = 1e-6
GRID_W = 64
ROPE_BASE = 10000.0

MLA_HEADS = 4
MLA_Q_LORA = 256
MLA_KV_LORA = 128
MLA_NOPE = 128
MLA_ROPE = 64
MLA_V = 128
MLA_WIDTH = MLA_HEADS * MLA_V
HEAD_PAD = 256

HG_HEADS = 4
HG_K = 128
HG_V = 64
HG_WIDTH = HG_HEADS * HG_V
HG_KW = HG_HEADS * HG_K
HG_CHUNK = 64
HG_LEVELS = 6
HG_MATMUL_LEVELS = 2
HG_UNROLL = 4

FN_GROUPS = 4
FN_GROUP_DIM = 64
FN_WIDTH = FN_GROUPS * FN_GROUP_DIM

O_CQ = 0
O_CKV = O_CQ + MLA_Q_LORA
O_KR = O_CKV + MLA_KV_LORA
O_HQ = O_KR + MLA_ROPE

P_CQ = 0
P_CKV = P_CQ + MLA_Q_LORA
P_KPA = P_CKV + MLA_KV_LORA
P_KPB = P_KPA + 128
P_HG = P_KPB + 128
HG_IN_WIDTH = 3 * HG_KW + 2 * HG_WIDTH
P_FN = P_HG + HG_IN_WIDTH
P_WIDTH = P_FN + FN_WIDTH

STREAM_TILE = 768
LATENT_TILE = 1024
ATTN_TILE = 512
FOURIER_TILE = 1024
FFN_TILE = 256
VMEM_LIMIT = 56 * 1024 * 1024

BF16 = jnp.bfloat16
F32 = jnp.float32


def _dot(a, b):
    return jnp.dot(a, b, preferred_element_type=F32)


def _dot_nt(a, b):
    return lax.dot_general(a, b, (((1,), (1,)), ((), ())), preferred_element_type=F32)


def _dot_tn(a, b):
    return lax.dot_general(a, b, (((0,), (0,)), ((), ())), preferred_element_type=F32)


def _silu(x):
    return x * (1.0 / (1.0 + jnp.exp(-x)))


def _sigmoid(x):
    return 1.0 / (1.0 + jnp.exp(-x))


def _split3(x):
    hi = x.astype(BF16)
    r1 = x - hi.astype(F32)
    mid = r1.astype(BF16)
    lo = (r1 - mid.astype(F32)).astype(BF16)
    return hi, mid, lo


def _const_spec(shape):
    nd = len(shape)
    return pl.BlockSpec(shape, lambda *_: (0,) * nd)


_RESIDENT = pl.BlockSpec(memory_space=pltpu.VMEM)


def _params(n_axes):
    return pltpu.CompilerParams(dimension_semantics=("arbitrary",) * n_axes, vmem_limit_bytes=VMEM_LIMIT)


def _mod_kernel(c_ref, w_ref, b_ref, o_ref):
    act = _silu(c_ref[...]).astype(BF16)
    o_ref[0] = _dot(act, w_ref[0].astype(BF16)) + b_ref[0]


def _modulation(c_all, w_mod, b_mod):
    depth, d, n6 = w_mod.shape
    rows = c_all.shape[0]
    tn = n6 // 6
    return pl.pallas_call(
        _mod_kernel,
        grid=(depth, n6 // tn),
        in_specs=[
            pl.BlockSpec((rows, d), lambda l, j: (0, 0)),
            pl.BlockSpec((1, d, tn), lambda l, j: (l, 0, j)),
            pl.BlockSpec((1, 1, tn), lambda l, j: (l, 0, j)),
        ],
        out_specs=pl.BlockSpec((1, rows, tn), lambda l, j: (l, 0, j)),
        out_shape=jax.ShapeDtypeStruct((depth, rows, n6), F32),
        name="modulation",
    )(c_all, w_mod, b_mod.reshape(depth, 1, n6))


def _lower_bound_kernel(lb_ref, o_ref):
    depth = lb_ref.shape[0]
    lp = [lb_ref[l] for l in range(depth)]
    m = lp[0]
    for l in range(1, depth):
        m = jnp.maximum(m, lp[l])
    e = [jnp.exp(v - m) for v in lp]
    tot = e[0]
    for l in range(1, depth):
        tot = tot + e[l]
    probs = [v / tot for v in e]
    cum = probs[0]
    o_ref[0] = cum - probs[0]
    for l in range(1, depth):
        cum = cum + probs[l]
        o_ref[l] = cum - probs[0]


def _lower_bounds(lb_param):
    return pl.pallas_call(
        _lower_bound_kernel,
        out_shape=jax.ShapeDtypeStruct(lb_param.shape, F32),
        name="hgrn_lower_bounds",
    )(lb_param.astype(F32))


def _modulated(tile_rows, n_lat, modc_ref, modb_ref, fn, *arrays, use_ctx=True):
    lat = modb_ref[0]
    if not use_ctx:
        return fn(lat, *arrays)
    tail = n_lat % tile_rows
    if tail == 0:
        return fn(jnp.where(pl.program_id(1) * tile_rows >= n_lat, modc_ref[0], lat), *arrays)
    is_last = pl.program_id(1) == pl.num_programs(1) - 1
    mixed = jnp.where(is_last, modc_ref[0], lat)
    return jnp.concatenate([fn(lat, *[a[0:tail] for a in arrays]),
                            fn(mixed, *[a[tail:tile_rows] for a in arrays])], axis=0)


def _stream_specs(src, tm):
    if not isinstance(src, tuple):
        return [pl.BlockSpec((1, tm, src.shape[2]), lambda i, j: (i, j, 0))], [src]
    lat, ctx = src
    t, d = lat.shape[1:]
    n_ctx = ctx.shape[1]
    n_full = t // tm
    tail = t - n_full * tm
    assert n_full >= 1 and tail > 0 and tail + n_ctx == tm and (n_full * tm) % tail == 0
    specs = [pl.BlockSpec((1, tm, d), lambda i, j: (i, jnp.minimum(j, n_full - 1), 0)),
             pl.BlockSpec((1, tail, d), lambda i, j: (i, (n_full * tm) // tail, 0)),
             pl.BlockSpec((1, n_ctx, d), lambda i, j: (i, 0, 0))]
    return specs, [lat, lat, ctx]


def _stream_tile(refs):
    if len(refs) == 1:
        return refs[0][0]
    main_ref, tail_ref, ctx_ref = refs
    mixed = jnp.concatenate([tail_ref[0], ctx_ref[0]], axis=0)
    return jnp.where(pl.program_id(1) < pl.num_programs(1) - 1, main_ref[0], mixed)


def _inproj_kernel(*refs, q_scale, n_lat, n_src):
    (modb_ref, modc_ref, g1_ref, w_in_ref, qg_ref, wq_ref, kvg_ref, wk_ref, wvt_ref, cos_ref, sin_ref,
     q_ref, k_ref, vt_ref, hg_ref, fn_ref) = refs[n_src:]
    x = _stream_tile(refs[:n_src])
    tm = x.shape[0]
    xn = x * lax.rsqrt(jnp.mean(x * x, axis=-1, keepdims=True) + EPS) * g1_ref[...]
    xm = _modulated(tm, n_lat, modc_ref, modb_ref,
                    lambda mod, rows: (rows * (1.0 + mod[1:2, :]) + mod[0:1, :]).astype(BF16), xn)

    hg_ref[0] = _dot(xm, w_in_ref[:, P_HG:P_FN]).astype(BF16)
    fn_ref[0] = _dot(xm, w_in_ref[:, P_FN:P_WIDTH]).astype(BF16)
    p = _dot(xm, w_in_ref[:, 0:P_HG])

    cos = cos_ref[...]
    sin = sin_ref[...]

    cq = p[:, P_CQ:P_CKV]
    cqn = (cq * lax.rsqrt(jnp.mean(cq * cq, axis=-1, keepdims=True) + EPS) * qg_ref[...]).astype(BF16)
    qq = _dot(cqn, wq_ref[...])
    nw = MLA_HEADS * MLA_NOPE
    for h in range(MLA_HEADS):
        q_ref[0, :, h * HEAD_PAD:h * HEAD_PAD + 128] = (qq[:, h * 128:(h + 1) * 128] * q_scale).astype(BF16)
        pe = qq[:, nw + h * 128:nw + (h + 1) * 128] * cos + qq[:, 2 * nw + h * 128:2 * nw + (h + 1) * 128] * sin
        q_ref[0, :, h * HEAD_PAD + 128:(h + 1) * HEAD_PAD] = (pe * q_scale).astype(BF16)

    ckv = p[:, P_CKV:P_KPA]
    ckvn = (ckv * lax.rsqrt(jnp.mean(ckv * ckv, axis=-1, keepdims=True) + EPS) * kvg_ref[...]).astype(BF16)
    kv = _dot(ckvn, wk_ref[...])
    vt_ref[0] = _dot_nt(wvt_ref[...], ckvn).astype(BF16)
    kpe = (p[:, P_KPA:P_KPB] * cos + p[:, P_KPB:P_HG] * sin).astype(BF16)
    for h in range(MLA_HEADS):
        k_ref[0, :, h * HEAD_PAD:h * HEAD_PAD + 128] = kv[:, h * 128:(h + 1) * 128].astype(BF16)
        k_ref[0, :, h * HEAD_PAD + 128:(h + 1) * HEAD_PAD] = kpe


def _inproj(src, mod, g1, w_in_p, qg, wq_all, kvg, wk_p, wvt_p, cos_t, sin_t, n_lat, ctx_row):
    s = cos_t.shape[0]
    d = g1.shape[0]
    b = (src[0] if isinstance(src, tuple) else src).shape[0]
    tm = STREAM_TILE
    row = lambda w: pl.BlockSpec((1, tm, w), lambda i, j: (i, j, 0))
    src_specs, src_args = _stream_specs(src, tm)
    return pl.pallas_call(
        functools.partial(_inproj_kernel, q_scale=math.log2(math.e) / math.sqrt(MLA_NOPE + MLA_ROPE), n_lat=n_lat,
                          n_src=len(src_args)),
        grid=(b, s // tm),
        in_specs=src_specs + [
            pl.BlockSpec((1, 6, d), lambda i, j: (i, 0, 0)),
            pl.BlockSpec((1, 6, d), lambda i, j: (ctx_row, 0, 0)),
            _const_spec((1, d)),
            _RESIDENT,
            _const_spec((1, MLA_Q_LORA)),
            _RESIDENT,
            _const_spec((1, MLA_KV_LORA)),
            _RESIDENT,
            _RESIDENT,
            pl.BlockSpec((tm, 128), lambda i, j: (j, 0)),
            pl.BlockSpec((tm, 128), lambda i, j: (j, 0)),
        ],
        out_specs=[row(MLA_HEADS * HEAD_PAD), row(MLA_HEADS * HEAD_PAD),
                   pl.BlockSpec((1, MLA_WIDTH, tm), lambda i, j: (i, 0, j)), row(HG_IN_WIDTH), row(FN_WIDTH)],
        out_shape=[
            jax.ShapeDtypeStruct((b, s, MLA_HEADS * HEAD_PAD), BF16),
            jax.ShapeDtypeStruct((b, s, MLA_HEADS * HEAD_PAD), BF16),
            jax.ShapeDtypeStruct((b, MLA_WIDTH, s), BF16),
            jax.ShapeDtypeStruct((b, s, HG_IN_WIDTH), BF16),
            jax.ShapeDtypeStruct((b, s, FN_WIDTH), BF16),
        ],
        compiler_params=_params(2),
        name="inproj",
    )(*src_args, mod, mod, g1.reshape(1, d), w_in_p, qg.reshape(1, -1), wq_all, kvg.reshape(1, -1), wk_p, wvt_p,
      cos_t, sin_t)


def _attend(q_ref, k_ref, vt_ref, o_ref, key_lo, key_hi):
    def scores(h):
        qh = q_ref[0, :, h * HEAD_PAD:(h + 1) * HEAD_PAD]
        kh = k_ref[0, key_lo:key_hi, h * HEAD_PAD:(h + 1) * HEAD_PAD]
        return _dot_nt(kh, qh)

    all_scores = [scores(h) for h in range(MLA_HEADS)]
    probs = []
    for st in all_scores:
        e = jnp.exp2(st - jnp.max(st, axis=0, keepdims=True))
        probs.append((e.astype(BF16), 1.0 / jnp.sum(e, axis=0, keepdims=True)))
    for h, (e, inv_denom) in enumerate(probs):
        vth = vt_ref[0, h * MLA_V:(h + 1) * MLA_V, key_lo:key_hi]
        ot = _dot(vth, e) * inv_denom
        o_ref[0, :, h * MLA_V:(h + 1) * MLA_V] = ot.T.astype(BF16)


def _attn_kernel(q_ref, k_ref, vt_ref, o_ref):
    _attend(q_ref, k_ref, vt_ref, o_ref, 0, k_ref.shape[1])


def _attn_ctx_kernel(q_ref, qc_ref, k_ref, vt_ref, o_ref, oc_ref, *, n_lat):
    s_len = k_ref.shape[1]
    _attend(q_ref, k_ref, vt_ref, o_ref, 0, s_len)

    @pl.when(pl.program_id(1) == pl.num_programs(1) - 1)
    def _():
        _attend(qc_ref, k_ref, vt_ref, oc_ref, n_lat, s_len)


def _attention(q, k, vt, n_lat, with_ctx_queries):
    b, s, _ = q.shape
    tq = ATTN_TILE
    n_ctx = s - n_lat
    q_spec = pl.BlockSpec((1, tq, q.shape[2]), lambda i, j: (i, j, 0))
    kv_specs = [pl.BlockSpec((1, s, k.shape[2]), lambda i, j: (i, 0, 0)),
                pl.BlockSpec((1, vt.shape[1], s), lambda i, j: (i, 0, 0))]
    o_spec = pl.BlockSpec((1, tq, MLA_WIDTH), lambda i, j: (i, j, 0))
    o_shape = jax.ShapeDtypeStruct((b, n_lat, MLA_WIDTH), BF16)
    if not with_ctx_queries:
        out = pl.pallas_call(
            _attn_kernel,
            grid=(b, n_lat // tq),
            in_specs=[q_spec] + kv_specs,
            out_specs=o_spec,
            out_shape=o_shape,
            compiler_params=_params(2),
            name="attention",
        )(q, k, vt)
        return out, None
    assert n_lat % n_ctx == 0
    return pl.pallas_call(
        functools.partial(_attn_ctx_kernel, n_lat=n_lat),
        grid=(b, n_lat // tq),
        in_specs=[q_spec, pl.BlockSpec((1, n_ctx, q.shape[2]), lambda i, j: (i, n_lat // n_ctx, 0))] + kv_specs,
        out_specs=[o_spec, pl.BlockSpec((1, n_ctx, MLA_WIDTH), lambda i, j: (i, 0, 0))],
        out_shape=[o_shape, jax.ShapeDtypeStruct((b, n_ctx, MLA_WIDTH), BF16)],
        compiler_params=_params(2),
        name="attention_ctx",
    )(q, q, k, vt)


def _hgrn_constants():
    c = HG_CHUNK
    t = np.arange(c)[:, None]
    u = np.arange(c)[None, :]
    blocks = [(u <= t)]
    for m in (8, 4):
        half = m // 2
        mid = (t // m) * m + half
        lower = (t % m) >= half
        blocks.append(np.where(lower, (u >= mid) & (u <= t), (u >= t + 1) & (u <= mid - 1)))
    mf = np.stack([blk.astype(np.float32) for blk in blocks])
    mb = np.ascontiguousarray(mf[:, ::-1, ::-1])
    msb = np.floor(np.log2(np.maximum(t ^ u, 1))).astype(np.int32)
    lvl = np.where(t > u, HG_LEVELS - 1 - msb, np.where(t == u, HG_LEVELS, -1)).astype(np.int32)
    lvl_f = np.tile(lvl, (1, 2))
    lvl_b = np.tile(np.ascontiguousarray(lvl.T), (1, 2))
    return mf.reshape(-1, c), mb.reshape(-1, c), lvl_f, lvl_b


def _pair_weights(x, p):
    zeros = jnp.zeros((x.shape[0], HG_K), x.dtype)
    xa = x[:, 2 * p * HG_K:(2 * p + 1) * HG_K]
    xb = x[:, (2 * p + 1) * HG_K:(2 * p + 2) * HG_K]
    blk = jnp.concatenate([jnp.concatenate([xa, zeros], axis=1), jnp.concatenate([zeros, xb], axis=1)], axis=0)
    return blk.T


def _hgrn_gates(z, lb, m_ref, with_output):
    f = lb + (1.0 - lb) * _sigmoid(z)
    g = jnp.log2(f)
    g_hi = g.astype(BF16)
    g_lo = (g - g_hi.astype(F32)).astype(BF16)
    m_all = m_ref[...] if with_output else m_ref[0:HG_CHUNK, :]
    return f, _dot(m_all, g_hi) + _dot(m_all, g_lo)


def _hgrn_decays(hq, f, eb, fwd, with_output):
    c = HG_CHUNK
    kk = 1.0 - f
    bc = eb[0:c]
    total = bc[c - 1:c] if fwd else bc[0:1]
    k_up = (kk * jnp.exp2(total - bc)).astype(BF16)
    chunk_decay = jnp.exp2(total)
    if not with_output:
        return k_up, chunk_decay, None
    qq = _silu(hq)
    row = lax.broadcasted_iota(jnp.int32, (c, HG_KW), 0)
    xs = []
    for l in range(HG_LEVELS):
        m = c >> l
        half = m // 2
        bit = (row & half) != 0
        is_query = bit if fwd else jnp.logical_not(bit)
        if m == 2:
            xs.append(jnp.where(is_query, qq * f, kk).astype(BF16))
            continue
        if m >= 16:
            refs = [bc[b0 + half - 1:b0 + half] if fwd else bc[b0 + half:b0 + half + 1] for b0 in range(0, c, m)]
            ref = jnp.concatenate([jnp.broadcast_to(r, (m, HG_KW)) for r in refs], axis=0)
            diff = bc - ref
            expo = jnp.where(is_query, diff, -diff)
        else:
            idx = 1 + l - (HG_LEVELS - 1 - HG_MATMUL_LEVELS)
            expo = eb[idx * c:(idx + 1) * c]
        xs.append((jnp.where(is_query, qq, kk) * jnp.exp2(expo)).astype(BF16))
    q_in = (qq * jnp.exp2(bc)).astype(BF16)
    return k_up, chunk_decay, (xs, qq.astype(BF16), kk.astype(BF16), q_in)


def _hgrn_scores(operands, lvl_masks):
    xs, qb, kb, _ = operands
    scores = []
    for p in range(HG_HEADS // 2):
        kp = slice(2 * p * HG_K, (2 * p + 2) * HG_K)
        sc = jnp.where(lvl_masks[HG_LEVELS], _dot(qb[:, kp], _pair_weights(kb, p)), 0.0)
        for l in range(HG_LEVELS):
            sc = jnp.where(lvl_masks[l], _dot(xs[l][:, kp], _pair_weights(xs[l], p)), sc)
        scores.append(sc.astype(BF16))
    return scores


def _hgrn_apply(scores, operands, k_up, chunk_decay, vv, v_lo, v_hi, state_ref):
    outs = []
    for p in range(HG_HEADS // 2):
        kp = slice(2 * p * HG_K, (2 * p + 2) * HG_K)
        vp = slice(2 * p * HG_V, (2 * p + 2) * HG_V)
        if scores is not None:
            v_pair = vv[:, vp]
            v_diag = jnp.concatenate([v_pair * v_lo, v_pair * v_hi], axis=0)
            st_pair = state_ref[2 * p * HG_V:(2 * p + 2) * HG_V, kp]
            outs.append(_dot(scores[p], v_diag) + _dot(operands[3][:, kp], st_pair.astype(BF16).T))
        for h in (2 * p, 2 * p + 1):
            ks = slice(h * HG_K, (h + 1) * HG_K)
            vs = slice(h * HG_V, (h + 1) * HG_V)
            state_ref[vs, ks] = chunk_decay[:, ks] * state_ref[vs, ks] + _dot_tn(vv[:, vs], k_up[:, ks])
    return jnp.concatenate(outs, axis=1) if outs else None


def _hgrn_kernel(x_ref, lb_ref, gain_ref, mf_ref, mb_ref, lvlf_ref, lvlb_ref, ones_ref, o_ref,
                 of_ref, ob_ref, sf_ref, sb_ref, *, n_lat_chunks):
    c = HG_CHUNK
    s_len = x_ref.shape[1]
    n_chunks = s_len // c
    n_ctx_chunks = n_chunks - n_lat_chunks
    sf_ref[...] = jnp.zeros_like(sf_ref)
    sb_ref[...] = jnp.zeros_like(sb_ref)
    lvl_f = lvlf_ref[...]
    lvl_b = lvlb_ref[...]
    masks_f = [lvl_f == l for l in range(HG_LEVELS + 1)]
    masks_b = [lvl_b == l for l in range(HG_LEVELS + 1)]
    lb_f = lb_ref[0:1, :]
    lb_b = lb_ref[1:2, :]
    v_lane = lax.broadcasted_iota(jnp.int32, (c, 2 * HG_V), 1)
    v_lo = (v_lane < HG_V).astype(F32).astype(BF16)
    v_hi = (v_lane >= HG_V).astype(F32).astype(BF16)

    def load(ci):
        r0 = pl.multiple_of(ci * c, c)
        hq = x_ref[0, pl.ds(r0, c), 0:HG_KW].astype(F32)
        vv = x_ref[0, pl.ds(r0, c), 3 * HG_KW:3 * HG_KW + HG_WIDTH]
        return r0, hq, vv

    def scan(first_fwd, first_bwd, n, with_output):
        per_trip = HG_UNROLL if n % HG_UNROLL == 0 else n

        def step(i, carry):
            gated = []
            for u in range(per_trip):
                for fwd in (True, False):
                    ci = first_fwd + i * per_trip + u if fwd else first_bwd - i * per_trip - u
                    r0, hq, vv = load(ci)
                    zcol = HG_KW if fwd else 2 * HG_KW
                    z = x_ref[0, pl.ds(r0, c), zcol:zcol + HG_KW].astype(F32)
                    gated.append((fwd, r0, hq, vv, _hgrn_gates(z, lb_f if fwd else lb_b, mf_ref if fwd else mb_ref,
                                                               with_output)))
            jobs = [(fwd, r0, vv, _hgrn_decays(hq, f, eb, fwd, with_output)) for fwd, r0, hq, vv, (f, eb) in gated]
            scores = [_hgrn_scores(ph[2], masks_f if fwd else masks_b) if with_output else None
                      for fwd, _, _, ph in jobs]
            for (fwd, r0, vv, (k_up, chunk_decay, operands)), sc in zip(jobs, scores):
                out = _hgrn_apply(sc, operands, k_up, chunk_decay, vv, v_lo, v_hi, sf_ref if fwd else sb_ref)
                if with_output:
                    (of_ref if fwd else ob_ref)[pl.ds(r0, c), :] = out
            return carry

        lax.fori_loop(0, n // per_trip, step, 0)

    scan(n_lat_chunks, n_chunks - 1, n_ctx_chunks, o_ref.shape[1] == s_len)
    scan(0, n_lat_chunks - 1, n_lat_chunks, True)

    rt = 256
    gain = gain_ref[...]
    ones = ones_ref[...]

    def readout(i, carry):
        r0 = pl.multiple_of(i * rt, rt)
        o = of_ref[pl.ds(r0, rt), :] + ob_ref[pl.ds(r0, rt), :]
        sq_hi, sq_mid, sq_lo = _split3(o * o)
        msq = _dot(sq_hi, ones) + _dot(sq_mid, ones) + _dot(sq_lo, ones)
        zg = x_ref[0, pl.ds(r0, rt), 3 * HG_KW + HG_WIDTH:HG_IN_WIDTH].astype(F32)
        y = o * lax.rsqrt(msq + EPS) * gain * _silu(zg)
        o_ref[0, pl.ds(r0, rt), :] = y.astype(BF16)
        return carry

    lax.fori_loop(0, o_ref.shape[1] // rt, readout, 0)


def _hgrn(hg_in, lb, gain, n_lat, with_ctx_outputs):
    b, s, w = hg_in.shape
    mf, mb, lvl_f, lvl_b = _hgrn_constants()
    ones = np.kron(np.eye(HG_HEADS, dtype=np.float32), np.full((HG_V, HG_V), 1.0 / HG_V, np.float32))
    n_out = s if with_ctx_outputs else n_lat
    return pl.pallas_call(
        functools.partial(_hgrn_kernel, n_lat_chunks=n_lat // HG_CHUNK),
        grid=(b,),
        in_specs=[
            pl.BlockSpec((1, s, w), lambda i: (i, 0, 0)),
            _const_spec((2, HG_KW)),
            _const_spec((1, HG_WIDTH)),
            _const_spec(mf.shape),
            _const_spec(mb.shape),
            _const_spec(lvl_f.shape),
            _const_spec(lvl_b.shape),
            _const_spec(ones.shape),
        ],
        out_specs=pl.BlockSpec((1, n_out, HG_WIDTH), lambda i: (i, 0, 0)),
        out_shape=jax.ShapeDtypeStruct((b, n_out, HG_WIDTH), BF16),
        scratch_shapes=[
            pltpu.VMEM((s, HG_WIDTH), F32),
            pltpu.VMEM((s, HG_WIDTH), F32),
            pltpu.VMEM((HG_WIDTH, HG_KW), F32),
            pltpu.VMEM((HG_WIDTH, HG_KW), F32),
        ],
        compiler_params=_params(1),
        name="hgrn2",
    )(hg_in, lb, gain.reshape(1, HG_WIDTH), jnp.asarray(mf, BF16), jnp.asarray(mb, BF16),
      jnp.asarray(lvl_f), jnp.asarray(lvl_b), jnp.asarray(ones, BF16))


def _dft_tables(n):
    idx = np.arange(n, dtype=np.int64)
    ang = 2.0 * np.pi * ((idx[:, None] * idx[None, :]) % n).astype(np.float64) / n
    return np.cos(ang).astype(np.float32), np.sin(ang).astype(np.float32)


def _fourier_tile(z, table, mix, w, norm):
    tn = table.shape[0] // 2
    zz = _dot(table, z)
    zcat = jnp.concatenate([zz[0:tn], zz[tn:2 * tn]], axis=1).astype(BF16)
    mixed = _dot(zcat, mix) * norm
    return _dot(mixed.astype(BF16), w).astype(BF16)


def _fourier_kernel(z_ref, tl_ref, tc_ref, mix_ref, w_ref, o_ref, *maybe_oc_ref, n_lat):
    s_len = z_ref.shape[1]
    n_ctx = s_len - n_lat
    mix = mix_ref[...]
    w = w_ref[...]
    table = tl_ref[pl.program_id(1)]
    o_ref[0] = _fourier_tile(z_ref[0, 0:n_lat, :], table, mix, w, 1.0 / math.sqrt(n_lat * FN_GROUP_DIM))
    if maybe_oc_ref:
        @pl.when(pl.program_id(1) == pl.num_programs(1) - 1)
        def _():
            maybe_oc_ref[0][0] = _fourier_tile(z_ref[0, n_lat:s_len, :], tc_ref[...], mix, w,
                                               1.0 / math.sqrt(n_ctx * FN_GROUP_DIM))


def _stacked_dft_table(n, tile):
    cos, sin = _dft_tables(n)
    return np.concatenate([cos.reshape(n // tile, tile, n), sin.reshape(n // tile, tile, n)], axis=1)


def _fourier_tables(n_lat, n_ctx):
    cgrp, sgrp = _dft_tables(FN_GROUP_DIM)
    eye = np.eye(FN_GROUPS, dtype=np.float32)
    mix = np.concatenate([np.kron(eye, cgrp), -np.kron(eye, sgrp)], axis=0)
    tables = (_stacked_dft_table(n_lat, FOURIER_TILE), _stacked_dft_table(n_ctx, n_ctx)[0], mix)
    return tuple(jnp.asarray(t).astype(BF16) for t in tables)


def _fourier(fn_in, tables, w_fourier, n_lat, with_ctx_outputs):
    b, s, w = fn_in.shape
    n_ctx = s - n_lat
    tn = FOURIER_TILE
    table_lat, table_ctx, mix = tables
    out_specs = [pl.BlockSpec((1, tn, FN_WIDTH), lambda i, j: (i, j, 0))]
    out_shape = [jax.ShapeDtypeStruct((b, n_lat, FN_WIDTH), BF16)]
    if with_ctx_outputs:
        out_specs.append(pl.BlockSpec((1, n_ctx, FN_WIDTH), lambda i, j: (i, 0, 0)))
        out_shape.append(jax.ShapeDtypeStruct((b, n_ctx, FN_WIDTH), BF16))
    outs = pl.pallas_call(
        functools.partial(_fourier_kernel, n_lat=n_lat),
        grid=(b, n_lat // tn),
        in_specs=[
            pl.BlockSpec((1, s, w), lambda i, j: (i, 0, 0)),
            _RESIDENT,
            _const_spec(table_ctx.shape),
            _const_spec(mix.shape),
            _const_spec(w_fourier.shape),
        ],
        out_specs=out_specs,
        out_shape=out_shape,
        compiler_params=_params(2),
        name="fourier",
    )(fn_in, table_lat, table_ctx, mix, w_fourier)
    return (outs[0], outs[1]) if with_ctx_outputs else (outs[0], None)


def _ffn_kernel(*refs, final_norm, n_lat, select_ctx, n_srcs):
    streams = []
    pos = 0
    for n in n_srcs:
        streams.append(_stream_tile(refs[pos:pos + n]))
        pos += n
    x, attn, hg, fn = streams
    modb_ref, modc_ref, g2_ref, wo_ref, wgu_ref, wd_ref, gf_ref, o_ref, act_ref = refs[pos:]
    tm = x.shape[0]

    modulated = functools.partial(_modulated, tm, n_lat, modc_ref, modb_ref, use_ctx=select_ctx)

    y = (_dot(attn, wo_ref[0:MLA_WIDTH, :])
         + _dot(hg, wo_ref[MLA_WIDTH:MLA_WIDTH + HG_WIDTH, :])
         + _dot(fn, wo_ref[MLA_WIDTH + HG_WIDTH:MLA_WIDTH + HG_WIDTH + FN_WIDTH, :]))
    x1 = modulated(lambda mod, xr, yr: xr + mod[2:3, :] * yr, x, y)
    o_ref[0] = x1
    xn = x1 * lax.rsqrt(jnp.mean(x1 * x1, axis=-1, keepdims=True) + EPS) * g2_ref[...]
    hm = modulated(lambda mod, rows: (rows * (1.0 + mod[4:5, :]) + mod[3:4, :]).astype(BF16), xn)

    ft = wgu_ref.shape[2] // 2
    for t in range(wgu_ref.shape[0]):
        gu = _dot(hm, wgu_ref[t])
        act_ref[:, t * ft:(t + 1) * ft] = (_silu(gu[:, 0:ft]) * gu[:, ft:2 * ft]).astype(BF16)
    out = modulated(lambda mod, xr, ar: xr + mod[5:6, :] * ar, o_ref[0], _dot(act_ref[...], wd_ref[...]))
    if final_norm:
        out = out * lax.rsqrt(jnp.mean(out * out, axis=-1, keepdims=True) + EPS) * gf_ref[...]
    o_ref[0] = out


def _outproj_ffn(x_src, attn, hg, fn, mod, g2, w_out, w_gu, w_d, g_final, n_lat, n_ctx, ctx_row, with_ctx_rows,
                 final_norm):
    d = g2.shape[0]
    b = (x_src[0] if isinstance(x_src, tuple) else x_src).shape[0]
    tm = STREAM_TILE if with_ctx_rows else LATENT_TILE
    n_rows = n_lat + n_ctx if with_ctx_rows else n_lat
    hidden = w_d.shape[0]
    row = lambda w: pl.BlockSpec((1, tm, w), lambda i, j: (i, j, 0))
    src_specs, src_args, n_srcs = [], [], []
    for src in (x_src, attn, hg, fn):
        specs, args = _stream_specs(src, tm)
        src_specs += specs
        src_args += args
        n_srcs.append(len(args))
    return pl.pallas_call(
        functools.partial(_ffn_kernel, final_norm=final_norm, n_lat=n_lat, select_ctx=with_ctx_rows,
                          n_srcs=tuple(n_srcs)),
        grid=(b, n_rows // tm),
        in_specs=src_specs + [
            pl.BlockSpec((1, 6, d), lambda i, j: (i, 0, 0)),
            pl.BlockSpec((1, 6, d), lambda i, j: (ctx_row, 0, 0)),
            _const_spec((1, d)),
            _RESIDENT,
            _RESIDENT,
            _RESIDENT,
            _const_spec((1, d)),
        ],
        out_specs=row(d),
        out_shape=jax.ShapeDtypeStruct((b, n_rows, d), F32),
        scratch_shapes=[pltpu.VMEM((tm, hidden), BF16)],
        compiler_params=_params(2),
        name="outproj_ffn",
    )(*src_args, mod, mod, g2.reshape(1, d), w_out, w_gu, w_d, g_final.reshape(1, d))


def _rotate_half_cols(w):
    r1, r2, c1, c2 = jnp.split(w, 4, axis=-1)
    return jnp.concatenate([-r2, r1, -c2, c1], axis=-1)


def _pack_w_in(w):
    d = w.shape[0]
    zeros = jnp.zeros((d, 128 - MLA_ROPE), w.dtype)
    w_kr = w[:, O_KR:O_HQ]
    return jnp.concatenate([w[:, O_CQ:O_KR], w_kr, zeros, _rotate_half_cols(w_kr), zeros, w[:, O_HQ:]],
                           axis=1).astype(BF16)


def _pack_w_uq(w):
    r = w.shape[0]
    wh = w.reshape(r, MLA_HEADS, MLA_NOPE + MLA_ROPE)
    nope = wh[:, :, :MLA_NOPE].reshape(r, MLA_HEADS * MLA_NOPE)
    pe = wh[:, :, MLA_NOPE:]
    zeros = jnp.zeros((r, MLA_HEADS, 128 - MLA_ROPE), w.dtype)
    pe_a = jnp.concatenate([pe, zeros], axis=-1).reshape(r, MLA_HEADS * 128)
    pe_b = jnp.concatenate([_rotate_half_cols(pe), zeros], axis=-1).reshape(r, MLA_HEADS * 128)
    return jnp.concatenate([nope, pe_a, pe_b], axis=1).astype(BF16)


def _pack_w_ukv(w):
    r = w.shape[0]
    wh = w.reshape(r, MLA_HEADS, MLA_NOPE + MLA_V)
    k_nope = wh[:, :, :MLA_NOPE].reshape(r, MLA_HEADS * MLA_NOPE)
    v = wh[:, :, MLA_NOPE:].reshape(r, MLA_HEADS * MLA_V)
    return k_nope.astype(BF16), v.T.astype(BF16)


def _pack_w_gate_up(w, hidden):
    d = w.shape[0]
    n_ft = hidden // FFN_TILE
    gate = w[:, :hidden].reshape(d, n_ft, FFN_TILE)
    up = w[:, hidden:].reshape(d, n_ft, FFN_TILE)
    return jnp.concatenate([gate, up], axis=2).transpose(1, 0, 2).astype(BF16)


def _rope_tables(n_lat, n_ctx):
    rows = n_lat // GRID_W
    row_pos = np.repeat(np.arange(rows, dtype=np.float32), GRID_W)
    col_pos = np.tile(np.arange(GRID_W, dtype=np.float32), rows)
    axis_dim = MLA_ROPE // 2
    inv_freq = (ROPE_BASE ** (-np.arange(0, axis_dim, 2, dtype=np.float32) / axis_dim)).astype(np.float32)
    ang_r = row_pos[:, None] * inv_freq
    ang_c = col_pos[:, None] * inv_freq
    ang = np.concatenate([ang_r, ang_r, ang_c, ang_c], axis=-1)
    cos = np.ones((n_lat + n_ctx, 128), np.float32)
    sin = np.zeros((n_lat + n_ctx, 128), np.float32)
    cos[:n_lat, :MLA_ROPE] = np.cos(ang)
    sin[:n_lat, :MLA_ROPE] = np.sin(ang)
    return jnp.asarray(cos), jnp.asarray(sin)


def kernel(x, c, ctx, c_ctx, w_mod, b_mod, norm1_g, norm2_g, w_in, q_norm_g, w_uq, kv_norm_g, w_ukv, lb_param,
           hg_norm_g, w_fourier, w_out, w_gate_up, w_down, final_norm_g):
    b, t, d = x.shape
    n_ctx = ctx.shape[1]
    s = t + n_ctx
    depth = w_mod.shape[0]
    hidden = w_down.shape[1]
    assert s % STREAM_TILE == 0 and t % LATENT_TILE == 0 and t % ATTN_TILE == 0 and t % FOURIER_TILE == 0
    assert hidden % FFN_TILE == 0 and t % GRID_W == 0 and n_ctx % HG_CHUNK == 0

    mod_rows = -(-(b + 1) // 8) * 8
    c_all = jnp.zeros((mod_rows, d), F32).at[:b].set(c).at[b].set(c_ctx)
    mod_all = _modulation(c_all, w_mod, b_mod).reshape(depth, mod_rows, 6, d)
    lower = _lower_bounds(lb_param)
    cos_t, sin_t = _rope_tables(t, n_ctx)
    dft_tables = _fourier_tables(t, n_ctx)

    xa = (x, ctx)
    for l in range(depth):
        last = l == depth - 1
        q, k, v, hg_in, fn_in = _inproj(xa, mod_all[l], norm1_g[l], _pack_w_in(w_in[l]), q_norm_g[l],
                                        _pack_w_uq(w_uq[l]), kv_norm_g[l], *_pack_w_ukv(w_ukv[l]), cos_t, sin_t, t, b)
        attn, attn_ctx = _attention(q, k, v, t, not last)
        hg = _hgrn(hg_in, lower[l], hg_norm_g[l], t, not last)
        fn, fn_ctx = _fourier(fn_in, dft_tables, w_fourier[l].astype(BF16), t, not last)
        if not last:
            attn, fn = (attn, attn_ctx), (fn, fn_ctx)
        xa = _outproj_ffn(xa, attn, hg, fn, mod_all[l], norm2_g[l], w_out[l].astype(BF16),
                          _pack_w_gate_up(w_gate_up[l], hidden), w_down[l].astype(BF16), final_norm_g, t, n_ctx, b,
                          not last, last)
    return xa
```

```python
import functools
import math

import numpy as np
import jax
import jax.numpy as jnp
from jax import lax
from jax.experimental import pallas as pl
from jax.experimental.pallas import tpu as pltpu

EPS = 1e-6
GRID_W = 64
ROPE_BASE = 10000.0

MLA_HEADS = 4
MLA_Q_LORA = 256
MLA_KV_LORA = 128
MLA_NOPE = 128
MLA_ROPE = 64
MLA_V = 128
MLA_WIDTH = MLA_HEADS * MLA_V
HEAD_PAD = 256

HG_HEADS = 4
HG_K = 128
HG_V = 64
HG_WIDTH = HG_HEADS * HG_V
HG_KW = HG_HEADS * HG_K
HG_CHUNK = 64
HG_LEVELS = 6
HG_MATMUL_LEVELS = 1
HG_UNROLL = 8

FN_GROUPS = 4
FN_GROUP_DIM = 64
FN_WIDTH = FN_GROUPS * FN_GROUP_DIM

O_CQ = 0
O_CKV = O_CQ + MLA_Q_LORA
O_KR = O_CKV + MLA_KV_LORA
O_HQ = O_KR + MLA_ROPE

P_CQ = 0
P_CKV = P_CQ + MLA_Q_LORA
P_KPA = P_CKV + MLA_KV_LORA
P_KPB = P_KPA + 128
P_HG = P_KPB + 128
HG_IN_WIDTH = 3 * HG_KW + 2 * HG_WIDTH
P_FN = P_HG + HG_IN_WIDTH
P_WIDTH = P_FN + FN_WIDTH

STREAM_TILE = 768
LATENT_TILE = 1024
ATTN_TILE = 512
FOURIER_TILE = 1024
FFN_TILE = 256
VMEM_LIMIT = 56 * 1024 * 1024

BF16 = jnp.bfloat16
F32 = jnp.float32


def _dot(a, b):
    return jnp.dot(a, b, preferred_element_type=F32)


def _dot_nt(a, b):
    return lax.dot_general(a, b, (((1,), (1,)), ((), ())), preferred_element_type=F32)


def _dot_tn(a, b):
    return lax.dot_general(a, b, (((0,), (0,)), ((), ())), preferred_element_type=F32)


def _silu(x):
    return x * (1.0 / (1.0 + jnp.exp(-x)))


def _sigmoid(x):
    return 1.0 / (1.0 + jnp.exp(-x))


def _split3(x):
    hi = x.astype(BF16)
    r1 = x - hi.astype(F32)
    mid = r1.astype(BF16)
    lo = (r1 - mid.astype(F32)).astype(BF16)
    return hi, mid, lo


def _const_spec(shape):
    nd = len(shape)
    return pl.BlockSpec(shape, lambda *_: (0,) * nd)


_RESIDENT = pl.BlockSpec(memory_space=pltpu.VMEM)


def _params(n_axes):
    return pltpu.CompilerParams(dimension_semantics=("arbitrary",) * n_axes, vmem_limit_bytes=VMEM_LIMIT)


def _mod_kernel(c_ref, w_ref, b_ref, o_ref):
    act = _silu(c_ref[...]).astype(BF16)
    o_ref[0] = _dot(act, w_ref[0].astype(BF16)) + b_ref[0]


def _modulation(c_all, w_mod, b_mod):
    depth, d, n6 = w_mod.shape
    rows = c_all.shape[0]
    tn = n6 // 6
    return pl.pallas_call(
        _mod_kernel,
        grid=(depth, n6 // tn),
        in_specs=[
            pl.BlockSpec((rows, d), lambda l, j: (0, 0)),
            pl.BlockSpec((1, d, tn), lambda l, j: (l, 0, j)),
            pl.BlockSpec((1, 1, tn), lambda l, j: (l, 0, j)),
        ],
        out_specs=pl.BlockSpec((1, rows, tn), lambda l, j: (l, 0, j)),
        out_shape=jax.ShapeDtypeStruct((depth, rows, n6), F32),
        name="modulation",
    )(c_all, w_mod, b_mod.reshape(depth, 1, n6))


def _lower_bound_kernel(lb_ref, o_ref):
    depth = lb_ref.shape[0]
    lp = [lb_ref[l] for l in range(depth)]
    m = lp[0]
    for l in range(1, depth):
        m = jnp.maximum(m, lp[l])
    e = [jnp.exp(v - m) for v in lp]
    tot = e[0]
    for l in range(1, depth):
        tot = tot + e[l]
    probs = [v / tot for v in e]
    cum = probs[0]
    o_ref[0] = cum - probs[0]
    for l in range(1, depth):
        cum = cum + probs[l]
        o_ref[l] = cum - probs[0]


def _lower_bounds(lb_param):
    return pl.pallas_call(
        _lower_bound_kernel,
        out_shape=jax.ShapeDtypeStruct(lb_param.shape, F32),
        name="hgrn_lower_bounds",
    )(lb_param.astype(F32))


def _modulated(tile_rows, n_lat, modc_ref, modb_ref, fn, *arrays, use_ctx=True):
    lat = modb_ref[0]
    if not use_ctx:
        return fn(lat, *arrays)
    tail = n_lat % tile_rows
    if tail == 0:
        return fn(jnp.where(pl.program_id(1) * tile_rows >= n_lat, modc_ref[0], lat), *arrays)
    is_last = pl.program_id(1) == pl.num_programs(1) - 1
    mixed = jnp.where(is_last, modc_ref[0], lat)
    return jnp.concatenate([fn(lat, *[a[0:tail] for a in arrays]),
                            fn(mixed, *[a[tail:tile_rows] for a in arrays])], axis=0)


def _stream_specs(src, tm):
    if not isinstance(src, tuple):
        return [pl.BlockSpec((1, tm, src.shape[2]), lambda i, j: (i, j, 0))], [src]
    lat, ctx = src
    t, d = lat.shape[1:]
    n_ctx = ctx.shape[1]
    n_full = t // tm
    tail = t - n_full * tm
    assert n_full >= 1 and tail > 0 and tail + n_ctx == tm and (n_full * tm) % tail == 0
    specs = [pl.BlockSpec((1, tm, d), lambda i, j: (i, jnp.minimum(j, n_full - 1), 0)),
             pl.BlockSpec((1, tail, d), lambda i, j: (i, (n_full * tm) // tail, 0)),
             pl.BlockSpec((1, n_ctx, d), lambda i, j: (i, 0, 0))]
    return specs, [lat, lat, ctx]


def _stream_tile(refs):
    if len(refs) == 1:
        return refs[0][0]
    main_ref, tail_ref, ctx_ref = refs
    mixed = jnp.concatenate([tail_ref[0], ctx_ref[0]], axis=0)
    return jnp.where(pl.program_id(1) < pl.num_programs(1) - 1, main_ref[0], mixed)


def _inproj_kernel(*refs, q_scale, n_lat, n_src):
    (modb_ref, modc_ref, g1_ref, w_in_ref, qg_ref, wq_ref, kvg_ref, wk_ref, wvt_ref, cos_ref, sin_ref,
     q_ref, k_ref, vt_ref, hg_ref, fn_ref) = refs[n_src:]
    x = _stream_tile(refs[:n_src])
    tm = x.shape[0]
    xn = x * lax.rsqrt(jnp.mean(x * x, axis=-1, keepdims=True) + EPS) * g1_ref[...]
    xm = _modulated(tm, n_lat, modc_ref, modb_ref,
                    lambda mod, rows: (rows * (1.0 + mod[1:2, :]) + mod[0:1, :]).astype(BF16), xn)

    hg_ref[0] = _dot(xm, w_in_ref[:, P_HG:P_FN]).astype(BF16)
    fn_ref[0] = _dot(xm, w_in_ref[:, P_FN:P_WIDTH]).astype(BF16)
    p = _dot(xm, w_in_ref[:, 0:P_HG])

    cos = cos_ref[...]
    sin = sin_ref[...]

    cq = p[:, P_CQ:P_CKV]
    cqn = (cq * lax.rsqrt(jnp.mean(cq * cq, axis=-1, keepdims=True) + EPS) * qg_ref[...]).astype(BF16)
    qq = _dot(cqn, wq_ref[...])
    nw = MLA_HEADS * MLA_NOPE
    for h in range(MLA_HEADS):
        q_ref[0, :, h * HEAD_PAD:h * HEAD_PAD + 128] = (qq[:, h * 128:(h + 1) * 128] * q_scale).astype(BF16)
        pe = qq[:, nw + h * 128:nw + (h + 1) * 128] * cos + qq[:, 2 * nw + h * 128:2 * nw + (h + 1) * 128] * sin
        q_ref[0, :, h * HEAD_PAD + 128:(h + 1) * HEAD_PAD] = (pe * q_scale).astype(BF16)

    ckv = p[:, P_CKV:P_KPA]
    ckvn = (ckv * lax.rsqrt(jnp.mean(ckv * ckv, axis=-1, keepdims=True) + EPS) * kvg_ref[...]).astype(BF16)
    kv = _dot(ckvn, wk_ref[...])
    vt_ref[0] = _dot_nt(wvt_ref[...], ckvn).astype(BF16)
    kpe = (p[:, P_KPA:P_KPB] * cos + p[:, P_KPB:P_HG] * sin).astype(BF16)
    for h in range(MLA_HEADS):
        k_ref[0, :, h * HEAD_PAD:h * HEAD_PAD + 128] = kv[:, h * 128:(h + 1) * 128].astype(BF16)
        k_ref[0, :, h * HEAD_PAD + 128:(h + 1) * HEAD_PAD] = kpe


def _inproj(src, mod, g1, w_in_p, qg, wq_all, kvg, wk_p, wvt_p, cos_t, sin_t, n_lat, ctx_row):
    s = cos_t.shape[0]
    d = g1.shape[0]
    b = (src[0] if isinstance(src, tuple) else src).shape[0]
    tm = STREAM_TILE
    row = lambda w: pl.BlockSpec((1, tm, w), lambda i, j: (i, j, 0))
    src_specs, src_args = _stream_specs(src, tm)
    return pl.pallas_call(
        functools.partial(_inproj_kernel, q_scale=math.log2(math.e) / math.sqrt(MLA_NOPE + MLA_ROPE), n_lat=n_lat,
                          n_src=len(src_args)),
        grid=(b, s // tm),
        in_specs=src_specs + [
            pl.BlockSpec((1, 6, d), lambda i, j: (i, 0, 0)),
            pl.BlockSpec((1, 6, d), lambda i, j: (ctx_row, 0, 0)),
            _const_spec((1, d)),
            _RESIDENT,
            _const_spec((1, MLA_Q_LORA)),
            _RESIDENT,
            _const_spec((1, MLA_KV_LORA)),
            _RESIDENT,
            _RESIDENT,
            pl.BlockSpec((tm, 128), lambda i, j: (j, 0)),
            pl.BlockSpec((tm, 128), lambda i, j: (j, 0)),
        ],
        out_specs=[row(MLA_HEADS * HEAD_PAD), row(MLA_HEADS * HEAD_PAD),
                   pl.BlockSpec((1, MLA_WIDTH, tm), lambda i, j: (i, 0, j)), row(HG_IN_WIDTH), row(FN_WIDTH)],
        out_shape=[
            jax.ShapeDtypeStruct((b, s, MLA_HEADS * HEAD_PAD), BF16),
            jax.ShapeDtypeStruct((b, s, MLA_HEADS * HEAD_PAD), BF16),
            jax.ShapeDtypeStruct((b, MLA_WIDTH, s), BF16),
            jax.ShapeDtypeStruct((b, s, HG_IN_WIDTH), BF16),
            jax.ShapeDtypeStruct((b, s, FN_WIDTH), BF16),
        ],
        compiler_params=_params(2),
        name="inproj",
    )(*src_args, mod, mod, g1.reshape(1, d), w_in_p, qg.reshape(1, -1), wq_all, kvg.reshape(1, -1), wk_p, wvt_p,
      cos_t, sin_t)


def _attend(q_ref, k_ref, vt_ref, o_ref, key_lo, key_hi):
    def scores(h):
        qh = q_ref[0, :, h * HEAD_PAD:(h + 1) * HEAD_PAD]
        kh = k_ref[0, key_lo:key_hi, h * HEAD_PAD:(h + 1) * HEAD_PAD]
        return _dot_nt(kh, qh)

    all_scores = [scores(h) for h in range(MLA_HEADS)]
    probs = []
    for st in all_scores:
        e = jnp.exp2(st - jnp.max(st, axis=0, keepdims=True))
        probs.append((e.astype(BF16), 1.0 / jnp.sum(e, axis=0, keepdims=True)))
    for h, (e, inv_denom) in enumerate(probs):
        vth = vt_ref[0, h * MLA_V:(h + 1) * MLA_V, key_lo:key_hi]
        ot = _dot(vth, e) * inv_denom
        o_ref[0, :, h * MLA_V:(h + 1) * MLA_V] = ot.T.astype(BF16)


def _attn_kernel(q_ref, k_ref, vt_ref, o_ref):
    _attend(q_ref, k_ref, vt_ref, o_ref, 0, k_ref.shape[1])


def _attn_ctx_kernel(q_ref, qc_ref, k_ref, vt_ref, o_ref, oc_ref, *, n_lat):
    s_len = k_ref.shape[1]
    _attend(q_ref, k_ref, vt_ref, o_ref, 0, s_len)

    @pl.when(pl.program_id(1) == pl.num_programs(1) - 1)
    def _():
        _attend(qc_ref, k_ref, vt_ref, oc_ref, n_lat, s_len)


def _attention(q, k, vt, n_lat, with_ctx_queries):
    b, s, _ = q.shape
    tq = ATTN_TILE
    n_ctx = s - n_lat
    q_spec = pl.BlockSpec((1, tq, q.shape[2]), lambda i, j: (i, j, 0))
    kv_specs = [pl.BlockSpec((1, s, k.shape[2]), lambda i, j: (i, 0, 0)),
                pl.BlockSpec((1, vt.shape[1], s), lambda i, j: (i, 0, 0))]
    o_spec = pl.BlockSpec((1, tq, MLA_WIDTH), lambda i, j: (i, j, 0))
    o_shape = jax.ShapeDtypeStruct((b, n_lat, MLA_WIDTH), BF16)
    if not with_ctx_queries:
        out = pl.pallas_call(
            _attn_kernel,
            grid=(b, n_lat // tq),
            in_specs=[q_spec] + kv_specs,
            out_specs=o_spec,
            out_shape=o_shape,
            compiler_params=_params(2),
            name="attention",
        )(q, k, vt)
        return out, None
    assert n_lat % n_ctx == 0
    return pl.pallas_call(
        functools.partial(_attn_ctx_kernel, n_lat=n_lat),
        grid=(b, n_lat // tq),
        in_specs=[q_spec, pl.BlockSpec((1, n_ctx, q.shape[2]), lambda i, j: (i, n_lat // n_ctx, 0))] + kv_specs,
        out_specs=[o_spec, pl.BlockSpec((1, n_ctx, MLA_WIDTH), lambda i, j: (i, 0, 0))],
        out_shape=[o_shape, jax.ShapeDtypeStruct((b, n_ctx, MLA_WIDTH), BF16)],
        compiler_params=_params(2),
        name="attention_ctx",
    )(q, q, k, vt)


def _hgrn_constants():
    c = HG_CHUNK
    t = np.arange(c)[:, None]
    u = np.arange(c)[None, :]
    blocks = [(u <= t)]
    for m in (8,):
        half = m // 2
        mid = (t // m) * m + half
        lower = (t % m) >= half
        blocks.append(np.where(lower, (u >= mid) & (u <= t), (u >= t + 1) & (u <= mid - 1)))
    mf = np.stack([blk.astype(np.float32) for blk in blocks])
    mb = np.ascontiguousarray(mf[:, ::-1, ::-1])
    msb = np.floor(np.log2(np.maximum(t ^ u, 1))).astype(np.int32)
    lvl = np.where(t > u, HG_LEVELS - 1 - msb, np.where(t == u, HG_LEVELS, -1)).astype(np.int32)
    lvl_f = np.tile(lvl, (1, 2))
    lvl_b = np.tile(np.ascontiguousarray(lvl.T), (1, 2))
    return mf.reshape(-1, c), mb.reshape(-1, c), lvl_f, lvl_b


def _pair_weights(x, p):
    zeros = jnp.zeros((x.shape[0], HG_K), x.dtype)
    xa = x[:, 2 * p * HG_K:(2 * p + 1) * HG_K]
    xb = x[:, (2 * p + 1) * HG_K:(2 * p + 2) * HG_K]
    blk = jnp.concatenate([jnp.concatenate([xa, zeros], axis=1), jnp.concatenate([zeros, xb], axis=1)], axis=0)
    return blk.T


def _hgrn_gates(z, lb, m_ref, with_output):
    f = lb + (1.0 - lb) * _sigmoid(z)
    g = jnp.log2(f)
    g_hi = g.astype(BF16)
    g_lo = (g - g_hi.astype(F32)).astype(BF16)
    m_all = m_ref[...] if with_output else m_ref[0:HG_CHUNK, :]
    return f, _dot(m_all, g_hi) + _dot(m_all, g_lo)


def _hgrn_decays(hq, f, eb, fwd, with_output):
    c = HG_CHUNK
    kk = 1.0 - f
    bc = eb[0:c]
    total = bc[c - 1:c] if fwd else bc[0:1]
    k_up = (kk * jnp.exp2(total - bc)).astype(BF16)
    chunk_decay = jnp.exp2(total)
    if not with_output:
        return k_up, chunk_decay, None
    qq = _silu(hq)
    row = lax.broadcasted_iota(jnp.int32, (c, HG_KW), 0)
    xs = []
    for l in range(HG_LEVELS):
        m = c >> l
        half = m // 2
        bit = (row & half) != 0
        is_query = bit if fwd else jnp.logical_not(bit)
        if m == 2:
            xs.append(jnp.where(is_query, qq * f, kk).astype(BF16))
            continue
        if m == 4:
            f_keys = pltpu.roll(f, 1 if fwd else c - 1, 0)
            f_qrys = pltpu.roll(f, c - 1 if fwd else 1, 0)
            far = ((row & 1) != 0) if fwd else ((row & 1) == 0)
            q_part = qq * f
            xs.append(jnp.where(is_query, jnp.where(far, q_part * f_keys, q_part),
                                jnp.where(far, kk, kk * f_qrys)).astype(BF16))
            continue
        if m >= 16:
            refs = [bc[b0 + half - 1:b0 + half] if fwd else bc[b0 + half:b0 + half + 1] for b0 in range(0, c, m)]
            ref = jnp.concatenate([jnp.broadcast_to(r, (m, HG_KW)) for r in refs], axis=0)
            diff = bc - ref
            expo = jnp.where(is_query, diff, -diff)
        else:
            idx = 1 + l - (HG_LEVELS - 2 - HG_MATMUL_LEVELS)
            expo = eb[idx * c:(idx + 1) * c]
        xs.append((jnp.where(is_query, qq, kk) * jnp.exp2(expo)).astype(BF16))
    q_in = (qq * jnp.exp2(bc)).astype(BF16)
    return k_up, chunk_decay, (xs, qq.astype(BF16), kk.astype(BF16), q_in)


def _hgrn_scores(operands, lvl_masks):
    xs, qb, kb, _ = operands
    scores = []
    for p in range(HG_HEADS // 2):
        kp = slice(2 * p * HG_K, (2 * p + 2) * HG_K)
        sc = jnp.where(lvl_masks[HG_LEVELS], _dot(qb[:, kp], _pair_weights(kb, p)), 0.0)
        for l in range(HG_LEVELS):
            sc = jnp.where(lvl_masks[l], _dot(xs[l][:, kp], _pair_weights(xs[l], p)), sc)
        scores.append(sc.astype(BF16))
    return scores


def _hgrn_apply(scores, operands, k_up, chunk_decay, vv, v_lo, v_hi, state_ref):
    outs = []
    for p in range(HG_HEADS // 2):
        kp = slice(2 * p * HG_K, (2 * p + 2) * HG_K)
        vp = slice(2 * p * HG_V, (2 * p + 2) * HG_V)
        if scores is not None:
            v_pair = vv[:, vp]
            v_diag = jnp.concatenate([v_pair * v_lo, v_pair * v_hi], axis=0)
            st_pair = state_ref[2 * p * HG_V:(2 * p + 2) * HG_V, kp]
            outs.append(_dot(scores[p], v_diag) + _dot(operands[3][:, kp], st_pair.astype(BF16).T))
        for h in (2 * p, 2 * p + 1):
            ks = slice(h * HG_K, (h + 1) * HG_K)
            vs = slice(h * HG_V, (h + 1) * HG_V)
            state_ref[vs, ks] = chunk_decay[:, ks] * state_ref[vs, ks] + _dot_tn(vv[:, vs], k_up[:, ks])
    return jnp.concatenate(outs, axis=1) if outs else None


def _hgrn_kernel(x_ref, lb_ref, gain_ref, mf_ref, mb_ref, lvlf_ref, lvlb_ref, ones_ref, o_ref,
                 of_ref, ob_ref, sf_ref, sb_ref, *, n_lat_chunks):
    c = HG_CHUNK
    s_len = x_ref.shape[1]
    n_chunks = s_len // c
    n_ctx_chunks = n_chunks - n_lat_chunks
    sf_ref[...] = jnp.zeros_like(sf_ref)
    sb_ref[...] = jnp.zeros_like(sb_ref)
    lvl_f = lvlf_ref[...]
    lvl_b = lvlb_ref[...]
    masks_f = [lvl_f == l for l in range(HG_LEVELS + 1)]
    masks_b = [lvl_b == l for l in range(HG_LEVELS + 1)]
    lb_f = lb_ref[0:1, :]
    lb_b = lb_ref[1:2, :]
    v_lane = lax.broadcasted_iota(jnp.int32, (c, 2 * HG_V), 1)
    v_lo = (v_lane < HG_V).astype(F32).astype(BF16)
    v_hi = (v_lane >= HG_V).astype(F32).astype(BF16)

    def load(ci):
        r0 = pl.multiple_of(ci * c, c)
        hq = x_ref[0, pl.ds(r0, c), 0:HG_KW].astype(F32)
        vv = x_ref[0, pl.ds(r0, c), 3 * HG_KW:3 * HG_KW + HG_WIDTH]
        return r0, hq, vv

    def scan(first_fwd, first_bwd, n, with_output):
        per_trip = HG_UNROLL if n % HG_UNROLL == 0 else n

        def step(i, carry):
            gated = []
            for u in range(per_trip):
                for fwd in (True, False):
                    ci = first_fwd + i * per_trip + u if fwd else first_bwd - i * per_trip - u
                    r0, hq, vv = load(ci)
                    zcol = HG_KW if fwd else 2 * HG_KW
                    z = x_ref[0, pl.ds(r0, c), zcol:zcol + HG_KW].astype(F32)
                    gated.append((fwd, r0, hq, vv, _hgrn_gates(z, lb_f if fwd else lb_b, mf_ref if fwd else mb_ref,
                                                               with_output)))
            jobs = [(fwd, r0, vv, _hgrn_decays(hq, f, eb, fwd, with_output)) for fwd, r0, hq, vv, (f, eb) in gated]
            scores = [_hgrn_scores(ph[2], masks_f if fwd else masks_b) if with_output else None
                      for fwd, _, _, ph in jobs]
            for (fwd, r0, vv, (k_up, chunk_decay, operands)), sc in zip(jobs, scores):
                out = _hgrn_apply(sc, operands, k_up, chunk_decay, vv, v_lo, v_hi, sf_ref if fwd else sb_ref)
                if with_output:
                    (of_ref if fwd else ob_ref)[pl.ds(r0, c), :] = out
            return carry

        lax.fori_loop(0, n // per_trip, step, 0)

    scan(n_lat_chunks, n_chunks - 1, n_ctx_chunks, o_ref.shape[1] == s_len)
    scan(0, n_lat_chunks - 1, n_lat_chunks, True)

    rt = 256
    gain = gain_ref[...]
    ones = ones_ref[...]

    def readout(i, carry):
        r0 = pl.multiple_of(i * rt, rt)
        o = of_ref[pl.ds(r0, rt), :] + ob_ref[pl.ds(r0, rt), :]
        sq_hi, sq_mid, sq_lo = _split3(o * o)
        msq = _dot(sq_hi, ones) + _dot(sq_mid, ones) + _dot(sq_lo, ones)
        zg = x_ref[0, pl.ds(r0, rt), 3 * HG_KW + HG_WIDTH:HG_IN_WIDTH].astype(F32)
        y = o * lax.rsqrt(msq + EPS) * gain * _silu(zg)
        o_ref[0, pl.ds(r0, rt), :] = y.astype(BF16)
        return carry

    lax.fori_loop(0, o_ref.shape[1] // rt, readout, 0)


def _hgrn(hg_in, lb, gain, n_lat, with_ctx_outputs):
    b, s, w = hg_in.shape
    mf, mb, lvl_f, lvl_b = _hgrn_constants()
    ones = np.kron(np.eye(HG_HEADS, dtype=np.float32), np.full((HG_V, HG_V), 1.0 / HG_V, np.float32))
    n_out = s if with_ctx_outputs else n_lat
    return pl.pallas_call(
        functools.partial(_hgrn_kernel, n_lat_chunks=n_lat // HG_CHUNK),
        grid=(b,),
        in_specs=[
            pl.BlockSpec((1, s, w), lambda i: (i, 0, 0)),
            _const_spec((2, HG_KW)),
            _const_spec((1, HG_WIDTH)),
            _const_spec(mf.shape),
            _const_spec(mb.shape),
            _const_spec(lvl_f.shape),
            _const_spec(lvl_b.shape),
            _const_spec(ones.shape),
        ],
        out_specs=pl.BlockSpec((1, n_out, HG_WIDTH), lambda i: (i, 0, 0)),
        out_shape=jax.ShapeDtypeStruct((b, n_out, HG_WIDTH), BF16),
        scratch_shapes=[
            pltpu.VMEM((s, HG_WIDTH), F32),
            pltpu.VMEM((s, HG_WIDTH), F32),
            pltpu.VMEM((HG_WIDTH, HG_KW), F32),
            pltpu.VMEM((HG_WIDTH, HG_KW), F32),
        ],
        compiler_params=_params(1),
        name="hgrn2",
    )(hg_in, lb, gain.reshape(1, HG_WIDTH), jnp.asarray(mf, BF16), jnp.asarray(mb, BF16),
      jnp.asarray(lvl_f), jnp.asarray(lvl_b), jnp.asarray(ones, BF16))


def _dft_tables(n):
    idx = np.arange(n, dtype=np.int64)
    ang = 2.0 * np.pi * ((idx[:, None] * idx[None, :]) % n).astype(np.float64) / n
    return np.cos(ang).astype(np.float32), np.sin(ang).astype(np.float32)


def _fourier_tile(z, table, mix, w, norm):
    tn = table.shape[0] // 2
    zz = _dot(table, z)
    zcat = jnp.concatenate([zz[0:tn], zz[tn:2 * tn]], axis=1).astype(BF16)
    mixed = _dot(zcat, mix) * norm
    return _dot(mixed.astype(BF16), w).astype(BF16)


def _fourier_kernel(z_ref, tl_ref, tc_ref, mix_ref, w_ref, o_ref, *maybe_oc_ref, n_lat):
    s_len = z_ref.shape[1]
    n_ctx = s_len - n_lat
    mix = mix_ref[...]
    w = w_ref[...]
    table = tl_ref[pl.program_id(1)]
    o_ref[0] = _fourier_tile(z_ref[0, 0:n_lat, :], table, mix, w, 1.0 / math.sqrt(n_lat * FN_GROUP_DIM))
    if maybe_oc_ref:
        @pl.when(pl.program_id(1) == pl.num_programs(1) - 1)
        def _():
            maybe_oc_ref[0][0] = _fourier_tile(z_ref[0, n_lat:s_len, :], tc_ref[...], mix, w,
                                               1.0 / math.sqrt(n_ctx * FN_GROUP_DIM))


def _stacked_dft_table(n, tile):
    cos, sin = _dft_tables(n)
    return np.concatenate([cos.reshape(n // tile, tile, n), sin.reshape(n // tile, tile, n)], axis=1)


def _fourier_tables(n_lat, n_ctx):
    cgrp, sgrp = _dft_tables(FN_GROUP_DIM)
    eye = np.eye(FN_GROUPS, dtype=np.float32)
    mix = np.concatenate([np.kron(eye, cgrp), -np.kron(eye, sgrp)], axis=0)
    tables = (_stacked_dft_table(n_lat, FOURIER_TILE), _stacked_dft_table(n_ctx, n_ctx)[0], mix)
    return tuple(jnp.asarray(t).astype(BF16) for t in tables)


def _fourier(fn_in, tables, w_fourier, n_lat, with_ctx_outputs):
    b, s, w = fn_in.shape
    n_ctx = s - n_lat
    tn = FOURIER_TILE
    table_lat, table_ctx, mix = tables
    out_specs = [pl.BlockSpec((1, tn, FN_WIDTH), lambda i, j: (i, j, 0))]
    out_shape = [jax.ShapeDtypeStruct((b, n_lat, FN_WIDTH), BF16)]
    if with_ctx_outputs:
        out_specs.append(pl.BlockSpec((1, n_ctx, FN_WIDTH), lambda i, j: (i, 0, 0)))
        out_shape.append(jax.ShapeDtypeStruct((b, n_ctx, FN_WIDTH), BF16))
    outs = pl.pallas_call(
        functools.partial(_fourier_kernel, n_lat=n_lat),
        grid=(b, n_lat // tn),
        in_specs=[
            pl.BlockSpec((1, s, w), lambda i, j: (i, 0, 0)),
            _RESIDENT,
            _const_spec(table_ctx.shape),
            _const_spec(mix.shape),
            _const_spec(w_fourier.shape),
        ],
        out_specs=out_specs,
        out_shape=out_shape,
        compiler_params=_params(2),
        name="fourier",
    )(fn_in, table_lat, table_ctx, mix, w_fourier)
    return (outs[0], outs[1]) if with_ctx_outputs else (outs[0], None)


def _ffn_kernel(*refs, final_norm, n_lat, select_ctx, n_srcs):
    streams = []
    pos = 0
    for n in n_srcs:
        streams.append(_stream_tile(refs[pos:pos + n]))
        pos += n
    x, attn, hg, fn = streams
    modb_ref, modc_ref, g2_ref, wo_ref, wgu_ref, wd_ref, gf_ref, o_ref, act_ref = refs[pos:]
    tm = x.shape[0]

    modulated = functools.partial(_modulated, tm, n_lat, modc_ref, modb_ref, use_ctx=select_ctx)

    y = (_dot(attn, wo_ref[0:MLA_WIDTH, :])
         + _dot(hg, wo_ref[MLA_WIDTH:MLA_WIDTH + HG_WIDTH, :])
         + _dot(fn, wo_ref[MLA_WIDTH + HG_WIDTH:MLA_WIDTH + HG_WIDTH + FN_WIDTH, :]))
    x1 = modulated(lambda mod, xr, yr: xr + mod[2:3, :] * yr, x, y)
    o_ref[0] = x1
    xn = x1 * lax.rsqrt(jnp.mean(x1 * x1, axis=-1, keepdims=True) + EPS) * g2_ref[...]
    hm = modulated(lambda mod, rows: (rows * (1.0 + mod[4:5, :]) + mod[3:4, :]).astype(BF16), xn)

    ft = wgu_ref.shape[2] // 2
    for t in range(wgu_ref.shape[0]):
        gu = _dot(hm, wgu_ref[t])
        act_ref[:, t * ft:(t + 1) * ft] = (_silu(gu[:, 0:ft]) * gu[:, ft:2 * ft]).astype(BF16)
    out = modulated(lambda mod, xr, ar: xr + mod[5:6, :] * ar, o_ref[0], _dot(act_ref[...], wd_ref[...]))
    if final_norm:
        out = out * lax.rsqrt(jnp.mean(out * out, axis=-1, keepdims=True) + EPS) * gf_ref[...]
    o_ref[0] = out


def _outproj_ffn(x_src, attn, hg, fn, mod, g2, w_out, w_gu, w_d, g_final, n_lat, n_ctx, ctx_row, with_ctx_rows,
                 final_norm):
    d = g2.shape[0]
    b = (x_src[0] if isinstance(x_src, tuple) else x_src).shape[0]
    tm = STREAM_TILE if with_ctx_rows else LATENT_TILE
    n_rows = n_lat + n_ctx if with_ctx_rows else n_lat
    hidden = w_d.shape[0]
    row = lambda w: pl.BlockSpec((1, tm, w), lambda i, j: (i, j, 0))
    src_specs, src_args, n_srcs = [], [], []
    for src in (x_src, attn, hg, fn):
        specs, args = _stream_specs(src, tm)
        src_specs += specs
        src_args += args
        n_srcs.append(len(args))
    return pl.pallas_call(
        functools.partial(_ffn_kernel, final_norm=final_norm, n_lat=n_lat, select_ctx=with_ctx_rows,
                          n_srcs=tuple(n_srcs)),
        grid=(b, n_rows // tm),
        in_specs=src_specs + [
            pl.BlockSpec((1, 6, d), lambda i, j: (i, 0, 0)),
            pl.BlockSpec((1, 6, d), lambda i, j: (ctx_row, 0, 0)),
            _const_spec((1, d)),
            _RESIDENT,
            _RESIDENT,
            _RESIDENT,
            _const_spec((1, d)),
        ],
        out_specs=row(d),
        out_shape=jax.ShapeDtypeStruct((b, n_rows, d), F32),
        scratch_shapes=[pltpu.VMEM((tm, hidden), BF16)],
        compiler_params=_params(2),
        name="outproj_ffn",
    )(*src_args, mod, mod, g2.reshape(1, d), w_out, w_gu, w_d, g_final.reshape(1, d))


def _rotate_half_cols(w):
    r1, r2, c1, c2 = jnp.split(w, 4, axis=-1)
    return jnp.concatenate([-r2, r1, -c2, c1], axis=-1)


def _pack_w_in(w):
    d = w.shape[0]
    zeros = jnp.zeros((d, 128 - MLA_ROPE), w.dtype)
    w_kr = w[:, O_KR:O_HQ]
    return jnp.concatenate([w[:, O_CQ:O_KR], w_kr, zeros, _rotate_half_cols(w_kr), zeros, w[:, O_HQ:]],
                           axis=1).astype(BF16)


def _pack_w_uq(w):
    r = w.shape[0]
    wh = w.reshape(r, MLA_HEADS, MLA_NOPE + MLA_ROPE)
    nope = wh[:, :, :MLA_NOPE].reshape(r, MLA_HEADS * MLA_NOPE)
    pe = wh[:, :, MLA_NOPE:]
    zeros = jnp.zeros((r, MLA_HEADS, 128 - MLA_ROPE), w.dtype)
    pe_a = jnp.concatenate([pe, zeros], axis=-1).reshape(r, MLA_HEADS * 128)
    pe_b = jnp.concatenate([_rotate_half_cols(pe), zeros], axis=-1).reshape(r, MLA_HEADS * 128)
    return jnp.concatenate([nope, pe_a, pe_b], axis=1).astype(BF16)


def _pack_w_ukv(w):
    r = w.shape[0]
    wh = w.reshape(r, MLA_HEADS, MLA_NOPE + MLA_V)
    k_nope = wh[:, :, :MLA_NOPE].reshape(r, MLA_HEADS * MLA_NOPE)
    v = wh[:, :, MLA_NOPE:].reshape(r, MLA_HEADS * MLA_V)
    return k_nope.astype(BF16), v.T.astype(BF16)


def _pack_w_gate_up(w, hidden):
    d = w.shape[0]
    n_ft = hidden // FFN_TILE
    gate = w[:, :hidden].reshape(d, n_ft, FFN_TILE)
    up = w[:, hidden:].reshape(d, n_ft, FFN_TILE)
    return jnp.concatenate([gate, up], axis=2).transpose(1, 0, 2).astype(BF16)


def _rope_tables(n_lat, n_ctx):
    rows = n_lat // GRID_W
    row_pos = np.repeat(np.arange(rows, dtype=np.float32), GRID_W)
    col_pos = np.tile(np.arange(GRID_W, dtype=np.float32), rows)
    axis_dim = MLA_ROPE // 2
    inv_freq = (ROPE_BASE ** (-np.arange(0, axis_dim, 2, dtype=np.float32) / axis_dim)).astype(np.float32)
    ang_r = row_pos[:, None] * inv_freq
    ang_c = col_pos[:, None] * inv_freq
    ang = np.concatenate([ang_r, ang_r, ang_c, ang_c], axis=-1)
    cos = np.ones((n_lat + n_ctx, 128), np.float32)
    sin = np.zeros((n_lat + n_ctx, 128), np.float32)
    cos[:n_lat, :MLA_ROPE] = np.cos(ang)
    sin[:n_lat, :MLA_ROPE] = np.sin(ang)
    return jnp.asarray(cos), jnp.asarray(sin)


def kernel(x, c, ctx, c_ctx, w_mod, b_mod, norm1_g, norm2_g, w_in, q_norm_g, w_uq, kv_norm_g, w_ukv, lb_param,
           hg_norm_g, w_fourier, w_out, w_gate_up, w_down, final_norm_g):
    b, t, d = x.shape
    n_ctx = ctx.shape[1]
    s = t + n_ctx
    depth = w_mod.shape[0]
    hidden = w_down.shape[1]
    assert s % STREAM_TILE == 0 and t % LATENT_TILE == 0 and t % ATTN_TILE == 0 and t % FOURIER_TILE == 0
    assert hidden % FFN_TILE == 0 and t % GRID_W == 0 and n_ctx % HG_CHUNK == 0

    mod_rows = -(-(b + 1) // 8) * 8
    c_all = jnp.zeros((mod_rows, d), F32).at[:b].set(c).at[b].set(c_ctx)
    mod_all = _modulation(c_all, w_mod, b_mod).reshape(depth, mod_rows, 6, d)
    lower = _lower_bounds(lb_param)
    cos_t, sin_t = _rope_tables(t, n_ctx)
    dft_tables = _fourier_tables(t, n_ctx)

    xa = (x, ctx)
    for l in range(depth):
        last = l == depth - 1
        q, k, v, hg_in, fn_in = _inproj(xa, mod_all[l], norm1_g[l], _pack_w_in(w_in[l]), q_norm_g[l],
                                        _pack_w_uq(w_uq[l]), kv_norm_g[l], *_pack_w_ukv(w_ukv[l]), cos_t, sin_t, t, b)
        attn, attn_ctx = _attention(q, k, v, t, not last)
        hg = _hgrn(hg_in, lower[l], hg_norm_g[l], t, not last)
        fn, fn_ctx = _fourier(fn_in, dft_tables, w_fourier[l].astype(BF16), t, not last)
        if not last:
            attn, fn = (attn, attn_ctx), (fn, fn_ctx)
        xa = _outproj_ffn(xa, attn, hg, fn, mod_all[l], norm2_g[l], w_out[l].astype(BF16),
                          _pack_w_gate_up(w_gate_up[l], hidden), w_down[l].astype(BF16), final_norm_g, t, n_ctx, b,
                          not last, last)
    return xa
```
